```python
import jax, jax.numpy as jnp
from jax import lax
import numpy as np

D_MODEL = 2048
BATCH = 2
SEQ = 4096
DEPTH = 1
DEC_BATCH = 32
DEC_SEQ = 1
PAST_LEN = 8192
PAGE_SIZE = 128

ATTN_HEAD_DIM = 64
ATTN_WIDTH = D_MODEL // 2
ATTN_HEADS = ATTN_WIDTH // ATTN_HEAD_DIM
DIL_PATTERNS = ((128, 1), (512, 4), (2048, 16))
ATTN_WINDOW_MAX = 2048
Q_BLOCK = 128
RWKV_HEAD_DIM = 64
RWKV_WIDTH = D_MODEL - ATTN_WIDTH
RWKV_HEADS = RWKV_WIDTH // RWKV_HEAD_DIM
RWKV_DECAY_RANK = 64
RWKV_A_RANK = 64
RWKV_GATE_RANK = 160
RWKV_PROJ = 3 * RWKV_WIDTH + RWKV_DECAY_RANK + RWKV_A_RANK + RWKV_GATE_RANK
PROJ_WIDTH = 3 * ATTN_WIDTH + RWKV_PROJ
RWKV_GN_EPS = RWKV_HEAD_DIM * 1e-5
NORM_EPS = 1e-6
PEER_HEADS = 8
PEER_N_KEYS = 128
PEER_N_EXPERTS = PEER_N_KEYS * PEER_N_KEYS
PEER_QUERY_DIM = 256
PEER_HALF = PEER_QUERY_DIM // 2
PEER_TOPK = 16
PEER_BLOCK = 128

kernel_name = 'dilated_rwkv7_peer_hybrid'


def _rmsnorm(x, g):
    xf = x.astype(jnp.float32)
    y = xf * lax.rsqrt(jnp.mean(xf * xf, axis=-1, keepdims=True) + NORM_EPS)
    return (y * g.astype(jnp.float32)).astype(x.dtype)


def _adaln(c, w_ada, b_ada):
    mod = jax.nn.silu(c) @ w_ada + b_ada
    return [m[:, None, :] for m in jnp.split(mod, 6, axis=-1)]


def _dilated_attention(q, k, v, q_start):
    tq = q.shape[1]
    qpos = q_start + jnp.arange(tq)
    lses, outs = [], []
    for window, dil in DIL_PATTERNS:
        taps = jnp.arange(window // dil + 1) * dil
        idx = qpos[:, None] - taps[None, :]
        valid = idx >= 0
        idx = jnp.maximum(idx, 0)
        kg = k[:, idx]
        vg = v[:, idx]
        s = jnp.einsum('bqhd,bqjhd->bqhj', q, kg, preferred_element_type=jnp.float32)
        s = jnp.where(valid[None, :, None, :], s, -jnp.inf)
        m = jnp.max(s, axis=-1, keepdims=True)
        p = jnp.exp(s - m)
        den = jnp.sum(p, axis=-1)
        o = jnp.einsum('bqhj,bqjhd->bqhd', p, vg.astype(jnp.float32)) / den[..., None]
        lses.append(m[..., 0] + jnp.log(den))
        outs.append(o)
    wts = jax.nn.softmax(jnp.stack(lses), axis=0)
    out = jnp.sum(wts[..., None] * jnp.stack(outs), axis=0)
    return out.astype(q.dtype)


def _attn_prompt(q, k, v):
    b, s, h, d = q.shape
    nb = s // Q_BLOCK
    qb = jnp.moveaxis(q.reshape(b, nb, Q_BLOCK, h, d), 1, 0)
    starts = jnp.arange(nb, dtype=jnp.int32) * Q_BLOCK
    out = lax.map(lambda a: _dilated_attention(a[0], k, v, a[1]), (qb, starts))
    return jnp.moveaxis(out, 0, 1).reshape(b, s, h, d)


def _rwkv7(p, p_last, s0, mu, w0, w2, a0, a2, g2, k_k, k_a, r_k, lnx_w, lnx_b):
    f32 = jnp.float32
    b, t, _ = p.shape
    p_prev = jnp.concatenate([p_last.astype(p.dtype), p[:, :-1]], axis=1)
    z = p + (p_prev - p) * mu
    cuts = [RWKV_WIDTH, 2 * RWKV_WIDTH, 3 * RWKV_WIDTH,
            3 * RWKV_WIDTH + RWKV_DECAY_RANK, 3 * RWKV_WIDTH + RWKV_DECAY_RANK + RWKV_A_RANK]
    r, k, v, w_lo, a_lo, g_lo = jnp.split(z, cuts, axis=-1)
    w_log = -jax.nn.softplus(-(w0 + jnp.tanh(w_lo) @ w2).astype(f32)) - 0.5
    decay = jnp.exp(-jnp.exp(w_log))
    a = jax.nn.sigmoid((a0 + a_lo @ a2).astype(f32))
    g = (jax.nn.sigmoid(g_lo) @ g2).astype(f32)
    heads = lambda u: u.reshape(b, t, RWKV_HEADS, RWKV_HEAD_DIM)
    kk = heads((k * k_k).astype(f32))
    kk = kk / jnp.maximum(jnp.linalg.norm(kk, axis=-1, keepdims=True), 1e-12)
    kf = heads(k.astype(f32) * (1.0 + (a - 1.0) * k_a.astype(f32)))
    rf = heads(r.astype(f32))
    vf = heads(v.astype(f32))
    xs = tuple(jnp.moveaxis(u, 1, 0) for u in (rf, heads(decay), kf, vf, -kk, kk * heads(a)))

    def step(S, inp):
        r_t, w_t, k_t, v_t, a_t, b_t = inp
        sa = jnp.einsum('bhij,bhj->bhi', S, a_t)
        S = S * w_t[:, :, None, :] + sa[..., None] * b_t[:, :, None, :] + v_t[..., None] * k_t[:, :, None, :]
        return S, jnp.einsum('bhij,bhj->bhi', S, r_t)

    s_final, ys = lax.scan(step, s0.astype(f32), xs)
    y = jnp.moveaxis(ys, 0, 1)
    mean = jnp.mean(y, axis=-1, keepdims=True)
    var = jnp.mean(jnp.square(y - mean), axis=-1, keepdims=True)
    y = (y - mean) * lax.rsqrt(var + RWKV_GN_EPS) * lnx_w.astype(f32).reshape(RWKV_HEADS, RWKV_HEAD_DIM) \
        + lnx_b.astype(f32).reshape(RWKV_HEADS, RWKV_HEAD_DIM)
    y = y + jnp.sum(rf * kf * r_k.astype(f32), axis=-1, keepdims=True) * vf
    out = y.reshape(b, t, RWKV_WIDTH) * g
    return out.astype(p.dtype), s_final


def _peer_block(h, w_q, sub_keys, u, v):
    n = h.shape[0]
    q = (h @ w_q).reshape(n, PEER_HEADS, 2, PEER_HALF)
    s = jnp.einsum('nhcd,hckd->nhck', q, sub_keys, preferred_element_type=jnp.float32)
    top_s, top_i = lax.top_k(s, PEER_TOPK)
    cand_s = (top_s[:, :, 0, :, None] + top_s[:, :, 1, None, :]).reshape(n, PEER_HEADS, PEER_TOPK * PEER_TOPK)
    cand_i = (top_i[:, :, 0, :, None] * PEER_N_KEYS + top_i[:, :, 1, None, :]).reshape(n, PEER_HEADS, PEER_TOPK * PEER_TOPK)
    best_s, best_j = lax.top_k(cand_s, PEER_TOPK)
    expert = jnp.take_along_axis(cand_i, best_j, axis=-1).reshape(n, PEER_HEADS * PEER_TOPK)
    gate = jax.nn.softmax(best_s, axis=-1).reshape(n, PEER_HEADS * PEER_TOPK)
    u_sel = jnp.take(u, expert, axis=0)
    v_sel = jnp.take(v, expert, axis=0)
    act = jax.nn.gelu(jnp.einsum('nd,ned->ne', h, u_sel, preferred_element_type=jnp.float32), approximate=False)
    return jnp.einsum('ne,ned->nd', (gate * act).astype(h.dtype), v_sel)


def _peer_ffn(h, w_q, sub_keys, u, v):
    n, d = h.shape
    if n <= PEER_BLOCK:
        return _peer_block(h, w_q, sub_keys, u, v)
    nb = -(-n // PEER_BLOCK)
    hp = jnp.pad(h, ((0, nb * PEER_BLOCK - n), (0, 0))).reshape(nb, PEER_BLOCK, d)
    out = lax.map(lambda hb: _peer_block(hb, w_q, sub_keys, u, v), hp)
    return out.reshape(nb * PEER_BLOCK, d)[:n]


def _layer(x, c, attn_fn, shift_last, s0, w_ada, b_ada, norm1_g, norm2_g, w_in, q_gain, k_gain,
           rwkv_w, w_out, peer_w_q, peer_sub_keys, peer_u, peer_v):
    b, t, _ = x.shape
    sh1, sc1, gt1, sh2, sc2, gt2 = _adaln(c, w_ada, b_ada)
    h = _rmsnorm(x, norm1_g) * (1 + sc1) + sh1
    q, k, v, p = jnp.split(h @ w_in, [ATTN_WIDTH, 2 * ATTN_WIDTH, 3 * ATTN_WIDTH], axis=-1)
    hd = lambda u: u.reshape(b, t, ATTN_HEADS, ATTN_HEAD_DIM)
    q = _rmsnorm(hd(q), q_gain) * (ATTN_HEAD_DIM ** -0.5)
    k = _rmsnorm(hd(k), k_gain)
    v = hd(v)
    attn = attn_fn(q, k, v).reshape(b, t, ATTN_WIDTH)
    rw, s_new = _rwkv7(p, shift_last, s0, *rwkv_w)
    x = x + gt1 * (jnp.concatenate([attn, rw], axis=-1) @ w_out)
    h2 = (_rmsnorm(x, norm2_g) * (1 + sc2) + sh2).reshape(b * t, D_MODEL)
    ffn = _peer_ffn(h2, peer_w_q, peer_sub_keys, peer_u, peer_v).reshape(b, t, D_MODEL)
    return x + gt2 * ffn, k, v, p[:, -1:], s_new


def setup_inputs(seed: int = 0) -> dict:
    key = jax.random.key(seed)
    ks = jax.random.split(key, 32)
    f32 = jnp.float32
    nrm = lambda kk, shape, scale: jax.random.normal(kk, shape, f32) * scale
    buf = min(ATTN_WINDOW_MAX, PAST_LEN)
    return {
        'x_prompt': nrm(ks[0], (BATCH, SEQ, D_MODEL), 1.0),
        'x_sample': nrm(ks[1], (DEC_BATCH, DEC_SEQ, D_MODEL), 1.0),
        'cache_attn_k': nrm(ks[2], (DEC_BATCH, buf, ATTN_HEADS, ATTN_HEAD_DIM), 1.0),
        'cache_attn_v': nrm(ks[3], (DEC_BATCH, buf, ATTN_HEADS, ATTN_HEAD_DIM), 1.0),
        'state_rwkv': nrm(ks[4], (DEC_BATCH, RWKV_HEADS, RWKV_HEAD_DIM, RWKV_HEAD_DIM), 0.1),
        'state_rwkv_shift': nrm(ks[5], (DEC_BATCH, 1, RWKV_PROJ), 1.0),
        'c_prompt': nrm(ks[6], (BATCH, D_MODEL), 1.0),
        'c_sample': nrm(ks[7], (DEC_BATCH, D_MODEL), 1.0),
        'w_ada': nrm(ks[8], (D_MODEL, 6 * D_MODEL), 0.5 * D_MODEL ** -0.5),
        'b_ada': nrm(ks[9], (6 * D_MODEL,), 0.02),
        'norm1_g': 1.0 + nrm(ks[10], (D_MODEL,), 0.02),
        'norm2_g': 1.0 + nrm(ks[11], (D_MODEL,), 0.02),
        'w_in': nrm(ks[12], (D_MODEL, PROJ_WIDTH), D_MODEL ** -0.5),
        'q_gain': 1.0 + nrm(ks[13], (ATTN_HEAD_DIM,), 0.02),
        'k_gain': 1.0 + nrm(ks[14], (ATTN_HEAD_DIM,), 0.02),
        'rwkv_mu': jax.random.uniform(ks[15], (RWKV_PROJ,), f32),
        'rwkv_w0': jnp.linspace(-6.0, -1.0, RWKV_WIDTH, dtype=f32) + nrm(ks[16], (RWKV_WIDTH,), 0.1),
        'rwkv_w2': nrm(ks[17], (RWKV_DECAY_RANK, RWKV_WIDTH), 0.1),
        'rwkv_a0': nrm(ks[18], (RWKV_WIDTH,), 0.1),
        'rwkv_a2': nrm(ks[19], (RWKV_A_RANK, RWKV_WIDTH), 0.1),
        'rwkv_g2': nrm(ks[20], (RWKV_GATE_RANK, RWKV_WIDTH), RWKV_GATE_RANK ** -0.5),
        'rwkv_k_k': 0.85 + nrm(ks[21], (RWKV_WIDTH,), 0.02),
        'rwkv_k_a': 1.0 + nrm(ks[22], (RWKV_WIDTH,), 0.02),
        'rwkv_r_k': nrm(ks[23], (RWKV_HEADS, RWKV_HEAD_DIM), 0.1),
        'rwkv_lnx_w': 1.0 + nrm(ks[24], (RWKV_WIDTH,), 0.02),
        'rwkv_lnx_b': nrm(ks[25], (RWKV_WIDTH,), 0.02),
        'w_out': nrm(ks[26], (D_MODEL, D_MODEL), D_MODEL ** -0.5),
        'peer_w_q': nrm(ks[27], (D_MODEL, PEER_HEADS * PEER_QUERY_DIM), D_MODEL ** -0.5),
        'peer_sub_keys': nrm(ks[28], (PEER_HEADS, 2, PEER_N_KEYS, PEER_HALF), PEER_HALF ** -0.5),
        'peer_u': nrm(ks[29], (PEER_N_EXPERTS, D_MODEL), D_MODEL ** -0.5),
        'peer_v': nrm(ks[30], (PEER_N_EXPERTS, D_MODEL), 1.0),
    }


def reference(x_prompt, x_sample, cache_attn_k, cache_attn_v, state_rwkv, state_rwkv_shift,
              c_prompt, c_sample, w_ada, b_ada, norm1_g, norm2_g, w_in, q_gain, k_gain,
              rwkv_mu, rwkv_w0, rwkv_w2, rwkv_a0, rwkv_a2, rwkv_g2, rwkv_k_k, rwkv_k_a, rwkv_r_k,
              rwkv_lnx_w, rwkv_lnx_b, w_out, peer_w_q, peer_sub_keys, peer_u, peer_v):
    rwkv_w = (rwkv_mu, rwkv_w0, rwkv_w2, rwkv_a0, rwkv_a2, rwkv_g2, rwkv_k_k, rwkv_k_a, rwkv_r_k,
              rwkv_lnx_w, rwkv_lnx_b)
    buf = cache_attn_k.shape[1]

    def attn_sample(q, k, v):
        k_all = jnp.concatenate([cache_attn_k.astype(k.dtype), k], axis=1)
        v_all = jnp.concatenate([cache_attn_v.astype(v.dtype), v], axis=1)
        return _dilated_attention(q, k_all, v_all, buf)

    y_prompt, y_sample = x_prompt, x_sample
    for _ in range(DEPTH):
        shift0 = jnp.zeros((x_prompt.shape[0], 1, RWKV_PROJ), x_prompt.dtype)
        s0 = jnp.zeros((x_prompt.shape[0], RWKV_HEADS, RWKV_HEAD_DIM, RWKV_HEAD_DIM), jnp.float32)
        y_prompt, k_p, v_p, shift_p, s_p = _layer(
            y_prompt, c_prompt, _attn_prompt, shift0, s0, w_ada, b_ada, norm1_g, norm2_g, w_in,
            q_gain, k_gain, rwkv_w, w_out, peer_w_q, peer_sub_keys, peer_u, peer_v)
        y_sample, k_s, v_s, shift_s, s_s = _layer(
            y_sample, c_sample, attn_sample, state_rwkv_shift, state_rwkv, w_ada, b_ada, norm1_g,
            norm2_g, w_in, q_gain, k_gain, rwkv_w, w_out, peer_w_q, peer_sub_keys, peer_u, peer_v)
    win = min(ATTN_WINDOW_MAX, k_p.shape[1])
    return (y_prompt, y_sample,
            k_p[:, -win:], v_p[:, -win:], s_p.astype(x_prompt.dtype), shift_p,
            k_s, v_s, s_s.astype(state_rwkv.dtype), shift_s)
```

```python
import functools

import numpy as np
import jax
import jax.numpy as jnp
from jax import lax
from jax.experimental import pallas as pl
from jax.experimental.pallas import tpu as pltpu

F32 = jnp.float32
BF16 = jnp.bfloat16

D_MODEL = 2048
HEAD_DIM = 64
ATTN_WIDTH = 1024
RWKV_WIDTH = 1024
RWKV_HEADS = RWKV_WIDTH // HEAD_DIM
DECAY_RANK = 64
A_RANK = 64
GATE_RANK = 160
RWKV_PROJ = 3 * RWKV_WIDTH + DECAY_RANK + A_RANK + GATE_RANK
P_PAD = 3584
PROJ_PAD = P_PAD + 3 * ATTN_WIDTH
LORA_PAD = 384
DIL_PATTERNS = ((128, 1), (512, 4), (2048, 16))
CACHE_LEN = 2048
NORM_EPS = 1e-6
GN_EPS = HEAD_DIM * 1e-5
PEER_HEADS = 8
PEER_KEYS = 128
PEER_TOPK = 16
PEER_EXPERTS = PEER_KEYS * PEER_KEYS
NEG = -1e30

LANE = 128
QBLK = 128
CHUNK = 64
VMEM_LIMIT = 56 * 1024 * 1024


def _cparams(sem):
    return pltpu.CompilerParams(dimension_semantics=sem, vmem_limit_bytes=VMEM_LIMIT)


_NN = (((1,), (0,)), ((), ()))
_NT = (((1,), (1,)), ((), ()))
_TN = (((0,), (0,)), ((), ()))


def _bf(x):
    return x.astype(BF16)


def _dot(a, b, dims=_NN):
    return lax.dot_general(a, b, dims, preferred_element_type=F32)


def _mm(a, b, dims=_NN):
    return _dot(_bf(a), _bf(b), dims)


def _split(x):
    hi = _bf(x)
    lo = _bf(x - hi.astype(F32))
    return hi, lo


def _mm3(a, b, dims=_NN):
    ah, al = _split(a)
    bh, bl = _split(b)
    return _dot(ah, bh, dims) + (_dot(ah, bl, dims) + _dot(al, bh, dims))


def _mm2l(a, b_exact, dims=_NN):
    ah, al = _split(a)
    return _dot(ah, b_exact, dims) + _dot(al, b_exact, dims)


def _mm2r(a_exact, b, dims=_NN):
    bh, bl = _split(b)
    return _dot(a_exact, bh, dims) + _dot(a_exact, bl, dims)


def _group_ones(n):
    r = lax.broadcasted_iota(jnp.int32, (n, n), 0) // HEAD_DIM
    c = lax.broadcasted_iota(jnp.int32, (n, n), 1) // HEAD_DIM
    return jnp.where(r == c, 1.0, 0.0).astype(BF16)


def _group_sum(x, g):
    cols = x.shape[1] // LANE
    parts = [_mm2l(x[:, c * LANE:(c + 1) * LANE], g) for c in range(cols)]
    return parts[0] if cols == 1 else jnp.concatenate(parts, axis=1)


def _adaln_body(c_ref, w_ref, b_ref, o_ref):
    c = c_ref[...]
    s = c * (1.0 / (1.0 + jnp.exp(-c)))
    o_ref[...] = _mm3(s, w_ref[...]) + b_ref[...]


def _adaln(c, w_ada, b_ada):
    rows = c.shape[0]
    n = w_ada.shape[1]
    tn = 1024
    return pl.pallas_call(
        _adaln_body,
        out_shape=jax.ShapeDtypeStruct((rows, n), F32),
        grid=(n // tn,),
        in_specs=[pl.BlockSpec((rows, D_MODEL), lambda j: (0, 0)),
                  pl.BlockSpec((D_MODEL, tn), lambda j: (0, j)),
                  pl.BlockSpec((1, tn), lambda j: (0, j))],
        out_specs=pl.BlockSpec((rows, tn), lambda j: (0, j)),
        compiler_params=_cparams(("arbitrary",)),
        name="adaln",
    )(c, w_ada, b_ada.reshape(1, n))


def _modulated_norm(x, g, sc, sh):
    ms = jnp.mean(x * x, axis=-1, keepdims=True)
    return (x * lax.rsqrt(ms + NORM_EPS) * g) * (1.0 + sc) + sh


def _inproj_body(x_ref, g_ref, sc_ref, sh_ref, w_ref, o_ref, h_s):
    @pl.when(pl.program_id(1) == 0)
    def _():
        h_s[...] = _bf(_modulated_norm(x_ref[...], g_ref[...], sc_ref[...], sh_ref[...]))

    o_ref[...] = _dot(h_s[...], w_ref[...])


def _inproj(x2d, g, sc, sh, w_bf, tm, tiles_per_group):
    n = x2d.shape[0]
    r = sc.shape[1]
    tn = 512
    mod_spec = pl.BlockSpec((None, r, D_MODEL), lambda i, j: (i // tiles_per_group, 0, 0))
    return pl.pallas_call(
        _inproj_body,
        out_shape=jax.ShapeDtypeStruct((n, PROJ_PAD), F32),
        grid=(n // tm, PROJ_PAD // tn),
        in_specs=[pl.BlockSpec((tm, D_MODEL), lambda i, j: (i, 0)),
                  pl.BlockSpec((1, D_MODEL), lambda i, j: (0, 0)),
                  mod_spec, mod_spec,
                  pl.BlockSpec((D_MODEL, tn), lambda i, j: (0, j))],
        out_specs=pl.BlockSpec((tm, tn), lambda i, j: (i, j)),
        scratch_shapes=[pltpu.VMEM((tm, D_MODEL), BF16)],
        compiler_params=_cparams(("arbitrary", "arbitrary")),
        name="inproj",
    )(x2d, g.reshape(1, D_MODEL), sc, sh, w_bf)


def _head_norm(x, gain, gmat):
    ss = _group_sum(x * x, gmat)
    return x * lax.rsqrt(ss * (1.0 / HEAD_DIM) + NORM_EPS) * gain


def _attn_prompt_body(q_ref, k_ref, v_ref, qg_ref, kg_ref, o_ref, kc_ref, vc_ref,
                      qn_s, kn_s, m_s, l_s, acc_s):
    t_len = q_ref.shape[0]
    win = kc_ref.shape[0]
    gmat = _group_ones(LANE)
    rows_per = 512

    def prologue(c, carry):
        rows = pl.ds(pl.multiple_of(c * rows_per, rows_per), rows_per)
        qn_s[rows, :] = _head_norm(q_ref[rows, :], qg_ref[...], gmat) * (HEAD_DIM ** -0.5)
        kn_s[rows, :] = _head_norm(k_ref[rows, :], kg_ref[...], gmat)
        m_s[rows, :] = jnp.full((rows_per, LANE), NEG, F32)
        l_s[rows, :] = jnp.zeros((rows_per, LANE), F32)
        acc_s[rows, :] = jnp.zeros((rows_per, LANE), F32)
        return carry

    lax.fori_loop(0, t_len // rows_per, prologue, 0)
    kc_ref[...] = kn_s[t_len - win:, :]
    vc_ref[...] = v_ref[t_len - win:, :]

    qi = lax.broadcasted_iota(jnp.int32, (QBLK, 2 * QBLK), 0)
    kj = lax.broadcasted_iota(jnp.int32, (QBLK, 2 * QBLK), 1)
    in_prev = kj < QBLK
    upper = qi + QBLK
    head0 = lax.broadcasted_iota(jnp.int32, (QBLK, LANE), 1) < HEAD_DIM

    for _, dil in DIL_PATTERNS:
        nblk = t_len // dil // QBLK

        def rows_of(start, dil=dil):
            if dil == 1:
                return pl.ds(start, QBLK)
            return pl.ds(start, QBLK, stride=dil)

        def unit(u, carry, dil=dil, nblk=nblk, rows_of=rows_of):
            res = u // nblk
            blk = u - res * nblk
            cur = rows_of(blk * (QBLK * dil) + res)
            prev = rows_of(jnp.maximum(blk - 1, 0) * (QBLK * dil) + res)
            qb = qn_s[cur, :]
            kcat = _bf(jnp.concatenate([kn_s[prev, :], kn_s[cur, :]], axis=0))
            vcat = _bf(jnp.concatenate([v_ref[prev, :], v_ref[cur, :]], axis=0))
            lower = jnp.where(in_prev, qi + jnp.where(blk > 0, 0, 4 * QBLK), 0)
            stats = []
            for sel in (head0, ~head0):
                qh = _bf(jnp.where(sel, qb, 0.0))
                s = jnp.where(kj >= lower, jnp.where(kj <= upper, _dot(qh, kcat, _NT), NEG), NEG)
                mb = jnp.max(s, axis=-1, keepdims=True)
                p = jnp.exp(s - mb)
                lb = jnp.sum(p, axis=-1, keepdims=True)
                stats.append((mb, lb, _dot(_bf(p), vcat)))
            mb = jnp.where(head0, stats[0][0], stats[1][0])
            lb = jnp.where(head0, stats[0][1], stats[1][1])
            ob = jnp.where(head0, stats[0][2], stats[1][2])
            m_old = m_s[cur, :]
            m_new = jnp.maximum(m_old, mb)
            a_old = jnp.exp(m_old - m_new)
            a_blk = jnp.exp(mb - m_new)
            l_s[cur, :] = l_s[cur, :] * a_old + lb * a_blk
            acc_s[cur, :] = acc_s[cur, :] * a_old + ob * a_blk
            m_s[cur, :] = m_new
            return carry

        lax.fori_loop(0, dil * nblk, unit, 0)

    def epilogue(c, carry):
        rows = pl.ds(pl.multiple_of(c * rows_per, rows_per), rows_per)
        o_ref[rows, :] = acc_s[rows, :] / l_s[rows, :]
        return carry

    lax.fori_loop(0, t_len // rows_per, epilogue, 0)


def _attn_prompt(proj3, q_gain, k_gain, win):
    b, t, _ = proj3.shape
    pairs = ATTN_WIDTH // LANE
    qoff, koff, voff = P_PAD // LANE, (P_PAD + ATTN_WIDTH) // LANE, (P_PAD + 2 * ATTN_WIDTH) // LANE
    col = lambda off: pl.BlockSpec((None, t, LANE), lambda bi, hp: (bi, 0, off + hp))
    gain = lambda gvec: jnp.tile(gvec, 2).reshape(1, LANE)
    return pl.pallas_call(
        _attn_prompt_body,
        out_shape=(jax.ShapeDtypeStruct((b, t, ATTN_WIDTH), F32),
                   jax.ShapeDtypeStruct((b, win, ATTN_WIDTH), F32),
                   jax.ShapeDtypeStruct((b, win, ATTN_WIDTH), F32)),
        grid=(b, pairs),
        in_specs=[col(qoff), col(koff), col(voff),
                  pl.BlockSpec((1, LANE), lambda bi, hp: (0, 0)),
                  pl.BlockSpec((1, LANE), lambda bi, hp: (0, 0))],
        out_specs=(pl.BlockSpec((None, t, LANE), lambda bi, hp: (bi, 0, hp)),
                   pl.BlockSpec((None, win, LANE), lambda bi, hp: (bi, 0, hp)),
                   pl.BlockSpec((None, win, LANE), lambda bi, hp: (bi, 0, hp))),
        scratch_shapes=[pltpu.VMEM((t, LANE), F32) for _ in range(5)],
        compiler_params=_cparams(("arbitrary", "arbitrary")),
        name="attn_prompt",
    )(proj3, proj3, proj3, gain(q_gain), gain(k_gain))


def _attn_sample_body(q_ref, k_ref, v_ref, qg_ref, kg_ref, e_ref, et_ref,
                      k1_ref, k4_ref, k16_ref, v1_ref, v4_ref, v16_ref, o_ref, kn_ref):
    e = e_ref[...]
    et = et_ref[...]

    def head_norm(x, gain):
        ss = _mm3(_mm3(x * x, e), et)
        return x * lax.rsqrt(ss * (1.0 / HEAD_DIM) + NORM_EPS) * gain

    qn = head_norm(q_ref[...], qg_ref[...]) * (HEAD_DIM ** -0.5)
    kn = head_norm(k_ref[...], kg_ref[...])
    kn_ref[...] = kn
    v_new = v_ref[...]
    s_self = _mm3(kn * qn, e)
    outs, lses = [], []
    for kc_ref, vc_ref in ((k1_ref, v1_ref), (k4_ref, v4_ref), (k16_ref, v16_ref)):
        s = _mm3(kc_ref[...] * qn, e)
        m = jnp.maximum(jnp.max(s, axis=0, keepdims=True), s_self)
        p = jnp.exp(s - m)
        p_self = jnp.exp(s_self - m)
        den = jnp.sum(p, axis=0, keepdims=True) + p_self
        num = jnp.sum(_mm3(p, et) * vc_ref[...], axis=0, keepdims=True) + _mm3(p_self, et) * v_new
        outs.append(num / _mm3(den, et))
        lses.append(m + jnp.log(den))
    top = jnp.maximum(jnp.maximum(lses[0], lses[1]), lses[2])
    ws = [jnp.exp(l - top) for l in lses]
    tot = ws[0] + ws[1] + ws[2]
    o_ref[...] = sum(_mm3(w / tot, et) * o for w, o in zip(ws, outs))


def _attn_sample(q, k, v, q_gain, k_gain, cache_k, cache_v):
    bs = q.shape[0]
    assert cache_k.shape[1] == CACHE_LEN
    heads = ATTN_WIDTH // HEAD_DIM
    ind = (np.arange(ATTN_WIDTH)[:, None] // HEAD_DIM == np.arange(heads)[None, :]).astype(np.float32)
    row = pl.BlockSpec((None, 1, ATTN_WIDTH), lambda b: (b, 0, 0))
    full = lambda shape: pl.BlockSpec(shape, lambda b: (0,) * len(shape))
    views, specs = [], []
    for cache in (cache_k, cache_v):
        for window, dil in DIL_PATTERNS:
            taps = window // dil
            views.append(cache.reshape(bs, CACHE_LEN // dil, dil * ATTN_WIDTH))
            blk = (CACHE_LEN - window) // dil // taps
            specs.append(pl.BlockSpec((None, taps, ATTN_WIDTH), lambda b, blk=blk: (b, blk, 0)))
    r3 = lambda a: a.reshape(bs, 1, ATTN_WIDTH)
    attn, kn = pl.pallas_call(
        _attn_sample_body,
        out_shape=(jax.ShapeDtypeStruct((bs, 1, ATTN_WIDTH), F32),
                   jax.ShapeDtypeStruct((bs, 1, ATTN_WIDTH), F32)),
        grid=(bs,),
        in_specs=[row, row, row, full((1, ATTN_WIDTH)), full((1, ATTN_WIDTH)),
                  full((ATTN_WIDTH, heads)), full((heads, ATTN_WIDTH))] + specs,
        out_specs=(row, row),
        compiler_params=_cparams(("arbitrary",)),
        name="attn_sample",
    )(r3(q), r3(k), r3(v), jnp.tile(q_gain, heads).reshape(1, -1), jnp.tile(k_gain, heads).reshape(1, -1),
      jnp.asarray(ind), jnp.asarray(ind.T), *views)
    return attn.reshape(bs, ATTN_WIDTH), kn.reshape(bs, ATTN_WIDTH)


def _rwkv_prep_body(seq_mode, p_ref, prev_ref, mu_ref, w0_ref, a0_ref, kk_ref, ka_ref, rk_ref,
                    w2_ref, a2_ref, g2_ref,
                    r_o, lw_o, kf_o, v_o, kn_o, al_o, g_o, bonus_o):
    p = p_ref[...]
    if seq_mode:
        row0 = lax.broadcasted_iota(jnp.int32, p.shape, 0) == 0
        p_prev = jnp.where(row0, prev_ref[...], pltpu.roll(p, 1, 0))
    else:
        p_prev = prev_ref[...]
    z = p + (p_prev - p) * mu_ref[...]
    r = z[:, 0:RWKV_WIDTH]
    k = z[:, RWKV_WIDTH:2 * RWKV_WIDTH]
    v = z[:, 2 * RWKV_WIDTH:3 * RWKV_WIDTH]
    tail = z[:, 3 * RWKV_WIDTH:3 * RWKV_WIDTH + LORA_PAD]
    dw = _mm3(jnp.tanh(tail), w2_ref[...])
    da = _mm3(tail, a2_ref[...])
    g = _mm3(1.0 / (1.0 + jnp.exp(-tail)), g2_ref[...])
    u = -(w0_ref[...] + dw)
    softplus = jnp.maximum(u, 0.0) + jnp.log1p(jnp.exp(-jnp.abs(u)))
    lw = -jnp.exp(-softplus - 0.5)
    alpha = 1.0 / (1.0 + jnp.exp(-(a0_ref[...] + da)))
    gmat = _group_ones(LANE)
    kk = k * kk_ref[...]
    nrm = jnp.sqrt(_group_sum(kk * kk, gmat))
    kn = kk / jnp.maximum(nrm, 1e-12)
    kf = k * (1.0 + (alpha - 1.0) * ka_ref[...])
    bonus = _group_sum(r * kf * rk_ref[...], gmat) * v
    r_o[...] = r
    lw_o[...] = lw
    kf_o[...] = kf
    v_o[...] = v
    kn_o[...] = kn
    al_o[...] = alpha
    g_o[...] = g
    bonus_o[...] = bonus


def _rwkv_prep(seq_mode, proj, prev, params, tm, tiles_per_seq):
    n = proj.shape[0]
    mu, w0, a0, k_k, k_a, r_k, w2p, a2p, g2p = params
    p_spec = pl.BlockSpec((tm, P_PAD), lambda i: (i, 0))
    if seq_mode:
        prev_rows = _prev_rows(proj, prev, tm, tiles_per_seq)
        prev_spec = pl.BlockSpec((None, 1, P_PAD), lambda i: (i, 0, 0))
        prev_arg = prev_rows
    else:
        prev_spec = pl.BlockSpec((tm, P_PAD), lambda i: (i, 0))
        prev_arg = prev
    vec = lambda width: pl.BlockSpec((1, width), lambda i: (0, 0))
    mat = pl.BlockSpec((LORA_PAD, RWKV_WIDTH), lambda i: (0, 0))
    out_spec = pl.BlockSpec((tm, RWKV_WIDTH), lambda i: (i, 0))
    out = jax.ShapeDtypeStruct((n, RWKV_WIDTH), F32)
    return pl.pallas_call(
        functools.partial(_rwkv_prep_body, seq_mode),
        out_shape=(out,) * 8,
        grid=(n // tm,),
        in_specs=[p_spec, prev_spec, vec(P_PAD)] + [vec(RWKV_WIDTH)] * 5 + [mat] * 3,
        out_specs=(out_spec,) * 8,
        compiler_params=_cparams(("arbitrary",)),
        name="rwkv_prep_seq" if seq_mode else "rwkv_prep_row",
    )(proj, prev_arg, mu, w0, a0, k_k, k_a, r_k, w2p, a2p, g2p)


def _prev_rows(proj, shift0, tm, tiles_per_seq):
    n = proj.shape[0]
    tiles = n // tm
    last = proj[tm - 1::tm, :P_PAD][:tiles - 1]
    rows = jnp.concatenate([jnp.zeros((1, P_PAD), F32), last], axis=0).reshape(tiles // tiles_per_seq, tiles_per_seq, P_PAD)
    rows = rows.at[:, 0, :].set(shift0[:, 0, :])
    return rows.reshape(tiles, 1, P_PAD)


def _stack2(x, head0):
    return jnp.concatenate([jnp.where(head0, x, 0.0), jnp.where(head0, 0.0, x)], axis=0)


def _rwkv_scan_body(r_ref, lw_ref, kf_ref, v_ref, kn_ref, al_ref, g_ref, bonus_ref, lnw_ref, lnb_ref, h0_ref,
                    y_ref, hT_ref, h_s):
    c = CHUNK
    tt = r_ref.shape[0]
    ti = pl.program_id(2)

    @pl.when(ti == 0)
    def _():
        h_s[...] = h0_ref[...]

    head0 = lax.broadcasted_iota(jnp.int32, (c, LANE), 1) < HEAD_DIM
    ri = lax.broadcasted_iota(jnp.int32, (2 * c, 2 * c), 0)
    ci = lax.broadcasted_iota(jnp.int32, (2 * c, 2 * c), 1)
    same_head = (ri // c) == (ci // c)
    strict = same_head & (ci < ri)
    incl = same_head & (ci <= ri)
    eye = ri == ci
    tri = jnp.where(lax.broadcasted_iota(jnp.int32, (c, c), 1) <= lax.broadcasted_iota(jnp.int32, (c, c), 0),
                    1.0, 0.0).astype(BF16)
    gmat = _group_ones(LANE)

    def chunk(ci_, carry):
        rows = pl.ds(pl.multiple_of(ci_ * c, c), c)
        r, lw, kf, v = r_ref[rows, :], lw_ref[rows, :], kf_ref[rows, :], v_ref[rows, :]
        kn, al = kn_ref[rows, :], al_ref[rows, :]
        gcum = _mm2r(tri, lw)
        g_last = gcum[c - 1:c, :]
        at = -kn * jnp.exp(gcum - lw)
        rt = r * jnp.exp(gcum)
        inv = jnp.exp(-gcum)
        bt = kn * al * inv
        kt = kf * inv
        dec_end = jnp.exp(g_last)
        b2 = bt * dec_end
        k2 = kt * dec_end
        at_st, rt_st = _stack2(at, head0), _stack2(rt, head0)
        v_st = _stack2(v, head0)
        lhs = jnp.concatenate([at_st, rt_st], axis=0)
        rhs = jnp.concatenate([bt, bt, kt, kt], axis=0)
        big = _mm3(lhs, rhs, _NT)
        a_ab = jnp.where(strict, big[0:2 * c, 0:2 * c], 0.0)
        a_ak = jnp.where(strict, big[0:2 * c, 2 * c:4 * c], 0.0)
        a_rb = jnp.where(incl, big[2 * c:4 * c, 0:2 * c], 0.0)
        a_rk = jnp.where(incl, big[2 * c:4 * c, 2 * c:4 * c], 0.0)
        x = jnp.where(eye, 1.0, 0.0) + a_ab
        pw = _mm3(a_ab, a_ab)
        steps = int(np.log2(c)) - 1
        for s in range(steps):
            if s < steps - 1:
                both = _mm3(pw, jnp.concatenate([pw, x], axis=1))
                x = x + both[:, 2 * c:]
                pw = both[:, :2 * c]
            else:
                x = x + _mm3(pw, x)
        av_st = _mm(a_ak, v_st)
        wu_uv = _mm3(x, jnp.concatenate([at_st, av_st], axis=1))
        wu_st, uv_st = wu_uv[:, :LANE], wu_uv[:, LANE:]
        rb = _mm(a_rb, wu_uv)
        r2_st = rt_st + rb[:, :LANE]
        yv_st = rb[:, LANE:] + _mm(a_rk, v_st)
        b2_st, k2_st = _stack2(b2, head0), _stack2(k2, head0)
        hmat = h_s[...]
        trans = jnp.where(eye, dec_end, 0.0) + _mm3(b2_st, wu_st, _TN)
        add = _mm(jnp.concatenate([b2_st, k2_st], axis=0), jnp.concatenate([uv_st, v_st], axis=0), _TN)
        y_st = _mm(r2_st, hmat) + yv_st
        h_s[...] = _mm3(trans, hmat) + add
        y = y_st[:c, :] + y_st[c:, :]
        mean = _group_sum(y, gmat) * (1.0 / HEAD_DIM)
        dev = y - mean
        var = _group_sum(dev * dev, gmat) * (1.0 / HEAD_DIM)
        yn = dev * lax.rsqrt(var + GN_EPS) * lnw_ref[...] + lnb_ref[...]
        y_ref[rows, :] = (yn + bonus_ref[rows, :]) * g_ref[rows, :]
        return carry

    lax.fori_loop(0, tt // c, chunk, 0)

    @pl.when(ti == pl.num_programs(2) - 1)
    def _():
        hT_ref[...] = h_s[...]


def _rwkv_scan(vecs, lnw, lnb, h0, b, t):
    pairs = RWKV_WIDTH // LANE
    tt = min(t, 1024)
    nt = t // tt
    seq = pl.BlockSpec((tt, LANE), lambda bi, hp, ti: (bi * nt + ti, hp))
    vec = pl.BlockSpec((1, LANE), lambda bi, hp, ti: (0, hp))
    st = pl.BlockSpec((None, None, LANE, LANE), lambda bi, hp, ti: (bi, hp, 0, 0))
    return pl.pallas_call(
        _rwkv_scan_body,
        out_shape=(jax.ShapeDtypeStruct((b * t, RWKV_WIDTH), F32),
                   jax.ShapeDtypeStruct((b, pairs, LANE, LANE), F32)),
        grid=(b, pairs, nt),
        in_specs=[seq] * 8 + [vec, vec, st],
        out_specs=(seq, st),
        scratch_shapes=[pltpu.VMEM((LANE, LANE), F32)],
        compiler_params=_cparams(("arbitrary", "arbitrary", "arbitrary")),
        name="rwkv_scan",
    )(*vecs, lnw.reshape(1, RWKV_WIDTH), lnb.reshape(1, RWKV_WIDTH), h0)


def _rwkv_step_body(s_ref, lw_ref, kn_ref, al_ref, k_ref, r_ref, v_ref, g_ref, bonus_ref, lnw_ref, lnb_ref,
                    y_ref, so_ref):
    s = s_ref[...]
    kn = kn_ref[...]
    sa = jnp.sum(s * (-kn), axis=-1, keepdims=True)
    s_new = s * jnp.exp(lw_ref[...]) + sa * (kn * al_ref[...]) + v_ref[...] * k_ref[...]
    so_ref[...] = s_new
    y = jnp.sum(s_new * r_ref[...], axis=-1, keepdims=True)
    mean = jnp.mean(y, axis=1, keepdims=True)
    dev = y - mean
    var = jnp.mean(dev * dev, axis=1, keepdims=True)
    yn = dev * lax.rsqrt(var + GN_EPS) * lnw_ref[...] + lnb_ref[...]
    y_ref[...] = (yn + bonus_ref[...]) * g_ref[...]


def _rwkv_step(state, vecs, lnw, lnb):
    bs = state.shape[0]
    r, lw, kf, v, kn, al, g, bonus = vecs
    rowv = lambda a: a.reshape(bs, RWKV_HEADS, 1, HEAD_DIM)
    colv = lambda a: a.reshape(bs, RWKV_HEADS, HEAD_DIM, 1)
    row_spec = pl.BlockSpec((None, RWKV_HEADS, 1, HEAD_DIM), lambda b: (b, 0, 0, 0))
    col_spec = pl.BlockSpec((None, RWKV_HEADS, HEAD_DIM, 1), lambda b: (b, 0, 0, 0))
    st_spec = pl.BlockSpec((None, RWKV_HEADS, HEAD_DIM, HEAD_DIM), lambda b: (b, 0, 0, 0))
    par_spec = pl.BlockSpec((RWKV_HEADS, HEAD_DIM, 1), lambda b: (0, 0, 0))
    y, s_new = pl.pallas_call(
        _rwkv_step_body,
        out_shape=(jax.ShapeDtypeStruct((bs, RWKV_HEADS, HEAD_DIM, 1), F32),
                   jax.ShapeDtypeStruct(state.shape, F32)),
        grid=(bs,),
        in_specs=[st_spec] + [row_spec] * 5 + [col_spec] * 3 + [par_spec] * 2,
        out_specs=(col_spec, st_spec),
        compiler_params=_cparams(("arbitrary",)),
        name="rwkv_step",
    )(state, rowv(lw), rowv(kn), rowv(al), rowv(kf), rowv(r), colv(v), colv(g), colv(bonus),
      lnw.reshape(RWKV_HEADS, HEAD_DIM, 1), lnb.reshape(RWKV_HEADS, HEAD_DIM, 1))
    return y.reshape(bs, RWKV_WIDTH), s_new


def _outproj_body(a_ref, r_ref, x_ref, gt_ref, w_ref, g2_ref, sc_ref, sh_ref, x1_ref, h2_ref):
    y = _dot(_bf(a_ref[...]), w_ref[0:ATTN_WIDTH, :]) + _dot(_bf(r_ref[...]), w_ref[ATTN_WIDTH:, :])
    x1 = x_ref[...] + gt_ref[...] * y
    x1_ref[...] = x1
    h2_ref[...] = _modulated_norm(x1, g2_ref[...], sc_ref[...], sh_ref[...])


def _outproj(attn, rw, x2d, gt, w_bf, g2, sc, sh, tm, tiles_per_group):
    n = x2d.shape[0]
    r = gt.shape[1]
    half = pl.BlockSpec((tm, ATTN_WIDTH), lambda i: (i, 0))
    full = pl.BlockSpec((tm, D_MODEL), lambda i: (i, 0))
    mod = pl.BlockSpec((None, r, D_MODEL), lambda i: (i // tiles_per_group, 0, 0))
    return pl.pallas_call(
        _outproj_body,
        out_shape=(jax.ShapeDtypeStruct((n, D_MODEL), F32), jax.ShapeDtypeStruct((n, D_MODEL), F32)),
        grid=(n // tm,),
        in_specs=[half, half, full, mod, pl.BlockSpec((D_MODEL, D_MODEL), lambda i: (0, 0)),
                  pl.BlockSpec((1, D_MODEL), lambda i: (0, 0)), mod, mod],
        out_specs=(full, full),
        compiler_params=_cparams(("arbitrary",)),
        name="outproj",
    )(attn, rw, x2d, gt, w_bf, g2.reshape(1, D_MODEL), sc, sh)


def _staircase():
    pairs = [(a, b) for a in range(PEER_TOPK) for b in range(PEER_TOPK) if (a + 1) * (b + 1) <= PEER_TOPK]
    return pairs


_CAND = _staircase()
_CAND_ROWS = 56


def _extract_top(s, count, rows):
    rank = jnp.full(s.shape, 99.0, F32)
    vals = []
    limit = s.shape[0]
    for kth in range(count):
        mx = jnp.max(s, axis=0, keepdims=True)
        first = jnp.min(jnp.where(s == mx, rows, float(limit)), axis=0, keepdims=True)
        hit = rows == first
        rank = jnp.where(hit, float(kth), rank)
        s = jnp.where(hit, NEG, s)
        vals.append(mx)
    return vals, rank


def _router_body(h2_ref, wq_ref, keys_ref, oh_ref, h2t_ref, cnt_ref, e1_ref, rk_ref, e2_ref):
    h2 = h2_ref[...]
    tm = h2.shape[0]
    h2t_ref[...] = _bf(h2.T)
    h2b = _bf(h2)
    rows = lax.broadcasted_iota(jnp.int32, (PEER_KEYS, tm), 0).astype(F32)
    crow = lax.broadcasted_iota(jnp.int32, (_CAND_ROWS, tm), 0).astype(F32)

    def head(h, carry):
        scores, tops, ranks = [], [], []
        for half in range(2):
            g = 2 * h + half
            wq = wq_ref[pl.ds(pl.multiple_of(g * PEER_KEYS, PEER_KEYS), PEER_KEYS), :]
            q_t = _dot(wq, h2b, _NT)
            s_t = _mm3(keys_ref[g], q_t)
            vals, rank = _extract_top(s_t, PEER_TOPK, rows)
            scores.append(s_t)
            tops.append(vals)
            ranks.append(rank)
        cand = [tops[0][a] + tops[1][b] for a, b in _CAND]
        cand += [jnp.full((1, tm), NEG, F32)] * (_CAND_ROWS - len(cand))
        cand = jnp.concatenate(cand, axis=0)
        top = tops[0][0] + tops[1][0]
        taken = jnp.zeros((_CAND_ROWS, tm), F32)
        zsum = jnp.zeros((1, tm), F32)
        for _ in range(PEER_TOPK):
            mx = jnp.max(cand, axis=0, keepdims=True)
            first = jnp.min(jnp.where(cand == mx, crow, float(_CAND_ROWS)), axis=0, keepdims=True)
            hit = crow == first
            taken = jnp.where(hit, 1.0, taken)
            cand = jnp.where(hit, NEG, cand)
            zsum = zsum + jnp.exp(mx - top)
        per_rank = _dot(oh_ref[...], _bf(taken))
        cnt = jnp.zeros((PEER_KEYS, tm), F32)
        for kth in range(PEER_TOPK):
            cnt = jnp.where(ranks[0] == float(kth), per_rank[kth:kth + 1, :], cnt)
        cnt_ref[h] = cnt
        rk_ref[h] = ranks[1]
        e1_ref[h] = jnp.where(ranks[0] < PEER_TOPK, jnp.exp(scores[0] - tops[0][0]), 0.0)
        e2_ref[h] = jnp.where(ranks[1] < PEER_TOPK, jnp.exp(scores[1] - tops[1][0]) / zsum, 0.0)
        return carry

    lax.fori_loop(0, PEER_HEADS, head, 0)


def _router(h2, wq_t_bf, keys, tm):
    n = h2.shape[0]
    onehot = np.zeros((PEER_TOPK, _CAND_ROWS), np.float32)
    for idx, (a, _) in enumerate(_CAND):
        onehot[a, idx] = 1.0
    tok = pl.BlockSpec((PEER_HEADS, PEER_KEYS, tm), lambda i: (0, 0, i))
    tok_shape = jax.ShapeDtypeStruct((PEER_HEADS, PEER_KEYS, n), F32)
    return pl.pallas_call(
        _router_body,
        out_shape=(jax.ShapeDtypeStruct((D_MODEL, n), BF16), tok_shape, tok_shape, tok_shape, tok_shape),
        grid=(n // tm,),
        in_specs=[pl.BlockSpec((tm, D_MODEL), lambda i: (i, 0)),
                  pl.BlockSpec((D_MODEL, D_MODEL), lambda i: (0, 0)),
                  pl.BlockSpec((2 * PEER_HEADS, PEER_KEYS, PEER_KEYS), lambda i: (0, 0, 0)),
                  pl.BlockSpec((PEER_TOPK, _CAND_ROWS), lambda i: (0, 0))],
        out_specs=(pl.BlockSpec((D_MODEL, tm), lambda i: (0, i)), tok, tok, tok, tok),
        compiler_params=_cparams(("arbitrary",)),
        name="peer_router",
    )(h2, wq_t_bf, keys, jnp.asarray(onehot, BF16))


def _experts_body(h2t_ref, u_ref, vt_ref, cnt_ref, e1_ref, rk_ref, e2_ref, x1_ref, gt_ref, y_ref, acc_s, p_s):
    j = pl.program_id(1)
    eb = u_ref.shape[0]
    per = eb // PEER_KEYS

    @pl.when(j == 0)
    def _():
        acc_s[...] = jnp.zeros(acc_s.shape, F32)

    act_in = _dot(u_ref[...], h2t_ref[...])
    act = 0.5 * act_in * (1.0 + lax.erf(act_in * (2.0 ** -0.5)))
    for ii in range(per):
        i = j * per + ii
        wt = None
        for h in range(PEER_HEADS):
            partners = cnt_ref[h, pl.ds(i, 1), :]
            e1 = e1_ref[h, pl.ds(i, 1), :]
            term = jnp.where(rk_ref[h] < partners, e2_ref[h], 0.0) * e1
            wt = term if wt is None else wt + term
        p_s[ii * PEER_KEYS:(ii + 1) * PEER_KEYS, :] = _bf(wt * act[ii * PEER_KEYS:(ii + 1) * PEER_KEYS, :])
    acc_s[...] += _dot(vt_ref[...], p_s[...])

    @pl.when(j == pl.num_programs(1) - 1)
    def _():
        y_ref[...] = x1_ref[...] + gt_ref[...] * acc_s[...].T


def _experts(h2t, u_bf, vt_bf, route, x1, gt, tm, tiles_per_group):
    n = x1.shape[0]
    eb = 512
    r = gt.shape[1]
    tok = pl.BlockSpec((PEER_HEADS, PEER_KEYS, tm), lambda i, j: (0, 0, i))
    return pl.pallas_call(
        _experts_body,
        out_shape=jax.ShapeDtypeStruct((n, D_MODEL), F32),
        grid=(n // tm, PEER_EXPERTS // eb),
        in_specs=[pl.BlockSpec((D_MODEL, tm), lambda i, j: (0, i)),
                  pl.BlockSpec((eb, D_MODEL), lambda i, j: (j, 0)),
                  pl.BlockSpec((D_MODEL, eb), lambda i, j: (0, j)),
                  tok, tok, tok, tok,
                  pl.BlockSpec((tm, D_MODEL), lambda i, j: (i, 0)),
                  pl.BlockSpec((None, r, D_MODEL), lambda i, j: (i // tiles_per_group, 0, 0))],
        out_specs=pl.BlockSpec((tm, D_MODEL), lambda i, j: (i, 0)),
        scratch_shapes=[pltpu.VMEM((D_MODEL, tm), F32), pltpu.VMEM((eb, tm), BF16)],
        compiler_params=_cparams(("arbitrary", "arbitrary")),
        name="peer_experts",
    )(h2t, u_bf, vt_bf, *route, x1, gt)


def _pad_rows(a, rows):
    return jnp.pad(a, ((0, rows - a.shape[0]),) + ((0, 0),) * (a.ndim - 1))


def _state_to_pairs(state):
    b = state.shape[0]
    h = jnp.swapaxes(state, -1, -2).reshape(b, RWKV_HEADS // 2, 2, HEAD_DIM, HEAD_DIM)
    z = jnp.zeros_like(h[:, :, 0])
    top = jnp.concatenate([h[:, :, 0], z], axis=-1)
    bot = jnp.concatenate([z, h[:, :, 1]], axis=-1)
    return jnp.concatenate([top, bot], axis=-2)


def _pairs_to_state(hp):
    b = hp.shape[0]
    h0 = hp[:, :, :HEAD_DIM, :HEAD_DIM]
    h1 = hp[:, :, HEAD_DIM:, HEAD_DIM:]
    h = jnp.stack([h0, h1], axis=2).reshape(b, RWKV_HEADS, HEAD_DIM, HEAD_DIM)
    return jnp.swapaxes(h, -1, -2)


def kernel(x_prompt, x_sample, cache_attn_k, cache_attn_v, state_rwkv, state_rwkv_shift, c_prompt, c_sample, w_ada, b_ada, norm1_g, norm2_g, w_in, q_gain, k_gain, rwkv_mu, rwkv_w0, rwkv_w2, rwkv_a0, rwkv_a2, rwkv_g2, rwkv_k_k, rwkv_k_a, rwkv_r_k, rwkv_lnx_w, rwkv_lnx_b, w_out, peer_w_q, peer_sub_keys, peer_u, peer_v):
    b, t, _ = x_prompt.shape
    bs = x_sample.shape[0]
    n = b * t
    heads = ATTN_WIDTH // HEAD_DIM
    win = min(CACHE_LEN, t)

    w_in_bf = jnp.concatenate([w_in[:, 3 * ATTN_WIDTH:], jnp.zeros((D_MODEL, P_PAD - RWKV_PROJ), F32),
                               w_in[:, :3 * ATTN_WIDTH]], axis=1).astype(BF16)
    w_out_bf = w_out.astype(BF16)
    wq_t_bf = peer_w_q.T.astype(BF16)
    keys = peer_sub_keys.reshape(2 * PEER_HEADS, PEER_KEYS, PEER_KEYS)
    u_bf = peer_u.astype(BF16)
    vt_bf = peer_v.T.astype(BF16)
    lora = lambda w, off: jnp.zeros((LORA_PAD, RWKV_WIDTH), F32).at[off:off + w.shape[0]].set(w)
    row = lambda a: a.reshape(1, -1)
    rwkv_params = (row(jnp.pad(rwkv_mu, (0, P_PAD - RWKV_PROJ))), row(rwkv_w0), row(rwkv_a0), row(rwkv_k_k),
                   row(rwkv_k_a), row(rwkv_r_k), lora(rwkv_w2, 0), lora(rwkv_a2, DECAY_RANK),
                   lora(rwkv_g2, DECAY_RANK + A_RANK))

    c_rows = b + bs
    c_pad = -(-c_rows // 8) * 8
    mod = _adaln(_pad_rows(jnp.concatenate([c_prompt, c_sample], axis=0), c_pad), w_ada, b_ada)
    mod_p = [m.reshape(b, 1, D_MODEL) for m in jnp.split(mod[:b], 6, axis=-1)]
    mod_s = [m.reshape(1, bs, D_MODEL) for m in jnp.split(mod[b:c_rows], 6, axis=-1)]

    tm_p = min(1024, t)
    xp = x_prompt.reshape(n, D_MODEL)
    proj_p = _inproj(xp, norm1_g, mod_p[1], mod_p[0], w_in_bf, tm_p, t // tm_p)
    attn_p, k_cache, v_cache = _attn_prompt(proj_p.reshape(b, t, PROJ_PAD), q_gain, k_gain, win)
    tm_r = min(256, t)
    shift0 = jnp.zeros((b, 1, P_PAD), F32)
    vecs_p = _rwkv_prep(True, proj_p, shift0, rwkv_params, tm_r, t // tm_r)
    rw_p, h_fin = _rwkv_scan(vecs_p, rwkv_lnx_w, rwkv_lnx_b, jnp.zeros((b, RWKV_HEADS // 2, LANE, LANE), F32), b, t)
    tm_o = min(256, t)
    x1_p, h2_p = _outproj(attn_p.reshape(n, ATTN_WIDTH), rw_p, xp, mod_p[2], w_out_bf, norm2_g,
                          mod_p[4], mod_p[3], tm_o, t // tm_o)
    tm_e = min(256, t)
    h2t_p, *route_p = _router(h2_p, wq_t_bf, keys, tm_e)
    y_p = _experts(h2t_p, u_bf, vt_bf, route_p, x1_p, mod_p[5], tm_e, t // tm_e)

    xs = x_sample.reshape(bs, D_MODEL)
    proj_s = _inproj(xs, norm1_g, mod_s[1], mod_s[0], w_in_bf, bs, 1)
    q_s = proj_s[:, P_PAD:P_PAD + ATTN_WIDTH]
    k_s = proj_s[:, P_PAD + ATTN_WIDTH:P_PAD + 2 * ATTN_WIDTH]
    v_s = proj_s[:, P_PAD + 2 * ATTN_WIDTH:]
    attn_s, kn_s = _attn_sample(q_s, k_s, v_s, q_gain, k_gain, cache_attn_k, cache_attn_v)
    prev_s = jnp.pad(state_rwkv_shift.reshape(bs, RWKV_PROJ), ((0, 0), (0, P_PAD - RWKV_PROJ)))
    vecs_s = _rwkv_prep(False, proj_s, prev_s, rwkv_params, bs, 1)
    rw_s, state_s = _rwkv_step(state_rwkv, vecs_s, rwkv_lnx_w, rwkv_lnx_b)
    x1_s, h2_s = _outproj(attn_s, rw_s, xs, mod_s[2], w_out_bf, norm2_g, mod_s[4], mod_s[3], bs, 1)
    ns = -(-bs // LANE) * LANE
    h2t_s, *route_s = _router(_pad_rows(h2_s, ns), wq_t_bf, keys, LANE)
    gt2_s = _pad_rows(mod_s[5][0], ns).reshape(ns // LANE, LANE, D_MODEL)
    y_s = _experts(h2t_s, u_bf, vt_bf, route_s, _pad_rows(x1_s, ns), gt2_s, LANE, 1)[:bs]

    return (y_p.reshape(b, t, D_MODEL), y_s.reshape(bs, 1, D_MODEL),
            k_cache.reshape(b, win, heads, HEAD_DIM), v_cache.reshape(b, win, heads, HEAD_DIM),
            _pairs_to_state(h_fin), proj_p.reshape(b, t, PROJ_PAD)[:, t - 1:, :RWKV_PROJ],
            kn_s.reshape(bs, 1, heads, HEAD_DIM), v_s.reshape(bs, 1, heads, HEAD_DIM),
            state_s, proj_s[:, :RWKV_PROJ].reshape(bs, 1, RWKV_PROJ))
```

```python
import functools

import numpy as np
import jax
import jax.numpy as jnp
from jax import lax
from jax.experimental import pallas as pl
from jax.experimental.pallas import tpu as pltpu

F32 = jnp.float32
BF16 = jnp.bfloat16

D_MODEL = 2048
HEAD_DIM = 64
ATTN_WIDTH = 1024
RWKV_WIDTH = 1024
RWKV_HEADS = RWKV_WIDTH // HEAD_DIM
DECAY_RANK = 64
A_RANK = 64
GATE_RANK = 160
RWKV_PROJ = 3 * RWKV_WIDTH + DECAY_RANK + A_RANK + GATE_RANK
P_PAD = 3584
PROJ_PAD = P_PAD + 3 * ATTN_WIDTH
LORA_PAD = 384
DIL_PATTERNS = ((128, 1), (512, 4), (2048, 16))
CACHE_LEN = 2048
NORM_EPS = 1e-6
GN_EPS = HEAD_DIM * 1e-5
PEER_HEADS = 8
PEER_KEYS = 128
PEER_TOPK = 16
PEER_EXPERTS = PEER_KEYS * PEER_KEYS
NEG = -1e30

LANE = 128
QBLK = 128
CHUNK = 64
RWKV_GROUP = 1
RWKV_INTERLEAVE = 8
ROUTER_INTERLEAVE = 2
ATTN_INTERLEAVE = 4
VMEM_LIMIT = 56 * 1024 * 1024


def _cparams(sem):
    return pltpu.CompilerParams(dimension_semantics=sem, vmem_limit_bytes=VMEM_LIMIT)


_NN = (((1,), (0,)), ((), ()))
_NT = (((1,), (1,)), ((), ()))
_TN = (((0,), (0,)), ((), ()))


def _bf(x):
    return x.astype(BF16)


def _dot(a, b, dims=_NN):
    return lax.dot_general(a, b, dims, preferred_element_type=F32)


def _mm(a, b, dims=_NN):
    return _dot(_bf(a), _bf(b), dims)


def _split(x):
    hi = _bf(x)
    lo = _bf(x - hi.astype(F32))
    return hi, lo


def _mm3(a, b, dims=_NN):
    ah, al = _split(a)
    bh, bl = _split(b)
    return _dot(ah, bh, dims) + (_dot(ah, bl, dims) + _dot(al, bh, dims))


def _mm2l(a, b_exact, dims=_NN):
    ah, al = _split(a)
    return _dot(ah, b_exact, dims) + _dot(al, b_exact, dims)


def _mm2r(a_exact, b, dims=_NN):
    bh, bl = _split(b)
    return _dot(a_exact, bh, dims) + _dot(a_exact, bl, dims)


def _group_ones(n):
    r = lax.broadcasted_iota(jnp.int32, (n, n), 0) // HEAD_DIM
    c = lax.broadcasted_iota(jnp.int32, (n, n), 1) // HEAD_DIM
    return jnp.where(r == c, 1.0, 0.0).astype(BF16)


def _group_sum(x, g):
    cols = x.shape[1] // LANE
    parts = [_mm2l(x[:, c * LANE:(c + 1) * LANE], g) for c in range(cols)]
    return parts[0] if cols == 1 else jnp.concatenate(parts, axis=1)


def _lockstep_gen(gens):
    results = [None] * len(gens)
    live = list(enumerate(gens))
    while live:
        nxt = []
        for idx, gen in live:
            try:
                next(gen)
                nxt.append((idx, gen))
            except StopIteration as stop:
                results[idx] = stop.value
        live = nxt
        if live:
            yield
    return results


def _lockstep(gens):
    runner = _lockstep_gen(gens)
    while True:
        try:
            next(runner)
        except StopIteration as stop:
            return stop.value


def _adaln_body(c_ref, w_ref, b_ref, o_ref):
    c = c_ref[...]
    s = c * (1.0 / (1.0 + jnp.exp(-c)))
    o_ref[...] = _mm3(s, w_ref[...]) + b_ref[...]


def _adaln(c, w_ada, b_ada):
    rows = c.shape[0]
    n = w_ada.shape[1]
    tn = 1024
    return pl.pallas_call(
        _adaln_body,
        out_shape=jax.ShapeDtypeStruct((rows, n), F32),
        grid=(n // tn,),
        in_specs=[pl.BlockSpec((rows, D_MODEL), lambda j: (0, 0)),
                  pl.BlockSpec((D_MODEL, tn), lambda j: (0, j)),
                  pl.BlockSpec((1, tn), lambda j: (0, j))],
        out_specs=pl.BlockSpec((rows, tn), lambda j: (0, j)),
        compiler_params=_cparams(("arbitrary",)),
        name="adaln",
    )(c, w_ada, b_ada.reshape(1, n))


def _modulated_norm(x, g, sc, sh):
    ms = jnp.mean(x * x, axis=-1, keepdims=True)
    return (x * lax.rsqrt(ms + NORM_EPS) * g) * (1.0 + sc) + sh


def _inproj_body(x_ref, g_ref, sc_ref, sh_ref, w_ref, o_ref, h_s):
    @pl.when(pl.program_id(1) == 0)
    def _():
        h_s[...] = _bf(_modulated_norm(x_ref[...], g_ref[...], sc_ref[...], sh_ref[...]))

    o_ref[...] = _dot(h_s[...], w_ref[...])


def _inproj(x2d, g, sc, sh, w_bf, tm, tiles_per_group):
    n = x2d.shape[0]
    r = sc.shape[1]
    tn = 512
    mod_spec = pl.BlockSpec((None, r, D_MODEL), lambda i, j: (i // tiles_per_group, 0, 0))
    return pl.pallas_call(
        _inproj_body,
        out_shape=jax.ShapeDtypeStruct((n, PROJ_PAD), F32),
        grid=(n // tm, PROJ_PAD // tn),
        in_specs=[pl.BlockSpec((tm, D_MODEL), lambda i, j: (i, 0)),
                  pl.BlockSpec((1, D_MODEL), lambda i, j: (0, 0)),
                  mod_spec, mod_spec,
                  pl.BlockSpec((D_MODEL, tn), lambda i, j: (0, j))],
        out_specs=pl.BlockSpec((tm, tn), lambda i, j: (i, j)),
        scratch_shapes=[pltpu.VMEM((tm, D_MODEL), BF16)],
        compiler_params=_cparams(("arbitrary", "arbitrary")),
        name="inproj",
    )(x2d, g.reshape(1, D_MODEL), sc, sh, w_bf)


def _head_norm(x, gain, gmat):
    ss = _group_sum(x * x, gmat)
    return x * lax.rsqrt(ss * (1.0 / HEAD_DIM) + NORM_EPS) * gain


def _attn_prompt_body(q_ref, k_ref, v_ref, qg_ref, kg_ref, o_ref, kc_ref, vc_ref,
                      qn_s, kn_s, m_s, l_s, acc_s):
    t_len = q_ref.shape[0]
    win = kc_ref.shape[0]
    gmat = _group_ones(LANE)
    rows_per = 512

    def prologue(c, carry):
        rows = pl.ds(pl.multiple_of(c * rows_per, rows_per), rows_per)
        qn_s[rows, :] = _head_norm(q_ref[rows, :], qg_ref[...], gmat) * (HEAD_DIM ** -0.5)
        kn_s[rows, :] = _head_norm(k_ref[rows, :], kg_ref[...], gmat)
        m_s[rows, :] = jnp.full((rows_per, LANE), NEG, F32)
        l_s[rows, :] = jnp.zeros((rows_per, LANE), F32)
        acc_s[rows, :] = jnp.zeros((rows_per, LANE), F32)
        return carry

    lax.fori_loop(0, t_len // rows_per, prologue, 0)
    kc_ref[...] = kn_s[t_len - win:, :]
    vc_ref[...] = v_ref[t_len - win:, :]

    qi = lax.broadcasted_iota(jnp.int32, (QBLK, 2 * QBLK), 0)
    kj = lax.broadcasted_iota(jnp.int32, (QBLK, 2 * QBLK), 1)
    in_prev = kj < QBLK
    upper = qi + QBLK
    head0 = lax.broadcasted_iota(jnp.int32, (QBLK, LANE), 1) < HEAD_DIM

    for _, dil in DIL_PATTERNS:
        nblk = t_len // dil // QBLK

        def rows_of(start, dil=dil):
            if dil == 1:
                return pl.ds(start, QBLK)
            return pl.ds(start, QBLK, stride=dil)

        def unit(u, dil=dil, nblk=nblk, rows_of=rows_of):
            res = u // nblk
            blk = u - res * nblk
            cur = rows_of(blk * (QBLK * dil) + res)
            prev = rows_of(jnp.maximum(blk - 1, 0) * (QBLK * dil) + res)
            qb = qn_s[cur, :]
            kcat = _bf(jnp.concatenate([kn_s[prev, :], kn_s[cur, :]], axis=0))
            vcat = _bf(jnp.concatenate([v_ref[prev, :], v_ref[cur, :]], axis=0))
            m_old, l_old, acc_old = m_s[cur, :], l_s[cur, :], acc_s[cur, :]
            lower = jnp.where(in_prev, qi + jnp.where(blk > 0, 0, 4 * QBLK), 0)
            yield
            stats = []
            for sel in (head0, ~head0):
                qh = _bf(jnp.where(sel, qb, 0.0))
                s = jnp.where(kj >= lower, jnp.where(kj <= upper, _dot(qh, kcat, _NT), NEG), NEG)
                yield
                mb = jnp.max(s, axis=-1, keepdims=True)
                p = jnp.exp(s - mb)
                lb = jnp.sum(p, axis=-1, keepdims=True)
                yield
                stats.append((mb, lb, _dot(_bf(p), vcat)))
                yield
            mb = jnp.where(head0, stats[0][0], stats[1][0])
            lb = jnp.where(head0, stats[0][1], stats[1][1])
            ob = jnp.where(head0, stats[0][2], stats[1][2])
            m_new = jnp.maximum(m_old, mb)
            a_old = jnp.exp(m_old - m_new)
            a_blk = jnp.exp(mb - m_new)
            yield
            l_s[cur, :] = l_old * a_old + lb * a_blk
            acc_s[cur, :] = acc_old * a_old + ob * a_blk
            m_s[cur, :] = m_new

        def units(ui, carry, unit=unit):
            _lockstep([unit(ui * ATTN_INTERLEAVE + k) for k in range(ATTN_INTERLEAVE)])
            return carry

        lax.fori_loop(0, dil * nblk // ATTN_INTERLEAVE, units, 0)

    def epilogue(c, carry):
        rows = pl.ds(pl.multiple_of(c * rows_per, rows_per), rows_per)
        o_ref[rows, :] = acc_s[rows, :] / l_s[rows, :]
        return carry

    lax.fori_loop(0, t_len // rows_per, epilogue, 0)


def _attn_prompt(proj3, q_gain, k_gain, win):
    b, t, _ = proj3.shape
    pairs = ATTN_WIDTH // LANE
    qoff, koff, voff = P_PAD // LANE, (P_PAD + ATTN_WIDTH) // LANE, (P_PAD + 2 * ATTN_WIDTH) // LANE
    col = lambda off: pl.BlockSpec((None, t, LANE), lambda bi, hp: (bi, 0, off + hp))
    gain = lambda gvec: jnp.tile(gvec, 2).reshape(1, LANE)
    return pl.pallas_call(
        _attn_prompt_body,
        out_shape=(jax.ShapeDtypeStruct((b, t, ATTN_WIDTH), F32),
                   jax.ShapeDtypeStruct((b, win, ATTN_WIDTH), F32),
                   jax.ShapeDtypeStruct((b, win, ATTN_WIDTH), F32)),
        grid=(b, pairs),
        in_specs=[col(qoff), col(koff), col(voff),
                  pl.BlockSpec((1, LANE), lambda bi, hp: (0, 0)),
                  pl.BlockSpec((1, LANE), lambda bi, hp: (0, 0))],
        out_specs=(pl.BlockSpec((None, t, LANE), lambda bi, hp: (bi, 0, hp)),
                   pl.BlockSpec((None, win, LANE), lambda bi, hp: (bi, 0, hp)),
                   pl.BlockSpec((None, win, LANE), lambda bi, hp: (bi, 0, hp))),
        scratch_shapes=[pltpu.VMEM((t, LANE), F32) for _ in range(5)],
        compiler_params=_cparams(("arbitrary", "arbitrary")),
        name="attn_prompt",
    )(proj3, proj3, proj3, gain(q_gain), gain(k_gain))


def _attn_sample_body(q_ref, k_ref, v_ref, qg_ref, kg_ref, kt_ref, vt_ref, o_ref, kn_ref):
    def head_norm(x, gain):
        ms = jnp.mean(x * x, axis=1, keepdims=True)
        return x * lax.rsqrt(ms + NORM_EPS) * gain

    qn = head_norm(q_ref[...], qg_ref[...]) * (HEAD_DIM ** -0.5)
    kn = head_norm(k_ref[...], kg_ref[...])
    kn_ref[...] = kn
    length = kt_ref.shape[-1]
    dist = length - lax.broadcasted_iota(jnp.int32, (1, 1, length), 2)
    cnt = jnp.zeros((1, 1, length), F32)
    for window, dil in DIL_PATTERNS:
        cnt = cnt + jnp.where(dist <= window, jnp.where((dist & (dil - 1)) == 0, 1.0, 0.0), 0.0)
    s = jnp.sum(kt_ref[...] * qn, axis=1, keepdims=True)
    s_self = jnp.sum(kn * qn, axis=1, keepdims=True)
    top = jnp.maximum(jnp.max(jnp.where(cnt > 0, s, NEG), axis=-1, keepdims=True), s_self)
    e = jnp.where(cnt > 0, jnp.exp(s - top), 0.0) * cnt
    e_self = float(len(DIL_PATTERNS)) * jnp.exp(s_self - top)
    den = jnp.sum(e, axis=-1, keepdims=True) + e_self
    num = jnp.sum(vt_ref[...] * e, axis=-1, keepdims=True) + e_self * v_ref[...]
    o_ref[...] = num / den


def _attn_sample(q, k, v, q_gain, k_gain, cache_k, cache_v):
    bs = q.shape[0]
    length = cache_k.shape[1]
    assert length >= max(w for w, _ in DIL_PATTERNS) and all(d & (d - 1) == 0 for _, d in DIL_PATTERNS)
    heads = ATTN_WIDTH // HEAD_DIM
    hb = 8
    col = lambda a: a.reshape(bs, heads, HEAD_DIM, 1)
    time_minor = lambda c: jnp.transpose(c, (0, 2, 3, 1))
    vec_spec = pl.BlockSpec((None, hb, HEAD_DIM, 1), lambda b, h: (b, h, 0, 0))
    gain_spec = pl.BlockSpec((HEAD_DIM, 1), lambda b, h: (0, 0))
    cache_spec = pl.BlockSpec((None, hb, HEAD_DIM, length), lambda b, h: (b, h, 0, 0))
    out = jax.ShapeDtypeStruct((bs, heads, HEAD_DIM, 1), F32)
    attn, kn = pl.pallas_call(
        _attn_sample_body,
        out_shape=(out, out),
        grid=(bs, heads // hb),
        in_specs=[vec_spec, vec_spec, vec_spec, gain_spec, gain_spec, cache_spec, cache_spec],
        out_specs=(vec_spec, vec_spec),
        compiler_params=_cparams(("arbitrary", "arbitrary")),
        name="attn_sample",
    )(col(q), col(k), col(v), q_gain.reshape(HEAD_DIM, 1), k_gain.reshape(HEAD_DIM, 1),
      time_minor(cache_k), time_minor(cache_v))
    return attn.reshape(bs, ATTN_WIDTH), kn.reshape(bs, ATTN_WIDTH)


def _rwkv_prep_body(seq_mode, p_ref, prev_ref, mu_ref, w0_ref, a0_ref, kk_ref, ka_ref, rk_ref,
                    w2_ref, a2_ref, g2_ref,
                    r_o, lw_o, kf_o, v_o, kn_o, al_o, g_o, bonus_o):
    p = p_ref[...]
    if seq_mode:
        row0 = lax.broadcasted_iota(jnp.int32, p.shape, 0) == 0
        p_prev = jnp.where(row0, prev_ref[...], pltpu.roll(p, 1, 0))
    else:
        p_prev = prev_ref[...]
    z = p + (p_prev - p) * mu_ref[...]
    r = z[:, 0:RWKV_WIDTH]
    k = z[:, RWKV_WIDTH:2 * RWKV_WIDTH]
    v = z[:, 2 * RWKV_WIDTH:3 * RWKV_WIDTH]
    tail = z[:, 3 * RWKV_WIDTH:3 * RWKV_WIDTH + LORA_PAD]
    dw = _mm3(jnp.tanh(tail), w2_ref[...])
    da = _mm3(tail, a2_ref[...])
    g = _mm3(1.0 / (1.0 + jnp.exp(-tail)), g2_ref[...])
    u = -(w0_ref[...] + dw)
    softplus = jnp.maximum(u, 0.0) + jnp.log1p(jnp.exp(-jnp.abs(u)))
    lw = -jnp.exp(-softplus - 0.5)
    alpha = 1.0 / (1.0 + jnp.exp(-(a0_ref[...] + da)))
    gmat = _group_ones(LANE)
    kk = k * kk_ref[...]
    nrm = jnp.sqrt(_group_sum(kk * kk, gmat))
    kn = kk / jnp.maximum(nrm, 1e-12)
    kf = k * (1.0 + (alpha - 1.0) * ka_ref[...])
    bonus = _group_sum(r * kf * rk_ref[...], gmat) * v
    r_o[...] = r
    lw_o[...] = lw
    kf_o[...] = kf
    v_o[...] = v
    kn_o[...] = kn
    al_o[...] = alpha
    g_o[...] = g
    bonus_o[...] = bonus


def _rwkv_prep(seq_mode, proj, prev, params, tm, tiles_per_seq):
    n = proj.shape[0]
    mu, w0, a0, k_k, k_a, r_k, w2p, a2p, g2p = params
    p_spec = pl.BlockSpec((tm, P_PAD), lambda i: (i, 0))
    if seq_mode:
        prev_rows = _prev_rows(proj, prev, tm, tiles_per_seq)
        prev_spec = pl.BlockSpec((None, 1, P_PAD), lambda i: (i, 0, 0))
        prev_arg = prev_rows
    else:
        prev_spec = pl.BlockSpec((tm, P_PAD), lambda i: (i, 0))
        prev_arg = prev
    vec = lambda width: pl.BlockSpec((1, width), lambda i: (0, 0))
    mat = pl.BlockSpec((LORA_PAD, RWKV_WIDTH), lambda i: (0, 0))
    out_spec = pl.BlockSpec((tm, RWKV_WIDTH), lambda i: (i, 0))
    out = jax.ShapeDtypeStruct((n, RWKV_WIDTH), F32)
    return pl.pallas_call(
        functools.partial(_rwkv_prep_body, seq_mode),
        out_shape=(out,) * 8,
        grid=(n // tm,),
        in_specs=[p_spec, prev_spec, vec(P_PAD)] + [vec(RWKV_WIDTH)] * 5 + [mat] * 3,
        out_specs=(out_spec,) * 8,
        compiler_params=_cparams(("arbitrary",)),
        name="rwkv_prep_seq" if seq_mode else "rwkv_prep_row",
    )(proj, prev_arg, mu, w0, a0, k_k, k_a, r_k, w2p, a2p, g2p)


def _prev_rows(proj, shift0, tm, tiles_per_seq):
    n = proj.shape[0]
    tiles = n // tm
    last = proj[tm - 1::tm, :P_PAD][:tiles - 1]
    rows = jnp.concatenate([jnp.zeros((1, P_PAD), F32), last], axis=0).reshape(tiles // tiles_per_seq, tiles_per_seq, P_PAD)
    rows = rows.at[:, 0, :].set(shift0[:, 0, :])
    return rows.reshape(tiles, 1, P_PAD)


def _rwkv_scan_body(r_ref, lw_ref, kf_ref, v_ref, kn_ref, al_ref, g_ref, bonus_ref, lnw_ref, lnb_ref, h0_ref,
                    y_ref, hT_ref, h_s):
    c = CHUNK
    gc = RWKV_GROUP * c
    n = 2 * gc
    tt = r_ref.shape[0]
    n_inst = r_ref.shape[1] // LANE
    ti = pl.program_id(2)

    @pl.when(ti == 0)
    def _():
        h_s[...] = h0_ref[...]

    head0 = lax.broadcasted_iota(jnp.int32, (c, LANE), 1) < HEAD_DIM
    ri = lax.broadcasted_iota(jnp.int32, (n, n), 0)
    ci = lax.broadcasted_iota(jnp.int32, (n, n), 1)
    same_block = (ri // c) == (ci // c)
    strict = same_block & (ci < ri)
    incl = same_block & (ci <= ri)
    eye = ri == ci
    eye_l = lax.broadcasted_iota(jnp.int32, (LANE, LANE), 0) == lax.broadcasted_iota(jnp.int32, (LANE, LANE), 1)
    tr = lax.broadcasted_iota(jnp.int32, (gc, gc), 0)
    tc = lax.broadcasted_iota(jnp.int32, (gc, gc), 1)
    tri = jnp.where(((tr // c) == (tc // c)) & (tc <= tr), 1.0, 0.0).astype(BF16)
    gmat = _group_ones(LANE)

    def stack(x):
        parts = []
        for k in range(RWKV_GROUP):
            xk = x[k * c:(k + 1) * c, :]
            parts += [jnp.where(head0, xk, 0.0), jnp.where(head0, 0.0, xk)]
        return jnp.concatenate(parts, axis=0)

    def twice(x):
        parts = []
        for k in range(RWKV_GROUP):
            xk = x[k * c:(k + 1) * c, :]
            parts += [xk, xk]
        return jnp.concatenate(parts, axis=0)

    def solve(gi, j):
        rows = pl.ds(pl.multiple_of(gi * gc, gc), gc)
        ln = slice(j * LANE, (j + 1) * LANE)
        r, lw, kf, v = r_ref[rows, ln], lw_ref[rows, ln], kf_ref[rows, ln], v_ref[rows, ln]
        kn, al = kn_ref[rows, ln], al_ref[rows, ln]
        gcum = _mm2r(tri, lw)
        g_end = jnp.concatenate([jnp.broadcast_to(gcum[(k + 1) * c - 1:(k + 1) * c, :], (c, LANE))
                                 for k in range(RWKV_GROUP)], axis=0)
        at = -kn * jnp.exp(gcum - lw)
        rt = r * jnp.exp(gcum)
        inv = jnp.exp(-gcum)
        bt = kn * al * inv
        kt = kf * inv
        dec_end = jnp.exp(g_end)
        at_st, rt_st, v_st = stack(at), stack(rt), stack(v)
        b2_st, k2_st = stack(bt * dec_end), stack(kt * dec_end)
        big = _mm(jnp.concatenate([at_st, rt_st], axis=0), jnp.concatenate([twice(bt), twice(kt)], axis=0), _NT)
        a_ab = jnp.where(strict, big[0:n, 0:n], 0.0)
        a_ak = jnp.where(strict, big[0:n, n:2 * n], 0.0)
        a_rb = jnp.where(incl, big[n:2 * n, 0:n], 0.0)
        a_rk = jnp.where(incl, big[n:2 * n, n:2 * n], 0.0)
        x = jnp.where(eye, 1.0, 0.0) + a_ab
        yield
        pw = _mm(a_ab, a_ab)
        yield
        steps = int(np.log2(c)) - 1
        for s in range(steps):
            if s < steps - 1:
                both = _mm(pw, jnp.concatenate([pw, x], axis=1))
                x = x + both[:, n:]
                pw = both[:, :n]
            else:
                x = x + _mm(pw, x)
            yield
        av_st = _mm(a_ak, v_st)
        yield
        wu_uv = _mm(x, jnp.concatenate([at_st, av_st], axis=1))
        wu_st, uv_st = wu_uv[:, :LANE], wu_uv[:, LANE:]
        yield
        rb = _mm(a_rb, wu_uv)
        r2_st = rt_st + rb[:, :LANE]
        yv_st = rb[:, LANE:] + _mm(a_rk, v_st)
        yield
        trans, add = [], []
        for k in range(RWKV_GROUP):
            blk = slice(2 * k * c, 2 * (k + 1) * c)
            trans.append(jnp.where(eye_l, dec_end[k * c:k * c + 1, :], 0.0) + _mm(b2_st[blk], wu_st[blk], _TN))
            add.append(_mm(jnp.concatenate([b2_st[blk], k2_st[blk]], axis=0),
                           jnp.concatenate([uv_st[blk], v_st[blk]], axis=0), _TN))
        yield
        hmat = h_s[j]
        ys = []
        for k in range(RWKV_GROUP):
            blk = slice(2 * k * c, 2 * (k + 1) * c)
            y_st = _mm(r2_st[blk], hmat) + yv_st[blk]
            hmat = _mm(trans[k], hmat) + add[k]
            ys.append(y_st[:c, :] + y_st[c:, :])
            yield
        h_s[j] = hmat
        y = ys[0] if len(ys) == 1 else jnp.concatenate(ys, axis=0)
        mean = _group_sum(y, gmat) * (1.0 / HEAD_DIM)
        dev = y - mean
        var = _group_sum(dev * dev, gmat) * (1.0 / HEAD_DIM)
        yn = dev * lax.rsqrt(var + GN_EPS) * lnw_ref[:, ln] + lnb_ref[:, ln]
        y_ref[rows, ln] = (yn + bonus_ref[rows, ln]) * g_ref[rows, ln]

    def group(gi, carry):
        _lockstep([solve(gi, j) for j in range(n_inst)])
        return carry

    lax.fori_loop(0, tt // gc, group, 0)

    @pl.when(ti == pl.num_programs(2) - 1)
    def _():
        hT_ref[...] = h_s[...]


def _rwkv_scan(vecs, lnw, lnb, h0, b, t):
    pairs = RWKV_WIDTH // LANE
    tt = min(t, 256)
    nt = t // tt
    ni = RWKV_INTERLEAVE
    seq = pl.BlockSpec((tt, ni * LANE), lambda bi, hp, ti: (bi * nt + ti, hp))
    vec = pl.BlockSpec((1, ni * LANE), lambda bi, hp, ti: (0, hp))
    st = pl.BlockSpec((None, ni, LANE, LANE), lambda bi, hp, ti: (bi, hp, 0, 0))
    return pl.pallas_call(
        _rwkv_scan_body,
        out_shape=(jax.ShapeDtypeStruct((b * t, RWKV_WIDTH), F32),
                   jax.ShapeDtypeStruct((b, pairs, LANE, LANE), F32)),
        grid=(b, pairs // ni, nt),
        in_specs=[seq] * 8 + [vec, vec, st],
        out_specs=(seq, st),
        scratch_shapes=[pltpu.VMEM((ni, LANE, LANE), F32)],
        compiler_params=_cparams(("arbitrary", "arbitrary", "arbitrary")),
        name="rwkv_scan",
    )(*vecs, lnw.reshape(1, RWKV_WIDTH), lnb.reshape(1, RWKV_WIDTH), h0)


def _rwkv_step_body(s_ref, lw_ref, kn_ref, al_ref, k_ref, r_ref, v_ref, g_ref, bonus_ref, lnw_ref, lnb_ref,
                    y_ref, so_ref):
    s = s_ref[...]
    kn = kn_ref[...]
    sa = jnp.sum(s * (-kn), axis=-1, keepdims=True)
    s_new = s * jnp.exp(lw_ref[...]) + sa * (kn * al_ref[...]) + v_ref[...] * k_ref[...]
    so_ref[...] = s_new
    y = jnp.sum(s_new * r_ref[...], axis=-1, keepdims=True)
    mean = jnp.mean(y, axis=1, keepdims=True)
    dev = y - mean
    var = jnp.mean(dev * dev, axis=1, keepdims=True)
    yn = dev * lax.rsqrt(var + GN_EPS) * lnw_ref[...] + lnb_ref[...]
    y_ref[...] = (yn + bonus_ref[...]) * g_ref[...]


def _rwkv_step(state, vecs, lnw, lnb):
    bs = state.shape[0]
    r, lw, kf, v, kn, al, g, bonus = vecs
    rowv = lambda a: a.reshape(bs, RWKV_HEADS, 1, HEAD_DIM)
    colv = lambda a: a.reshape(bs, RWKV_HEADS, HEAD_DIM, 1)
    row_spec = pl.BlockSpec((None, RWKV_HEADS, 1, HEAD_DIM), lambda b: (b, 0, 0, 0))
    col_spec = pl.BlockSpec((None, RWKV_HEADS, HEAD_DIM, 1), lambda b: (b, 0, 0, 0))
    st_spec = pl.BlockSpec((None, RWKV_HEADS, HEAD_DIM, HEAD_DIM), lambda b: (b, 0, 0, 0))
    par_spec = pl.BlockSpec((RWKV_HEADS, HEAD_DIM, 1), lambda b: (0, 0, 0))
    y, s_new = pl.pallas_call(
        _rwkv_step_body,
        out_shape=(jax.ShapeDtypeStruct((bs, RWKV_HEADS, HEAD_DIM, 1), F32),
                   jax.ShapeDtypeStruct(state.shape, F32)),
        grid=(bs,),
        in_specs=[st_spec] + [row_spec] * 5 + [col_spec] * 3 + [par_spec] * 2,
        out_specs=(col_spec, st_spec),
        compiler_params=_cparams(("arbitrary",)),
        name="rwkv_step",
    )(state, rowv(lw), rowv(kn), rowv(al), rowv(kf), rowv(r), colv(v), colv(g), colv(bonus),
      lnw.reshape(RWKV_HEADS, HEAD_DIM, 1), lnb.reshape(RWKV_HEADS, HEAD_DIM, 1))
    return y.reshape(bs, RWKV_WIDTH), s_new


def _outproj_body(a_ref, r_ref, x_ref, gt_ref, w_ref, g2_ref, sc_ref, sh_ref, x1_ref, h2_ref):
    y = _dot(_bf(a_ref[...]), w_ref[0:ATTN_WIDTH, :]) + _dot(_bf(r_ref[...]), w_ref[ATTN_WIDTH:, :])
    x1 = x_ref[...] + gt_ref[...] * y
    x1_ref[...] = x1
    h2_ref[...] = _modulated_norm(x1, g2_ref[...], sc_ref[...], sh_ref[...])


def _outproj(attn, rw, x2d, gt, w_bf, g2, sc, sh, tm, tiles_per_group):
    n = x2d.shape[0]
    r = gt.shape[1]
    half = pl.BlockSpec((tm, ATTN_WIDTH), lambda i: (i, 0))
    full = pl.BlockSpec((tm, D_MODEL), lambda i: (i, 0))
    mod = pl.BlockSpec((None, r, D_MODEL), lambda i: (i // tiles_per_group, 0, 0))
    return pl.pallas_call(
        _outproj_body,
        out_shape=(jax.ShapeDtypeStruct((n, D_MODEL), F32), jax.ShapeDtypeStruct((n, D_MODEL), F32)),
        grid=(n // tm,),
        in_specs=[half, half, full, mod, pl.BlockSpec((D_MODEL, D_MODEL), lambda i: (0, 0)),
                  pl.BlockSpec((1, D_MODEL), lambda i: (0, 0)), mod, mod],
        out_specs=(full, full),
        compiler_params=_cparams(("arbitrary",)),
        name="outproj",
    )(attn, rw, x2d, gt, w_bf, g2.reshape(1, D_MODEL), sc, sh)


def _staircase():
    pairs = [(a, b) for a in range(PEER_TOPK) for b in range(PEER_TOPK) if (a + 1) * (b + 1) <= PEER_TOPK]
    return pairs


_CAND = _staircase()
_CAND_ROWS = 56


def _extract_top(s, count, rows):
    rank = jnp.full(s.shape, 99.0, F32)
    vals = []
    limit = s.shape[0]
    for kth in range(count):
        mx = jnp.max(s, axis=0, keepdims=True)
        first = jnp.min(jnp.where(s == mx, rows, float(limit)), axis=0, keepdims=True)
        hit = rows == first
        rank = jnp.where(hit, float(kth), rank)
        s = jnp.where(hit, NEG, s)
        vals.append(mx)
        yield
    return vals, rank


def _router_body(h2_ref, wq_ref, keys_ref, oh_ref, h2t_ref, cnt_ref, e1_ref, rk_ref, e2_ref):
    h2 = h2_ref[...]
    tm = h2.shape[0]
    h2t_ref[...] = _bf(h2.T)
    h2b = _bf(h2)
    rows = lax.broadcasted_iota(jnp.int32, (PEER_KEYS, tm), 0).astype(F32)
    crow = lax.broadcasted_iota(jnp.int32, (_CAND_ROWS, tm), 0).astype(F32)

    def route(h):
        scores = []
        for half in range(2):
            g = 2 * h + half
            wq = wq_ref[pl.ds(pl.multiple_of(g * PEER_KEYS, PEER_KEYS), PEER_KEYS), :]
            q_t = _dot(wq, h2b, _NT)
            scores.append(_mm3(keys_ref[g], q_t))
        (tops0, rank0), (tops1, rank1) = yield from _lockstep_gen(
            [_extract_top(scores[0], PEER_TOPK, rows), _extract_top(scores[1], PEER_TOPK, rows)])
        cand = [tops0[a] + tops1[b] for a, b in _CAND]
        cand += [jnp.full((1, tm), NEG, F32)] * (_CAND_ROWS - len(cand))
        cand = jnp.concatenate(cand, axis=0)
        top = tops0[0] + tops1[0]
        taken = jnp.zeros((_CAND_ROWS, tm), F32)
        zsum = jnp.zeros((1, tm), F32)
        for _ in range(PEER_TOPK):
            mx = jnp.max(cand, axis=0, keepdims=True)
            first = jnp.min(jnp.where(cand == mx, crow, float(_CAND_ROWS)), axis=0, keepdims=True)
            hit = crow == first
            taken = jnp.where(hit, 1.0, taken)
            cand = jnp.where(hit, NEG, cand)
            zsum = zsum + jnp.exp(mx - top)
            yield
        per_rank = _dot(oh_ref[...], _bf(taken))
        cnt = jnp.zeros((PEER_KEYS, tm), F32)
        for kth in range(PEER_TOPK):
            cnt = jnp.where(rank0 == float(kth), per_rank[kth:kth + 1, :], cnt)
        cnt_ref[h] = cnt
        rk_ref[h] = _bf(rank1)
        e1_ref[h] = jnp.where(rank0 < PEER_TOPK, jnp.exp(scores[0] - tops0[0]), 0.0)
        e2_ref[h] = _bf(jnp.where(rank1 < PEER_TOPK, jnp.exp(scores[1] - tops1[0]) / zsum, 0.0))

    def heads(hi, carry):
        _lockstep([route(hi * ROUTER_INTERLEAVE + k) for k in range(ROUTER_INTERLEAVE)])
        return carry

    lax.fori_loop(0, PEER_HEADS // ROUTER_INTERLEAVE, heads, 0)


def _router(h2, wq_t_bf, keys, tm):
    n = h2.shape[0]
    onehot = np.zeros((PEER_TOPK, _CAND_ROWS), np.float32)
    for idx, (a, _) in enumerate(_CAND):
        onehot[a, idx] = 1.0
    tok = pl.BlockSpec((PEER_HEADS, PEER_KEYS, tm), lambda i: (0, 0, i))
    tok_shape = lambda dt: jax.ShapeDtypeStruct((PEER_HEADS, PEER_KEYS, n), dt)
    return pl.pallas_call(
        _router_body,
        out_shape=(jax.ShapeDtypeStruct((D_MODEL, n), BF16),
                   tok_shape(F32), tok_shape(F32), tok_shape(BF16), tok_shape(BF16)),
        grid=(n // tm,),
        in_specs=[pl.BlockSpec((tm, D_MODEL), lambda i: (i, 0)),
                  pl.BlockSpec((D_MODEL, D_MODEL), lambda i: (0, 0)),
                  pl.BlockSpec((2 * PEER_HEADS, PEER_KEYS, PEER_KEYS), lambda i: (0, 0, 0)),
                  pl.BlockSpec((PEER_TOPK, _CAND_ROWS), lambda i: (0, 0))],
        out_specs=(pl.BlockSpec((D_MODEL, tm), lambda i: (0, i)), tok, tok, tok, tok),
        compiler_params=_cparams(("arbitrary",)),
        name="peer_router",
    )(h2, wq_t_bf, keys, jnp.asarray(onehot, BF16))


def _experts_body(h2t_ref, u_ref, vt_ref, cnt_ref, e1_ref, rk_ref, e2_ref, x1_ref, gt_ref, y_ref, acc_s, p_s):
    j = pl.program_id(1)
    last = pl.num_programs(1) - 1
    eb = u_ref.shape[0]
    per = eb // PEER_KEYS
    cur = j % 2

    @pl.when(j == 0)
    def _():
        acc_s[...] = jnp.zeros(acc_s.shape, F32)
        p_s[1] = jnp.zeros(p_s.shape[1:], BF16)

    acc_s[...] += _dot(vt_ref[...], p_s[1 - cur])
    blk = jnp.minimum(j, last - 1)
    h2t = h2t_ref[...]
    for ii in range(per):
        sub = slice(ii * PEER_KEYS, (ii + 1) * PEER_KEYS)
        i = blk * per + ii
        wt = None
        for h in range(PEER_HEADS):
            partners = _bf(cnt_ref[h, pl.ds(i, 1), :])
            e1 = _bf(e1_ref[h, pl.ds(i, 1), :])
            term = jnp.where(rk_ref[h] < partners, e2_ref[h], 0.0) * e1
            wt = term if wt is None else wt + term
        act_in = _dot(u_ref[sub, :], h2t)
        act = 0.5 * act_in * (1.0 + lax.erf(act_in * (2.0 ** -0.5)))
        p_s[cur, sub, :] = wt * _bf(act)

    @pl.when(j == last)
    def _():
        y_ref[...] = x1_ref[...] + gt_ref[...] * acc_s[...].T


def _experts(h2t, u_bf, vt_bf, route, x1, gt, tm, tiles_per_group):
    n = x1.shape[0]
    eb = 512
    nblk = PEER_EXPERTS // eb
    r = gt.shape[1]
    tok = pl.BlockSpec((PEER_HEADS, PEER_KEYS, tm), lambda i, j: (0, 0, i))
    return pl.pallas_call(
        _experts_body,
        out_shape=jax.ShapeDtypeStruct((n, D_MODEL), F32),
        grid=(n // tm, nblk + 1),
        in_specs=[pl.BlockSpec((D_MODEL, tm), lambda i, j: (0, i)),
                  pl.BlockSpec((eb, D_MODEL), lambda i, j: (jnp.minimum(j, nblk - 1), 0)),
                  pl.BlockSpec((D_MODEL, eb), lambda i, j: (0, jnp.maximum(j - 1, 0))),
                  tok, tok, tok, tok,
                  pl.BlockSpec((tm, D_MODEL), lambda i, j: (i, 0)),
                  pl.BlockSpec((None, r, D_MODEL), lambda i, j: (i // tiles_per_group, 0, 0))],
        out_specs=pl.BlockSpec((tm, D_MODEL), lambda i, j: (i, 0)),
        scratch_shapes=[pltpu.VMEM((D_MODEL, tm), F32), pltpu.VMEM((2, eb, tm), BF16)],
        compiler_params=_cparams(("arbitrary", "arbitrary")),
        name="peer_experts",
    )(h2t, u_bf, vt_bf, *route, x1, gt)


def _pad_rows(a, rows):
    return jnp.pad(a, ((0, rows - a.shape[0]),) + ((0, 0),) * (a.ndim - 1))


def _state_to_pairs(state):
    b = state.shape[0]
    h = jnp.swapaxes(state, -1, -2).reshape(b, RWKV_HEADS // 2, 2, HEAD_DIM, HEAD_DIM)
    z = jnp.zeros_like(h[:, :, 0])
    top = jnp.concatenate([h[:, :, 0], z], axis=-1)
    bot = jnp.concatenate([z, h[:, :, 1]], axis=-1)
    return jnp.concatenate([top, bot], axis=-2)


def _pairs_to_state(hp):
    b = hp.shape[0]
    h0 = hp[:, :, :HEAD_DIM, :HEAD_DIM]
    h1 = hp[:, :, HEAD_DIM:, HEAD_DIM:]
    h = jnp.stack([h0, h1], axis=2).reshape(b, RWKV_HEADS, HEAD_DIM, HEAD_DIM)
    return jnp.swapaxes(h, -1, -2)


def kernel(x_prompt, x_sample, cache_attn_k, cache_attn_v, state_rwkv, state_rwkv_shift, c_prompt, c_sample, w_ada, b_ada, norm1_g, norm2_g, w_in, q_gain, k_gain, rwkv_mu, rwkv_w0, rwkv_w2, rwkv_a0, rwkv_a2, rwkv_g2, rwkv_k_k, rwkv_k_a, rwkv_r_k, rwkv_lnx_w, rwkv_lnx_b, w_out, peer_w_q, peer_sub_keys, peer_u, peer_v):
    b, t, _ = x_prompt.shape
    bs = x_sample.shape[0]
    n = b * t
    heads = ATTN_WIDTH // HEAD_DIM
    win = min(CACHE_LEN, t)

    w_in_bf = jnp.concatenate([w_in[:, 3 * ATTN_WIDTH:], jnp.zeros((D_MODEL, P_PAD - RWKV_PROJ), F32),
                               w_in[:, :3 * ATTN_WIDTH]], axis=1).astype(BF16)
    w_out_bf = w_out.astype(BF16)
    wq_t_bf = peer_w_q.T.astype(BF16)
    keys = peer_sub_keys.reshape(2 * PEER_HEADS, PEER_KEYS, PEER_KEYS)
    u_bf = peer_u.astype(BF16)
    vt_bf = peer_v.T.astype(BF16)
    lora = lambda w, off: jnp.zeros((LORA_PAD, RWKV_WIDTH), F32).at[off:off + w.shape[0]].set(w)
    row = lambda a: a.reshape(1, -1)
    rwkv_params = (row(jnp.pad(rwkv_mu, (0, P_PAD - RWKV_PROJ))), row(rwkv_w0), row(rwkv_a0), row(rwkv_k_k),
                   row(rwkv_k_a), row(rwkv_r_k), lora(rwkv_w2, 0), lora(rwkv_a2, DECAY_RANK),
                   lora(rwkv_g2, DECAY_RANK + A_RANK))

    c_rows = b + bs
    c_pad = -(-c_rows // 8) * 8
    mod = _adaln(_pad_rows(jnp.concatenate([c_prompt, c_sample], axis=0), c_pad), w_ada, b_ada)
    mod_p = [m.reshape(b, 1, D_MODEL) for m in jnp.split(mod[:b], 6, axis=-1)]
    mod_s = [m.reshape(1, bs, D_MODEL) for m in jnp.split(mod[b:c_rows], 6, axis=-1)]

    tm_p = min(1024, t)
    xp = x_prompt.reshape(n, D_MODEL)
    proj_p = _inproj(xp, norm1_g, mod_p[1], mod_p[0], w_in_bf, tm_p, t // tm_p)
    attn_p, k_cache, v_cache = _attn_prompt(proj_p.reshape(b, t, PROJ_PAD), q_gain, k_gain, win)
    tm_r = min(256, t)
    shift0 = jnp.zeros((b, 1, P_PAD), F32)
    vecs_p = _rwkv_prep(True, proj_p, shift0, rwkv_params, tm_r, t // tm_r)
    rw_p, h_fin = _rwkv_scan(vecs_p, rwkv_lnx_w, rwkv_lnx_b, jnp.zeros((b, RWKV_HEADS // 2, LANE, LANE), F32), b, t)
    tm_o = min(256, t)
    x1_p, h2_p = _outproj(attn_p.reshape(n, ATTN_WIDTH), rw_p, xp, mod_p[2], w_out_bf, norm2_g,
                          mod_p[4], mod_p[3], tm_o, t // tm_o)
    tm_q = min(256, t)
    tm_e = min(512, t)
    h2t_p, *route_p = _router(h2_p, wq_t_bf, keys, tm_q)
    y_p = _experts(h2t_p, u_bf, vt_bf, route_p, x1_p, mod_p[5], tm_e, t // tm_e)

    xs = x_sample.reshape(bs, D_MODEL)
    proj_s = _inproj(xs, norm1_g, mod_s[1], mod_s[0], w_in_bf, bs, 1)
    q_s = proj_s[:, P_PAD:P_PAD + ATTN_WIDTH]
    k_s = proj_s[:, P_PAD + ATTN_WIDTH:P_PAD + 2 * ATTN_WIDTH]
    v_s = proj_s[:, P_PAD + 2 * ATTN_WIDTH:]
    attn_s, kn_s = _attn_sample(q_s, k_s, v_s, q_gain, k_gain, cache_attn_k, cache_attn_v)
    prev_s = jnp.pad(state_rwkv_shift.reshape(bs, RWKV_PROJ), ((0, 0), (0, P_PAD - RWKV_PROJ)))
    vecs_s = _rwkv_prep(False, proj_s, prev_s, rwkv_params, bs, 1)
    rw_s, state_s = _rwkv_step(state_rwkv, vecs_s, rwkv_lnx_w, rwkv_lnx_b)
    x1_s, h2_s = _outproj(attn_s, rw_s, xs, mod_s[2], w_out_bf, norm2_g, mod_s[4], mod_s[3], bs, 1)
    ns = -(-bs // LANE) * LANE
    h2t_s, *route_s = _router(_pad_rows(h2_s, ns), wq_t_bf, keys, LANE)
    gt2_s = _pad_rows(mod_s[5][0], ns).reshape(ns // LANE, LANE, D_MODEL)
    y_s = _experts(h2t_s, u_bf, vt_bf, route_s, _pad_rows(x1_s, ns), gt2_s, LANE, 1)[:bs]

    return (y_p.reshape(b, t, D_MODEL), y_s.reshape(bs, 1, D_MODEL),
            k_cache.reshape(b, win, heads, HEAD_DIM), v_cache.reshape(b, win, heads, HEAD_DIM),
            _pairs_to_state(h_fin), proj_p.reshape(b, t, PROJ_PAD)[:, t - 1:, :RWKV_PROJ],
            kn_s.reshape(bs, 1, heads, HEAD_DIM), v_s.reshape(bs, 1, heads, HEAD_DIM),
            state_s, proj_s[:, :RWKV_PROJ].reshape(bs, 1, RWKV_PROJ))
```

```python
import functools

import numpy as np
import jax
import jax.numpy as jnp
from jax import lax
from jax.experimental import pallas as pl
from jax.experimental.pallas import tpu as pltpu

F32 = jnp.float32
BF16 = jnp.bfloat16

D_MODEL = 2048
HEAD_DIM = 64
ATTN_WIDTH = 1024
RWKV_WIDTH = 1024
RWKV_HEADS = RWKV_WIDTH // HEAD_DIM
DECAY_RANK = 64
A_RANK = 64
GATE_RANK = 160
RWKV_PROJ = 3 * RWKV_WIDTH + DECAY_RANK + A_RANK + GATE_RANK
P_PAD = 3584
PROJ_PAD = P_PAD + 3 * ATTN_WIDTH
LORA_PAD = 384
DIL_PATTERNS = ((128, 1), (512, 4), (2048, 16))
CACHE_LEN = 2048
NORM_EPS = 1e-6
GN_EPS = HEAD_DIM * 1e-5
PEER_HEADS = 8
PEER_KEYS = 128
PEER_TOPK = 16
PEER_EXPERTS = PEER_KEYS * PEER_KEYS
NEG = -1e30

LANE = 128
QBLK = 128
CHUNK = 64
RWKV_GROUP = 1
RWKV_INTERLEAVE = 8
ROUTER_INTERLEAVE = 2
ATTN_INTERLEAVE = 4
VMEM_LIMIT = 56 * 1024 * 1024


def _cparams(sem):
    return pltpu.CompilerParams(dimension_semantics=sem, vmem_limit_bytes=VMEM_LIMIT)


_NN = (((1,), (0,)), ((), ()))
_NT = (((1,), (1,)), ((), ()))
_TN = (((0,), (0,)), ((), ()))


def _bf(x):
    return x.astype(BF16)


def _dot(a, b, dims=_NN):
    return lax.dot_general(a, b, dims, preferred_element_type=F32)


def _mm(a, b, dims=_NN):
    return _dot(_bf(a), _bf(b), dims)


def _split(x):
    hi = _bf(x)
    lo = _bf(x - hi.astype(F32))
    return hi, lo


def _mm3(a, b, dims=_NN):
    ah, al = _split(a)
    bh, bl = _split(b)
    return _dot(ah, bh, dims) + (_dot(ah, bl, dims) + _dot(al, bh, dims))


def _mm2l(a, b_exact, dims=_NN):
    ah, al = _split(a)
    return _dot(ah, b_exact, dims) + _dot(al, b_exact, dims)


def _mm2r(a_exact, b, dims=_NN):
    bh, bl = _split(b)
    return _dot(a_exact, bh, dims) + _dot(a_exact, bl, dims)


def _group_ones(n):
    r = lax.broadcasted_iota(jnp.int32, (n, n), 0) // HEAD_DIM
    c = lax.broadcasted_iota(jnp.int32, (n, n), 1) // HEAD_DIM
    return jnp.where(r == c, 1.0, 0.0).astype(BF16)


def _group_sum(x, g):
    cols = x.shape[1] // LANE
    parts = [_mm2l(x[:, c * LANE:(c + 1) * LANE], g) for c in range(cols)]
    return parts[0] if cols == 1 else jnp.concatenate(parts, axis=1)


def _lockstep_gen(gens):
    results = [None] * len(gens)
    live = list(enumerate(gens))
    while live:
        nxt = []
        for idx, gen in live:
            try:
                next(gen)
                nxt.append((idx, gen))
            except StopIteration as stop:
                results[idx] = stop.value
        live = nxt
        if live:
            yield
    return results


def _lockstep(gens):
    runner = _lockstep_gen(gens)
    while True:
        try:
            next(runner)
        except StopIteration as stop:
            return stop.value


def _adaln_body(c_ref, w_ref, b_ref, o_ref):
    c = c_ref[...]
    s = c * (1.0 / (1.0 + jnp.exp(-c)))
    o_ref[...] = _mm3(s, w_ref[...]) + b_ref[...]


def _adaln(c, w_ada, b_ada):
    rows = c.shape[0]
    n = w_ada.shape[1]
    tn = 1024
    return pl.pallas_call(
        _adaln_body,
        out_shape=jax.ShapeDtypeStruct((rows, n), F32),
        grid=(n // tn,),
        in_specs=[pl.BlockSpec((rows, D_MODEL), lambda j: (0, 0)),
                  pl.BlockSpec((D_MODEL, tn), lambda j: (0, j)),
                  pl.BlockSpec((1, tn), lambda j: (0, j))],
        out_specs=pl.BlockSpec((rows, tn), lambda j: (0, j)),
        compiler_params=_cparams(("arbitrary",)),
        name="adaln",
    )(c, w_ada, b_ada.reshape(1, n))


def _modulated_norm(x, g, sc, sh):
    ms = jnp.mean(x * x, axis=-1, keepdims=True)
    return (x * lax.rsqrt(ms + NORM_EPS) * g) * (1.0 + sc) + sh


def _inproj_body(x_ref, g_ref, sc_ref, sh_ref, w_ref, o_ref, h_s):
    @pl.when(pl.program_id(1) == 0)
    def _():
        h_s[...] = _bf(_modulated_norm(x_ref[...], g_ref[...], sc_ref[...], sh_ref[...]))

    o_ref[...] = _dot(h_s[...], w_ref[...])


def _inproj(x2d, g, sc, sh, w_bf, tm, tiles_per_group):
    n = x2d.shape[0]
    r = sc.shape[1]
    tn = 512
    mod_spec = pl.BlockSpec((None, r, D_MODEL), lambda i, j: (i // tiles_per_group, 0, 0))
    return pl.pallas_call(
        _inproj_body,
        out_shape=jax.ShapeDtypeStruct((n, PROJ_PAD), F32),
        grid=(n // tm, PROJ_PAD // tn),
        in_specs=[pl.BlockSpec((tm, D_MODEL), lambda i, j: (i, 0)),
                  pl.BlockSpec((1, D_MODEL), lambda i, j: (0, 0)),
                  mod_spec, mod_spec,
                  pl.BlockSpec((D_MODEL, tn), lambda i, j: (0, j))],
        out_specs=pl.BlockSpec((tm, tn), lambda i, j: (i, j)),
        scratch_shapes=[pltpu.VMEM((tm, D_MODEL), BF16)],
        compiler_params=_cparams(("arbitrary", "arbitrary")),
        name="inproj",
    )(x2d, g.reshape(1, D_MODEL), sc, sh, w_bf)


def _head_norm(x, gain, gmat):
    ss = _group_sum(x * x, gmat)
    return x * lax.rsqrt(ss * (1.0 / HEAD_DIM) + NORM_EPS) * gain


def _attn_prompt_body(q_ref, k_ref, v_ref, qg_ref, kg_ref, o_ref, kc_ref, vc_ref,
                      qn_s, kn_s, m_s, l_s, acc_s):
    t_len = q_ref.shape[0]
    win = kc_ref.shape[0]
    gmat = _group_ones(LANE)
    rows_per = 512

    def prologue(c, carry):
        rows = pl.ds(pl.multiple_of(c * rows_per, rows_per), rows_per)
        qn_s[rows, :] = _head_norm(q_ref[rows, :], qg_ref[...], gmat) * (HEAD_DIM ** -0.5)
        kn_s[rows, :] = _head_norm(k_ref[rows, :], kg_ref[...], gmat)
        m_s[rows, :] = jnp.full((rows_per, LANE), NEG, F32)
        l_s[rows, :] = jnp.zeros((rows_per, LANE), F32)
        acc_s[rows, :] = jnp.zeros((rows_per, LANE), F32)
        return carry

    lax.fori_loop(0, t_len // rows_per, prologue, 0)
    kc_ref[...] = kn_s[t_len - win:, :]
    vc_ref[...] = v_ref[t_len - win:, :]

    qi = lax.broadcasted_iota(jnp.int32, (QBLK, 2 * QBLK), 0)
    kj = lax.broadcasted_iota(jnp.int32, (QBLK, 2 * QBLK), 1)
    in_prev = kj < QBLK
    upper = qi + QBLK
    head0 = lax.broadcasted_iota(jnp.int32, (QBLK, LANE), 1) < HEAD_DIM

    for _, dil in DIL_PATTERNS:
        nblk = t_len // dil // QBLK

        def rows_of(start, dil=dil):
            if dil == 1:
                return pl.ds(start, QBLK)
            return pl.ds(start, QBLK, stride=dil)

        def unit(u, dil=dil, nblk=nblk, rows_of=rows_of):
            res = u // nblk
            blk = u - res * nblk
            cur = rows_of(blk * (QBLK * dil) + res)
            prev = rows_of(jnp.maximum(blk - 1, 0) * (QBLK * dil) + res)
            qb = qn_s[cur, :]
            kcat = _bf(jnp.concatenate([kn_s[prev, :], kn_s[cur, :]], axis=0))
            vcat = _bf(jnp.concatenate([v_ref[prev, :], v_ref[cur, :]], axis=0))
            m_old, l_old, acc_old = m_s[cur, :], l_s[cur, :], acc_s[cur, :]
            lower = jnp.where(in_prev, qi + jnp.where(blk > 0, 0, 4 * QBLK), 0)
            yield
            stats = []
            for sel in (head0, ~head0):
                qh = _bf(jnp.where(sel, qb, 0.0))
                s = jnp.where(kj >= lower, jnp.where(kj <= upper, _dot(qh, kcat, _NT), NEG), NEG)
                yield
                mb = jnp.max(s, axis=-1, keepdims=True)
                p = jnp.exp(s - mb)
                lb = jnp.sum(p, axis=-1, keepdims=True)
                yield
                stats.append((mb, lb, _dot(_bf(p), vcat)))
                yield
            mb = jnp.where(head0, stats[0][0], stats[1][0])
            lb = jnp.where(head0, stats[0][1], stats[1][1])
            ob = jnp.where(head0, stats[0][2], stats[1][2])
            m_new = jnp.maximum(m_old, mb)
            a_old = jnp.exp(m_old - m_new)
            a_blk = jnp.exp(mb - m_new)
            yield
            l_s[cur, :] = l_old * a_old + lb * a_blk
            acc_s[cur, :] = acc_old * a_old + ob * a_blk
            m_s[cur, :] = m_new

        def units(ui, carry, unit=unit):
            _lockstep([unit(ui * ATTN_INTERLEAVE + k) for k in range(ATTN_INTERLEAVE)])
            return carry

        lax.fori_loop(0, dil * nblk // ATTN_INTERLEAVE, units, 0)

    def epilogue(c, carry):
        rows = pl.ds(pl.multiple_of(c * rows_per, rows_per), rows_per)
        o_ref[rows, :] = acc_s[rows, :] / l_s[rows, :]
        return carry

    lax.fori_loop(0, t_len // rows_per, epilogue, 0)


def _attn_prompt(proj3, q_gain, k_gain, win):
    b, t, _ = proj3.shape
    pairs = ATTN_WIDTH // LANE
    qoff, koff, voff = P_PAD // LANE, (P_PAD + ATTN_WIDTH) // LANE, (P_PAD + 2 * ATTN_WIDTH) // LANE
    col = lambda off: pl.BlockSpec((None, t, LANE), lambda bi, hp: (bi, 0, off + hp))
    gain = lambda gvec: jnp.tile(gvec, 2).reshape(1, LANE)
    return pl.pallas_call(
        _attn_prompt_body,
        out_shape=(jax.ShapeDtypeStruct((b, t, ATTN_WIDTH), F32),
                   jax.ShapeDtypeStruct((b, win, ATTN_WIDTH), F32),
                   jax.ShapeDtypeStruct((b, win, ATTN_WIDTH), F32)),
        grid=(b, pairs),
        in_specs=[col(qoff), col(koff), col(voff),
                  pl.BlockSpec((1, LANE), lambda bi, hp: (0, 0)),
                  pl.BlockSpec((1, LANE), lambda bi, hp: (0, 0))],
        out_specs=(pl.BlockSpec((None, t, LANE), lambda bi, hp: (bi, 0, hp)),
                   pl.BlockSpec((None, win, LANE), lambda bi, hp: (bi, 0, hp)),
                   pl.BlockSpec((None, win, LANE), lambda bi, hp: (bi, 0, hp))),
        scratch_shapes=[pltpu.VMEM((t, LANE), F32) for _ in range(5)],
        compiler_params=_cparams(("arbitrary", "arbitrary")),
        name="attn_prompt",
    )(proj3, proj3, proj3, gain(q_gain), gain(k_gain))


def _attn_sample_body(qkv_ref, qg_ref, kg_ref, kt_ref, vt_ref, o_ref):
    def head_norm(x, gain):
        ms = jnp.mean(x * x, axis=1, keepdims=True)
        return x * lax.rsqrt(ms + NORM_EPS) * gain

    qkv = qkv_ref[...]
    qn = head_norm(qkv[:, :, 0:1], qg_ref[...]) * (HEAD_DIM ** -0.5)
    kn = head_norm(qkv[:, :, 1:2], kg_ref[...])
    o_ref[:, :, 1:2] = kn
    length = kt_ref.shape[-1]
    dist = length - lax.broadcasted_iota(jnp.int32, (1, 1, length), 2)
    cnt = jnp.zeros((1, 1, length), F32)
    for window, dil in DIL_PATTERNS:
        cnt = cnt + jnp.where(dist <= window, jnp.where((dist & (dil - 1)) == 0, 1.0, 0.0), 0.0)
    s = jnp.sum(kt_ref[...] * qn, axis=1, keepdims=True)
    s_self = jnp.sum(kn * qn, axis=1, keepdims=True)
    top = jnp.maximum(jnp.max(jnp.where(cnt > 0, s, NEG), axis=-1, keepdims=True), s_self)
    e = jnp.where(cnt > 0, jnp.exp(s - top), 0.0) * cnt
    e_self = float(len(DIL_PATTERNS)) * jnp.exp(s_self - top)
    den = jnp.sum(e, axis=-1, keepdims=True) + e_self
    num = jnp.sum(vt_ref[...] * e, axis=-1, keepdims=True) + e_self * qkv[:, :, 2:3]
    o_ref[:, :, 0:1] = num / den


def _attn_sample(q, k, v, q_gain, k_gain, cache_k, cache_v):
    bs = q.shape[0]
    length = cache_k.shape[1]
    assert length >= max(w for w, _ in DIL_PATTERNS) and all(d & (d - 1) == 0 for _, d in DIL_PATTERNS)
    heads = ATTN_WIDTH // HEAD_DIM
    hb = 8
    qkv = jnp.stack([q, k, v], axis=-1).reshape(bs, heads, HEAD_DIM, 3)
    time_minor = lambda c: jnp.transpose(c, (0, 2, 3, 1))
    in_spec = pl.BlockSpec((None, hb, HEAD_DIM, 3), lambda b, h: (b, h, 0, 0))
    out_spec = pl.BlockSpec((None, hb, HEAD_DIM, 2), lambda b, h: (b, h, 0, 0))
    gain_spec = pl.BlockSpec((HEAD_DIM, 1), lambda b, h: (0, 0))
    cache_spec = pl.BlockSpec((None, hb, HEAD_DIM, length), lambda b, h: (b, h, 0, 0))
    both = pl.pallas_call(
        _attn_sample_body,
        out_shape=jax.ShapeDtypeStruct((bs, heads, HEAD_DIM, 2), F32),
        grid=(bs, heads // hb),
        in_specs=[in_spec, gain_spec, gain_spec, cache_spec, cache_spec],
        out_specs=out_spec,
        compiler_params=_cparams(("arbitrary", "arbitrary")),
        name="attn_sample",
    )(qkv, q_gain.reshape(HEAD_DIM, 1), k_gain.reshape(HEAD_DIM, 1), time_minor(cache_k), time_minor(cache_v))
    return both[..., 0].reshape(bs, ATTN_WIDTH), both[..., 1].reshape(bs, ATTN_WIDTH)


def _rwkv_prep_body(seq_mode, p_ref, prev_ref, mu_ref, w0_ref, a0_ref, kk_ref, ka_ref, rk_ref,
                    w2_ref, a2_ref, g2_ref,
                    r_o, lw_o, kf_o, v_o, kn_o, al_o, g_o, bonus_o):
    p = p_ref[...]
    if seq_mode:
        row0 = lax.broadcasted_iota(jnp.int32, p.shape, 0) == 0
        p_prev = jnp.where(row0, prev_ref[...], pltpu.roll(p, 1, 0))
    else:
        p_prev = prev_ref[...]
    z = p + (p_prev - p) * mu_ref[...]
    r = z[:, 0:RWKV_WIDTH]
    k = z[:, RWKV_WIDTH:2 * RWKV_WIDTH]
    v = z[:, 2 * RWKV_WIDTH:3 * RWKV_WIDTH]
    tail = z[:, 3 * RWKV_WIDTH:3 * RWKV_WIDTH + LORA_PAD]
    dw = _mm3(jnp.tanh(tail), w2_ref[...])
    da = _mm3(tail, a2_ref[...])
    g = _mm3(1.0 / (1.0 + jnp.exp(-tail)), g2_ref[...])
    u = -(w0_ref[...] + dw)
    softplus = jnp.maximum(u, 0.0) + jnp.log1p(jnp.exp(-jnp.abs(u)))
    lw = -jnp.exp(-softplus - 0.5)
    alpha = 1.0 / (1.0 + jnp.exp(-(a0_ref[...] + da)))
    gmat = _group_ones(LANE)
    kk = k * kk_ref[...]
    nrm = jnp.sqrt(_group_sum(kk * kk, gmat))
    kn = kk / jnp.maximum(nrm, 1e-12)
    kf = k * (1.0 + (alpha - 1.0) * ka_ref[...])
    bonus = _group_sum(r * kf * rk_ref[...], gmat) * v
    lw_o[...] = lw
    for ref, val in ((r_o, r), (kf_o, kf), (v_o, v), (kn_o, kn), (al_o, alpha), (g_o, g), (bonus_o, bonus)):
        ref[...] = val.astype(ref.dtype)


def _rwkv_prep(seq_mode, proj, prev, params, tm, tiles_per_seq):
    n = proj.shape[0]
    mu, w0, a0, k_k, k_a, r_k, w2p, a2p, g2p = params
    p_spec = pl.BlockSpec((tm, P_PAD), lambda i: (i, 0))
    if seq_mode:
        prev_rows = _prev_rows(proj, prev, tm, tiles_per_seq)
        prev_spec = pl.BlockSpec((None, 1, P_PAD), lambda i: (i, 0, 0))
        prev_arg = prev_rows
    else:
        prev_spec = pl.BlockSpec((tm, P_PAD), lambda i: (i, 0))
        prev_arg = prev
    vec = lambda width: pl.BlockSpec((1, width), lambda i: (0, 0))
    mat = pl.BlockSpec((LORA_PAD, RWKV_WIDTH), lambda i: (0, 0))
    out_spec = pl.BlockSpec((tm, RWKV_WIDTH), lambda i: (i, 0))
    vec_dtype = BF16 if seq_mode else F32
    out = lambda dt: jax.ShapeDtypeStruct((n, RWKV_WIDTH), dt)
    return pl.pallas_call(
        functools.partial(_rwkv_prep_body, seq_mode),
        out_shape=(out(vec_dtype), out(F32)) + (out(vec_dtype),) * 6,
        grid=(n // tm,),
        in_specs=[p_spec, prev_spec, vec(P_PAD)] + [vec(RWKV_WIDTH)] * 5 + [mat] * 3,
        out_specs=(out_spec,) * 8,
        compiler_params=_cparams(("arbitrary",)),
        name="rwkv_prep_seq" if seq_mode else "rwkv_prep_row",
    )(proj, prev_arg, mu, w0, a0, k_k, k_a, r_k, w2p, a2p, g2p)


def _prev_rows(proj, shift0, tm, tiles_per_seq):
    n = proj.shape[0]
    tiles = n // tm
    last = proj[tm - 1::tm, :P_PAD][:tiles - 1]
    rows = jnp.concatenate([jnp.zeros((1, P_PAD), F32), last], axis=0).reshape(tiles // tiles_per_seq, tiles_per_seq, P_PAD)
    rows = rows.at[:, 0, :].set(shift0[:, 0, :])
    return rows.reshape(tiles, 1, P_PAD)


def _rwkv_scan_body(r_ref, lw_ref, kf_ref, v_ref, kn_ref, al_ref, g_ref, bonus_ref, lnw_ref, lnb_ref, h0_ref,
                    y_ref, hT_ref, h_s):
    c = CHUNK
    gc = RWKV_GROUP * c
    n = 2 * gc
    tt = r_ref.shape[0]
    n_inst = r_ref.shape[1] // LANE
    ti = pl.program_id(2)

    @pl.when(ti == 0)
    def _():
        h_s[...] = h0_ref[...]

    head0 = lax.broadcasted_iota(jnp.int32, (c, LANE), 1) < HEAD_DIM
    ri = lax.broadcasted_iota(jnp.int32, (n, n), 0)
    ci = lax.broadcasted_iota(jnp.int32, (n, n), 1)
    same_block = (ri // c) == (ci // c)
    strict = same_block & (ci < ri)
    incl = same_block & (ci <= ri)
    eye = ri == ci
    eye_l = lax.broadcasted_iota(jnp.int32, (LANE, LANE), 0) == lax.broadcasted_iota(jnp.int32, (LANE, LANE), 1)
    tr = lax.broadcasted_iota(jnp.int32, (gc, gc), 0)
    tc = lax.broadcasted_iota(jnp.int32, (gc, gc), 1)
    tri = jnp.where(((tr // c) == (tc // c)) & (tc <= tr), 1.0, 0.0).astype(BF16)
    gmat = _group_ones(LANE)

    def stack(x):
        parts = []
        for k in range(RWKV_GROUP):
            xk = x[k * c:(k + 1) * c, :]
            parts += [jnp.where(head0, xk, 0.0), jnp.where(head0, 0.0, xk)]
        return jnp.concatenate(parts, axis=0)

    def twice(x):
        parts = []
        for k in range(RWKV_GROUP):
            xk = x[k * c:(k + 1) * c, :]
            parts += [xk, xk]
        return jnp.concatenate(parts, axis=0)

    def solve(gi, j):
        rows = pl.ds(pl.multiple_of(gi * gc, gc), gc)
        ln = slice(j * LANE, (j + 1) * LANE)
        lw = lw_ref[rows, ln]
        r, kf, v, kn, al = (ref[rows, ln].astype(F32) for ref in (r_ref, kf_ref, v_ref, kn_ref, al_ref))
        gcum = _mm2r(tri, lw)
        g_end = jnp.concatenate([jnp.broadcast_to(gcum[(k + 1) * c - 1:(k + 1) * c, :], (c, LANE))
                                 for k in range(RWKV_GROUP)], axis=0)
        at = -kn * jnp.exp(gcum - lw)
        rt = r * jnp.exp(gcum)
        inv = jnp.exp(-gcum)
        bt = kn * al * inv
        kt = kf * inv
        dec_end = jnp.exp(g_end)
        at_st, rt_st, v_st = stack(at), stack(rt), stack(v)
        b2_st, k2_st = stack(bt * dec_end), stack(kt * dec_end)
        big = _mm(jnp.concatenate([at_st, rt_st], axis=0), jnp.concatenate([twice(bt), twice(kt)], axis=0), _NT)
        a_ab = jnp.where(strict, big[0:n, 0:n], 0.0)
        a_ak = jnp.where(strict, big[0:n, n:2 * n], 0.0)
        a_rb = jnp.where(incl, big[n:2 * n, 0:n], 0.0)
        a_rk = jnp.where(incl, big[n:2 * n, n:2 * n], 0.0)
        x = jnp.where(eye, 1.0, 0.0) + a_ab
        yield
        pw = _mm(a_ab, a_ab)
        yield
        steps = int(np.log2(c)) - 1
        for s in range(steps):
            if s < steps - 1:
                both = _mm(pw, jnp.concatenate([pw, x], axis=1))
                x = x + both[:, n:]
                pw = both[:, :n]
            else:
                x = x + _mm(pw, x)
            yield
        av_st = _mm(a_ak, v_st)
        yield
        wu_uv = _mm(x, jnp.concatenate([at_st, av_st], axis=1))
        wu_st, uv_st = wu_uv[:, :LANE], wu_uv[:, LANE:]
        yield
        rb = _mm(a_rb, wu_uv)
        r2_st = rt_st + rb[:, :LANE]
        yv_st = rb[:, LANE:] + _mm(a_rk, v_st)
        yield
        trans, add = [], []
        for k in range(RWKV_GROUP):
            blk = slice(2 * k * c, 2 * (k + 1) * c)
            trans.append(jnp.where(eye_l, dec_end[k * c:k * c + 1, :], 0.0) + _mm(b2_st[blk], wu_st[blk], _TN))
            add.append(_mm(jnp.concatenate([b2_st[blk], k2_st[blk]], axis=0),
                           jnp.concatenate([uv_st[blk], v_st[blk]], axis=0), _TN))
        yield
        hmat = h_s[j]
        ys = []
        for k in range(RWKV_GROUP):
            blk = slice(2 * k * c, 2 * (k + 1) * c)
            y_st = _mm(r2_st[blk], hmat) + yv_st[blk]
            hmat = _mm(trans[k], hmat) + add[k]
            ys.append(y_st[:c, :] + y_st[c:, :])
            yield
        h_s[j] = hmat
        y = ys[0] if len(ys) == 1 else jnp.concatenate(ys, axis=0)
        mean = _group_sum(y, gmat) * (1.0 / HEAD_DIM)
        dev = y - mean
        var = _group_sum(dev * dev, gmat) * (1.0 / HEAD_DIM)
        yn = dev * lax.rsqrt(var + GN_EPS) * lnw_ref[:, ln] + lnb_ref[:, ln]
        y_ref[rows, ln] = (yn + bonus_ref[rows, ln].astype(F32)) * g_ref[rows, ln].astype(F32)

    def group(gi, carry):
        _lockstep([solve(gi, j) for j in range(n_inst)])
        return carry

    lax.fori_loop(0, tt // gc, group, 0)

    @pl.when(ti == pl.num_programs(2) - 1)
    def _():
        hT_ref[...] = h_s[...]


def _rwkv_scan(vecs, lnw, lnb, h0, b, t):
    pairs = RWKV_WIDTH // LANE
    tt = min(t, 256)
    nt = t // tt
    ni = RWKV_INTERLEAVE
    seq = pl.BlockSpec((tt, ni * LANE), lambda bi, hp, ti: (bi * nt + ti, hp))
    vec = pl.BlockSpec((1, ni * LANE), lambda bi, hp, ti: (0, hp))
    st = pl.BlockSpec((None, ni, LANE, LANE), lambda bi, hp, ti: (bi, hp, 0, 0))
    return pl.pallas_call(
        _rwkv_scan_body,
        out_shape=(jax.ShapeDtypeStruct((b * t, RWKV_WIDTH), F32),
                   jax.ShapeDtypeStruct((b, pairs, LANE, LANE), F32)),
        grid=(b, pairs // ni, nt),
        in_specs=[seq] * 8 + [vec, vec, st],
        out_specs=(seq, st),
        scratch_shapes=[pltpu.VMEM((ni, LANE, LANE), F32)],
        compiler_params=_cparams(("arbitrary", "arbitrary", "arbitrary")),
        name="rwkv_scan",
    )(*vecs, lnw.reshape(1, RWKV_WIDTH), lnb.reshape(1, RWKV_WIDTH), h0)


def _rwkv_step_body(s_ref, lw_ref, kn_ref, al_ref, k_ref, r_ref, vgb_ref, lnw_ref, lnb_ref, y_ref, so_ref):
    s = s_ref[...]
    kn = kn_ref[...]
    sa = jnp.sum(s * (-kn), axis=-1, keepdims=True)
    vgb = vgb_ref[...]
    s_new = s * jnp.exp(lw_ref[...]) + sa * (kn * al_ref[...]) + vgb[:, :, 0:1] * k_ref[...]
    so_ref[...] = s_new
    y = jnp.sum(s_new * r_ref[...], axis=-1, keepdims=True)
    mean = jnp.mean(y, axis=1, keepdims=True)
    dev = y - mean
    var = jnp.mean(dev * dev, axis=1, keepdims=True)
    yn = dev * lax.rsqrt(var + GN_EPS) * lnw_ref[...] + lnb_ref[...]
    y_ref[...] = (yn + vgb[:, :, 2:3]) * vgb[:, :, 1:2]


def _rwkv_step(state, vecs, lnw, lnb):
    bs = state.shape[0]
    r, lw, kf, v, kn, al, g, bonus = vecs
    rowv = lambda a: a.reshape(bs, RWKV_HEADS, 1, HEAD_DIM)
    vgb = jnp.stack([v, g, bonus], axis=-1).reshape(bs, RWKV_HEADS, HEAD_DIM, 3)
    row_spec = pl.BlockSpec((None, RWKV_HEADS, 1, HEAD_DIM), lambda b: (b, 0, 0, 0))
    col_spec = pl.BlockSpec((None, RWKV_HEADS, HEAD_DIM, 1), lambda b: (b, 0, 0, 0))
    vgb_spec = pl.BlockSpec((None, RWKV_HEADS, HEAD_DIM, 3), lambda b: (b, 0, 0, 0))
    st_spec = pl.BlockSpec((None, RWKV_HEADS, HEAD_DIM, HEAD_DIM), lambda b: (b, 0, 0, 0))
    par_spec = pl.BlockSpec((RWKV_HEADS, HEAD_DIM, 1), lambda b: (0, 0, 0))
    y, s_new = pl.pallas_call(
        _rwkv_step_body,
        out_shape=(jax.ShapeDtypeStruct((bs, RWKV_HEADS, HEAD_DIM, 1), F32),
                   jax.ShapeDtypeStruct(state.shape, F32)),
        grid=(bs,),
        in_specs=[st_spec] + [row_spec] * 5 + [vgb_spec] + [par_spec] * 2,
        out_specs=(col_spec, st_spec),
        compiler_params=_cparams(("arbitrary",)),
        name="rwkv_step",
    )(state, rowv(lw), rowv(kn), rowv(al), rowv(kf), rowv(r), vgb,
      lnw.reshape(RWKV_HEADS, HEAD_DIM, 1), lnb.reshape(RWKV_HEADS, HEAD_DIM, 1))
    return y.reshape(bs, RWKV_WIDTH), s_new


def _outproj_body(a_ref, r_ref, x_ref, gt_ref, w_ref, g2_ref, sc_ref, sh_ref, x1_ref, h2_ref):
    y = _dot(_bf(a_ref[...]), w_ref[0:ATTN_WIDTH, :]) + _dot(_bf(r_ref[...]), w_ref[ATTN_WIDTH:, :])
    x1 = x_ref[...] + gt_ref[...] * y
    x1_ref[...] = x1
    h2_ref[...] = _modulated_norm(x1, g2_ref[...], sc_ref[...], sh_ref[...])


def _outproj(attn, rw, x2d, gt, w_bf, g2, sc, sh, tm, tiles_per_group):
    n = x2d.shape[0]
    r = gt.shape[1]
    half = pl.BlockSpec((tm, ATTN_WIDTH), lambda i: (i, 0))
    full = pl.BlockSpec((tm, D_MODEL), lambda i: (i, 0))
    mod = pl.BlockSpec((None, r, D_MODEL), lambda i: (i // tiles_per_group, 0, 0))
    return pl.pallas_call(
        _outproj_body,
        out_shape=(jax.ShapeDtypeStruct((n, D_MODEL), F32), jax.ShapeDtypeStruct((n, D_MODEL), F32)),
        grid=(n // tm,),
        in_specs=[half, half, full, mod, pl.BlockSpec((D_MODEL, D_MODEL), lambda i: (0, 0)),
                  pl.BlockSpec((1, D_MODEL), lambda i: (0, 0)), mod, mod],
        out_specs=(full, full),
        compiler_params=_cparams(("arbitrary",)),
        name="outproj",
    )(attn, rw, x2d, gt, w_bf, g2.reshape(1, D_MODEL), sc, sh)


def _staircase():
    pairs = [(a, b) for a in range(PEER_TOPK) for b in range(PEER_TOPK) if (a + 1) * (b + 1) <= PEER_TOPK]
    return pairs


_CAND = _staircase()
_CAND_ROWS = 56


_SENTINEL = 2.0 ** 100


def _extract_top(s, count, rows, tie_safe):
    vals = []
    limit = float(s.shape[0])
    for kth in range(count):
        mx = jnp.max(s, axis=0, keepdims=True)
        if tie_safe:
            hit = rows == jnp.min(jnp.where(s == mx, rows, limit), axis=0, keepdims=True)
        else:
            hit = s == mx
        s = jnp.where(hit, _rank_mark(kth), s)
        vals.append(mx)
        yield
    return vals, s


def _rank_mark(kth):
    return -_SENTINEL * (1.0 + kth / 16.0)


def _was_taken(s):
    return s < -0.5 * _SENTINEL


def _decode_rank(s):
    return jnp.where(_was_taken(s), (s * (-1.0 / _SENTINEL) - 1.0) * 16.0, 99.0)


def _router_body(h2_ref, wq_ref, keys_ref, oh_ref, h2t_ref, cnt_ref, e1_ref, rk_ref, e2_ref, q_s):
    h2 = h2_ref[...]
    tm = h2.shape[0]
    h2t_ref[...] = _bf(h2.T)
    q_s[...] = _dot(wq_ref[...], _bf(h2), _NT)
    rows = lax.broadcasted_iota(jnp.int32, (PEER_KEYS, LANE), 0).astype(F32)
    crow = lax.broadcasted_iota(jnp.int32, (_CAND_ROWS, LANE), 0).astype(F32)

    def route(h, tie_safe):
        scores = []
        for half in range(2):
            g = 2 * h + half
            q_t = q_s[pl.ds(pl.multiple_of(g * PEER_KEYS, PEER_KEYS), PEER_KEYS), :]
            scores.append(_mm3(keys_ref[g], q_t))
        wrong = jnp.zeros((1, LANE), F32)
        for sub in range(tm // LANE):
            sl = slice(sub * LANE, (sub + 1) * LANE)
            s0, s1 = scores[0][:, sl], scores[1][:, sl]
            (tops0, fin0), (tops1, fin1) = yield from _lockstep_gen(
                [_extract_top(s0, PEER_TOPK, rows, tie_safe), _extract_top(s1, PEER_TOPK, rows, tie_safe)])
            cand = [tops0[a] + tops1[b] for a, b in _CAND]
            cand += [jnp.full((1, LANE), NEG, F32)] * (_CAND_ROWS - len(cand))
            cand = jnp.concatenate(cand, axis=0)
            top = tops0[0] + tops1[0]
            taken = jnp.zeros((_CAND_ROWS, LANE), F32)
            zsum = jnp.zeros((1, LANE), F32)
            for _ in range(PEER_TOPK):
                mx = jnp.max(cand, axis=0, keepdims=True)
                if tie_safe:
                    hit = crow == jnp.min(jnp.where(cand == mx, crow, float(_CAND_ROWS)), axis=0, keepdims=True)
                else:
                    hit = cand == mx
                taken = jnp.where(hit, 1.0, taken)
                cand = jnp.where(hit, NEG, cand)
                zsum = zsum + jnp.exp(mx - top)
                yield
            per_rank = _dot(oh_ref[...], _bf(taken))
            cnt = jnp.full((PEER_KEYS, LANE), -0.5, F32)
            for kth in range(PEER_TOPK):
                cnt = jnp.where(fin0 == _rank_mark(kth), per_rank[kth:kth + 1, :] - 0.5, cnt)
            in0, in1 = _was_taken(fin0), _was_taken(fin1)
            cnt_ref[h, :, sl] = cnt
            rk_ref[h, :, sl] = _bf(_decode_rank(fin1))
            e1_ref[h, :, sl] = jnp.where(in0, jnp.exp(s0 - tops0[0]), 0.0)
            e2_ref[h, :, sl] = _bf(jnp.where(in1, jnp.exp(s1 - tops1[0]) / zsum, 0.0))
            for mask in (jnp.where(in0, 1.0, 0.0), jnp.where(in1, 1.0, 0.0), taken):
                wrong = wrong + jnp.abs(jnp.sum(mask, axis=0, keepdims=True) - float(PEER_TOPK))
        return wrong

    def heads(tie_safe):
        def body(hi, wrong):
            res = _lockstep([route(hi * ROUTER_INTERLEAVE + k, tie_safe) for k in range(ROUTER_INTERLEAVE)])
            return wrong + sum(res)
        return lax.fori_loop(0, PEER_HEADS // ROUTER_INTERLEAVE, body, jnp.zeros((1, LANE), F32))

    wrong = heads(False)

    @pl.when(jnp.max(wrong) > 0.0)
    def _():
        heads(True)


def _router(h2, wq_t_bf, keys, tm):
    n = h2.shape[0]
    onehot = np.zeros((PEER_TOPK, _CAND_ROWS), np.float32)
    for idx, (a, _) in enumerate(_CAND):
        onehot[a, idx] = 1.0
    tok = pl.BlockSpec((PEER_HEADS, PEER_KEYS, tm), lambda i: (0, 0, i))
    tok_shape = lambda dt: jax.ShapeDtypeStruct((PEER_HEADS, PEER_KEYS, n), dt)
    return pl.pallas_call(
        _router_body,
        out_shape=(jax.ShapeDtypeStruct((D_MODEL, n), BF16),
                   tok_shape(F32), tok_shape(F32), tok_shape(BF16), tok_shape(BF16)),
        grid=(n // tm,),
        in_specs=[pl.BlockSpec((tm, D_MODEL), lambda i: (i, 0)),
                  pl.BlockSpec((D_MODEL, D_MODEL), lambda i: (0, 0)),
                  pl.BlockSpec((2 * PEER_HEADS, PEER_KEYS, PEER_KEYS), lambda i: (0, 0, 0)),
                  pl.BlockSpec((PEER_TOPK, _CAND_ROWS), lambda i: (0, 0))],
        out_specs=(pl.BlockSpec((D_MODEL, tm), lambda i: (0, i)), tok, tok, tok, tok),
        scratch_shapes=[pltpu.VMEM((D_MODEL, tm), F32)],
        compiler_params=_cparams(("arbitrary",)),
        name="peer_router",
    )(h2, wq_t_bf, keys, jnp.asarray(onehot, BF16))


def _experts_body(h2t_ref, u_ref, v_ref, cnt_ref, e1_ref, rk_ref, e2_ref, x1_ref, gt_ref, y_ref, acc_s, p_s):
    j = pl.program_id(1)
    last = pl.num_programs(1) - 1
    eb = u_ref.shape[0]
    per = eb // PEER_KEYS
    cur = j % 2

    @pl.when(j == 0)
    def _():
        acc_s[...] = jnp.zeros(acc_s.shape, F32)
        p_s[1] = jnp.zeros(p_s.shape[1:], BF16)

    acc_s[...] += _dot(p_s[1 - cur], v_ref[...], _TN)
    blk = jnp.minimum(j, last - 1)
    h2t = h2t_ref[...]
    for ii in range(per):
        sub = slice(ii * PEER_KEYS, (ii + 1) * PEER_KEYS)
        i = blk * per + ii
        wt = None
        for h in range(PEER_HEADS):
            partners = _bf(cnt_ref[h, pl.ds(i, 1), :])
            e1 = _bf(e1_ref[h, pl.ds(i, 1), :])
            term = jnp.where(rk_ref[h] < partners, e2_ref[h], 0.0) * e1
            wt = term if wt is None else wt + term
        act_in = _dot(u_ref[sub, :], h2t)
        act = 0.5 * act_in * (1.0 + lax.erf(act_in * (2.0 ** -0.5)))
        p_s[cur, sub, :] = wt * _bf(act)

    @pl.when(j == last)
    def _():
        y_ref[...] = x1_ref[...] + gt_ref[...] * acc_s[...]


def _experts(h2t, u_bf, v_bf, route, x1, gt, tm, tiles_per_group):
    n = x1.shape[0]
    eb = 512
    nblk = PEER_EXPERTS // eb
    r = gt.shape[1]
    tok = pl.BlockSpec((PEER_HEADS, PEER_KEYS, tm), lambda i, j: (0, 0, i))
    return pl.pallas_call(
        _experts_body,
        out_shape=jax.ShapeDtypeStruct((n, D_MODEL), F32),
        grid=(n // tm, nblk + 1),
        in_specs=[pl.BlockSpec((D_MODEL, tm), lambda i, j: (0, i)),
                  pl.BlockSpec((eb, D_MODEL), lambda i, j: (jnp.minimum(j, nblk - 1), 0)),
                  pl.BlockSpec((eb, D_MODEL), lambda i, j: (jnp.maximum(j - 1, 0), 0)),
                  tok, tok, tok, tok,
                  pl.BlockSpec((tm, D_MODEL), lambda i, j: (i, 0)),
                  pl.BlockSpec((None, r, D_MODEL), lambda i, j: (i // tiles_per_group, 0, 0))],
        out_specs=pl.BlockSpec((tm, D_MODEL), lambda i, j: (i, 0)),
        scratch_shapes=[pltpu.VMEM((tm, D_MODEL), F32), pltpu.VMEM((2, eb, tm), BF16)],
        compiler_params=_cparams(("arbitrary", "arbitrary")),
        name="peer_experts",
    )(h2t, u_bf, v_bf, *route, x1, gt)


def _pad_rows(a, rows):
    return jnp.pad(a, ((0, rows - a.shape[0]),) + ((0, 0),) * (a.ndim - 1))


def _state_to_pairs(state):
    b = state.shape[0]
    h = jnp.swapaxes(state, -1, -2).reshape(b, RWKV_HEADS // 2, 2, HEAD_DIM, HEAD_DIM)
    z = jnp.zeros_like(h[:, :, 0])
    top = jnp.concatenate([h[:, :, 0], z], axis=-1)
    bot = jnp.concatenate([z, h[:, :, 1]], axis=-1)
    return jnp.concatenate([top, bot], axis=-2)


def _pairs_to_state(hp):
    b = hp.shape[0]
    h0 = hp[:, :, :HEAD_DIM, :HEAD_DIM]
    h1 = hp[:, :, HEAD_DIM:, HEAD_DIM:]
    h = jnp.stack([h0, h1], axis=2).reshape(b, RWKV_HEADS, HEAD_DIM, HEAD_DIM)
    return jnp.swapaxes(h, -1, -2)


def kernel(x_prompt, x_sample, cache_attn_k, cache_attn_v, state_rwkv, state_rwkv_shift, c_prompt, c_sample, w_ada, b_ada, norm1_g, norm2_g, w_in, q_gain, k_gain, rwkv_mu, rwkv_w0, rwkv_w2, rwkv_a0, rwkv_a2, rwkv_g2, rwkv_k_k, rwkv_k_a, rwkv_r_k, rwkv_lnx_w, rwkv_lnx_b, w_out, peer_w_q, peer_sub_keys, peer_u, peer_v):
    b, t, _ = x_prompt.shape
    bs = x_sample.shape[0]
    n = b * t
    heads = ATTN_WIDTH // HEAD_DIM
    win = min(CACHE_LEN, t)

    w_in_bf = jnp.concatenate([w_in[:, 3 * ATTN_WIDTH:], jnp.zeros((D_MODEL, P_PAD - RWKV_PROJ), F32),
                               w_in[:, :3 * ATTN_WIDTH]], axis=1).astype(BF16)
    w_out_bf = w_out.astype(BF16)
    wq_t_bf = peer_w_q.T.astype(BF16)
    keys = peer_sub_keys.reshape(2 * PEER_HEADS, PEER_KEYS, PEER_KEYS)
    u_bf = peer_u.astype(BF16)
    v_bf = peer_v.astype(BF16)
    lora = lambda w, off: jnp.zeros((LORA_PAD, RWKV_WIDTH), F32).at[off:off + w.shape[0]].set(w)
    row = lambda a: a.reshape(1, -1)
    rwkv_params = (row(jnp.pad(rwkv_mu, (0, P_PAD - RWKV_PROJ))), row(rwkv_w0), row(rwkv_a0), row(rwkv_k_k),
                   row(rwkv_k_a), row(rwkv_r_k), lora(rwkv_w2, 0), lora(rwkv_a2, DECAY_RANK),
                   lora(rwkv_g2, DECAY_RANK + A_RANK))

    c_rows = b + bs
    c_pad = -(-c_rows // 8) * 8
    mod = _adaln(_pad_rows(jnp.concatenate([c_prompt, c_sample], axis=0), c_pad), w_ada, b_ada)
    mod_p = [m.reshape(b, 1, D_MODEL) for m in jnp.split(mod[:b], 6, axis=-1)]
    mod_s = [m.reshape(1, bs, D_MODEL) for m in jnp.split(mod[b:c_rows], 6, axis=-1)]

    tm_p = min(1024, t)
    xp = x_prompt.reshape(n, D_MODEL)
    proj_p = _inproj(xp, norm1_g, mod_p[1], mod_p[0], w_in_bf, tm_p, t // tm_p)
    attn_p, k_cache, v_cache = _attn_prompt(proj_p.reshape(b, t, PROJ_PAD), q_gain, k_gain, win)
    tm_r = min(256, t)
    shift0 = jnp.zeros((b, 1, P_PAD), F32)
    vecs_p = _rwkv_prep(True, proj_p, shift0, rwkv_params, tm_r, t // tm_r)
    rw_p, h_fin = _rwkv_scan(vecs_p, rwkv_lnx_w, rwkv_lnx_b, jnp.zeros((b, RWKV_HEADS // 2, LANE, LANE), F32), b, t)
    tm_o = min(256, t)
    x1_p, h2_p = _outproj(attn_p.reshape(n, ATTN_WIDTH), rw_p, xp, mod_p[2], w_out_bf, norm2_g,
                          mod_p[4], mod_p[3], tm_o, t // tm_o)
    tm_q = min(256, t)
    tm_e = min(512, t)
    h2t_p, *route_p = _router(h2_p, wq_t_bf, keys, tm_q)
    y_p = _experts(h2t_p, u_bf, v_bf, route_p, x1_p, mod_p[5], tm_e, t // tm_e)

    xs = x_sample.reshape(bs, D_MODEL)
    proj_s = _inproj(xs, norm1_g, mod_s[1], mod_s[0], w_in_bf, bs, 1)
    q_s = proj_s[:, P_PAD:P_PAD + ATTN_WIDTH]
    k_s = proj_s[:, P_PAD + ATTN_WIDTH:P_PAD + 2 * ATTN_WIDTH]
    v_s = proj_s[:, P_PAD + 2 * ATTN_WIDTH:]
    attn_s, kn_s = _attn_sample(q_s, k_s, v_s, q_gain, k_gain, cache_attn_k, cache_attn_v)
    prev_s = jnp.pad(state_rwkv_shift.reshape(bs, RWKV_PROJ), ((0, 0), (0, P_PAD - RWKV_PROJ)))
    vecs_s = _rwkv_prep(False, proj_s, prev_s, rwkv_params, bs, 1)
    rw_s, state_s = _rwkv_step(state_rwkv, vecs_s, rwkv_lnx_w, rwkv_lnx_b)
    x1_s, h2_s = _outproj(attn_s, rw_s, xs, mod_s[2], w_out_bf, norm2_g, mod_s[4], mod_s[3], bs, 1)
    ns = -(-bs // LANE) * LANE
    h2t_s, *route_s = _router(_pad_rows(h2_s, ns), wq_t_bf, keys, LANE)
    gt2_s = _pad_rows(mod_s[5][0], ns).reshape(ns // LANE, LANE, D_MODEL)
    y_s = _experts(h2t_s, u_bf, v_bf, route_s, _pad_rows(x1_s, ns), gt2_s, LANE, 1)[:bs]

    return (y_p.reshape(b, t, D_MODEL), y_s.reshape(bs, 1, D_MODEL),
            k_cache.reshape(b, win, heads, HEAD_DIM), v_cache.reshape(b, win, heads, HEAD_DIM),
            _pairs_to_state(h_fin), proj_p.reshape(b, t, PROJ_PAD)[:, t - 1:, :RWKV_PROJ],
            kn_s.reshape(bs, 1, heads, HEAD_DIM), v_s.reshape(bs, 1, heads, HEAD_DIM),
            state_s, proj_s[:, :RWKV_PROJ].reshape(bs, 1, RWKV_PROJ))
```

```python
import functools

import numpy as np
import jax
import jax.numpy as jnp
from jax import lax
from jax.experimental import pallas as pl
from jax.experimental.pallas import tpu as pltpu

F32 = jnp.float32
BF16 = jnp.bfloat16

D_MODEL = 2048
HEAD_DIM = 64
ATTN_WIDTH = 1024
RWKV_WIDTH = 1024
RWKV_HEADS = RWKV_WIDTH // HEAD_DIM
DECAY_RANK = 64
A_RANK = 64
GATE_RANK = 160
RWKV_PROJ = 3 * RWKV_WIDTH + DECAY_RANK + A_RANK + GATE_RANK
P_PAD = 3584
PROJ_PAD = P_PAD + 3 * ATTN_WIDTH
LORA_PAD = 384
DIL_PATTERNS = ((128, 1), (512, 4), (2048, 16))
CACHE_LEN = 2048
NORM_EPS = 1e-6
GN_EPS = HEAD_DIM * 1e-5
PEER_HEADS = 8
PEER_KEYS = 128
PEER_TOPK = 16
PEER_EXPERTS = PEER_KEYS * PEER_KEYS
NEG = -1e30

LANE = 128
QBLK = 128
CHUNK = 64
RWKV_GROUP = 1
RWKV_INTERLEAVE = 8
ROUTER_INTERLEAVE = 2
ATTN_INTERLEAVE = 4
VMEM_LIMIT = 56 * 1024 * 1024


def _cparams(sem):
    return pltpu.CompilerParams(dimension_semantics=sem, vmem_limit_bytes=VMEM_LIMIT)


_NN = (((1,), (0,)), ((), ()))
_NT = (((1,), (1,)), ((), ()))
_TN = (((0,), (0,)), ((), ()))


def _bf(x):
    return x.astype(BF16)


def _dot(a, b, dims=_NN):
    return lax.dot_general(a, b, dims, preferred_element_type=F32)


def _mm(a, b, dims=_NN):
    return _dot(_bf(a), _bf(b), dims)


def _split(x):
    hi = _bf(x)
    lo = _bf(x - hi.astype(F32))
    return hi, lo


def _mm3(a, b, dims=_NN):
    ah, al = _split(a)
    bh, bl = _split(b)
    return _dot(ah, bh, dims) + (_dot(ah, bl, dims) + _dot(al, bh, dims))


def _mm2l(a, b_exact, dims=_NN):
    ah, al = _split(a)
    return _dot(ah, b_exact, dims) + _dot(al, b_exact, dims)


def _mm2r(a_exact, b, dims=_NN):
    bh, bl = _split(b)
    return _dot(a_exact, bh, dims) + _dot(a_exact, bl, dims)


def _group_ones(n):
    r = lax.broadcasted_iota(jnp.int32, (n, n), 0) // HEAD_DIM
    c = lax.broadcasted_iota(jnp.int32, (n, n), 1) // HEAD_DIM
    return jnp.where(r == c, 1.0, 0.0).astype(BF16)


def _group_sum(x, g):
    cols = x.shape[1] // LANE
    parts = [_mm2l(x[:, c * LANE:(c + 1) * LANE], g) for c in range(cols)]
    return parts[0] if cols == 1 else jnp.concatenate(parts, axis=1)


def _lockstep_gen(gens):
    results = [None] * len(gens)
    live = list(enumerate(gens))
    while live:
        nxt = []
        for idx, gen in live:
            try:
                next(gen)
                nxt.append((idx, gen))
            except StopIteration as stop:
                results[idx] = stop.value
        live = nxt
        if live:
            yield
    return results


def _lockstep(gens):
    runner = _lockstep_gen(gens)
    while True:
        try:
            next(runner)
        except StopIteration as stop:
            return stop.value


def _adaln_body(c_ref, w_ref, b_ref, o_ref):
    c = c_ref[...]
    s = c * (1.0 / (1.0 + jnp.exp(-c)))
    o_ref[...] = _mm3(s, w_ref[...]) + b_ref[...]


def _adaln(c, w_ada, b_ada):
    rows = c.shape[0]
    n = w_ada.shape[1]
    tn = 1024
    return pl.pallas_call(
        _adaln_body,
        out_shape=jax.ShapeDtypeStruct((rows, n), F32),
        grid=(n // tn,),
        in_specs=[pl.BlockSpec((rows, D_MODEL), lambda j: (0, 0)),
                  pl.BlockSpec((D_MODEL, tn), lambda j: (0, j)),
                  pl.BlockSpec((1, tn), lambda j: (0, j))],
        out_specs=pl.BlockSpec((rows, tn), lambda j: (0, j)),
        compiler_params=_cparams(("arbitrary",)),
        name="adaln",
    )(c, w_ada, b_ada.reshape(1, n))


def _modulated_norm(x, g, sc, sh):
    ms = jnp.mean(x * x, axis=-1, keepdims=True)
    return (x * lax.rsqrt(ms + NORM_EPS) * g) * (1.0 + sc) + sh


def _inproj_body(x_ref, g_ref, sc_ref, sh_ref, w_ref, o_ref, h_s):
    j = pl.program_id(1)

    @pl.when(j == 0)
    def _():
        h_s[...] = _bf(_modulated_norm(x_ref[...], g_ref[...], sc_ref[...], sh_ref[...]))

    tn = w_ref.shape[1]
    valid = jnp.where(j == _P_TILES - 1, RWKV_PROJ - (_P_TILES - 1) * tn, tn)
    cols = lax.broadcasted_iota(jnp.int32, w_ref.shape, 1)
    o_ref[...] = _dot(h_s[...], _bf(jnp.where(cols < valid, w_ref[...], 0.0)))


_IN_TN = 512
_P_TILES = P_PAD // _IN_TN


def _inproj(x2d, g, sc, sh, w_in, tm, tiles_per_group):
    n = x2d.shape[0]
    r = sc.shape[1]
    tn = _IN_TN
    qkv_tiles = 3 * ATTN_WIDTH // tn
    src = lambda j: jnp.where(j < _P_TILES, j + qkv_tiles, j - _P_TILES)
    mod_spec = pl.BlockSpec((None, r, D_MODEL), lambda i, j: (i // tiles_per_group, 0, 0))
    return pl.pallas_call(
        _inproj_body,
        out_shape=jax.ShapeDtypeStruct((n, PROJ_PAD), F32),
        grid=(n // tm, PROJ_PAD // tn),
        in_specs=[pl.BlockSpec((tm, D_MODEL), lambda i, j: (i, 0)),
                  pl.BlockSpec((1, D_MODEL), lambda i, j: (0, 0)),
                  mod_spec, mod_spec,
                  pl.BlockSpec((D_MODEL, tn), lambda i, j: (0, src(j)))],
        out_specs=pl.BlockSpec((tm, tn), lambda i, j: (i, j)),
        scratch_shapes=[pltpu.VMEM((tm, D_MODEL), BF16)],
        compiler_params=_cparams(("arbitrary", "arbitrary")),
        name="inproj",
    )(x2d, g.reshape(1, D_MODEL), sc, sh, w_in)


def _head_norm(x, gain, gmat):
    ss = _group_sum(x * x, gmat)
    return x * lax.rsqrt(ss * (1.0 / HEAD_DIM) + NORM_EPS) * gain


def _attn_prompt_body(q_ref, k_ref, v_ref, qg_ref, kg_ref, o_ref, kc_ref, vc_ref,
                      qn_s, kn_s, m_s, l_s, acc_s):
    t_len = q_ref.shape[0]
    win = kc_ref.shape[0]
    gmat = _group_ones(LANE)
    rows_per = 512

    def prologue(c, carry):
        rows = pl.ds(pl.multiple_of(c * rows_per, rows_per), rows_per)
        qn_s[rows, :] = _head_norm(q_ref[rows, :], qg_ref[...], gmat) * (HEAD_DIM ** -0.5)
        kn_s[rows, :] = _head_norm(k_ref[rows, :], kg_ref[...], gmat)
        m_s[rows, :] = jnp.full((rows_per, LANE), NEG, F32)
        l_s[rows, :] = jnp.zeros((rows_per, LANE), F32)
        acc_s[rows, :] = jnp.zeros((rows_per, LANE), F32)
        return carry

    lax.fori_loop(0, t_len // rows_per, prologue, 0)
    kc_ref[...] = kn_s[t_len - win:, :]
    vc_ref[...] = v_ref[t_len - win:, :]

    qi = lax.broadcasted_iota(jnp.int32, (QBLK, 2 * QBLK), 0)
    kj = lax.broadcasted_iota(jnp.int32, (QBLK, 2 * QBLK), 1)
    in_prev = kj < QBLK
    upper = qi + QBLK
    head0 = lax.broadcasted_iota(jnp.int32, (QBLK, LANE), 1) < HEAD_DIM

    for _, dil in DIL_PATTERNS:
        nblk = t_len // dil // QBLK

        def rows_of(start, dil=dil):
            if dil == 1:
                return pl.ds(start, QBLK)
            return pl.ds(start, QBLK, stride=dil)

        def unit(u, dil=dil, nblk=nblk, rows_of=rows_of):
            res = u // nblk
            blk = u - res * nblk
            cur = rows_of(blk * (QBLK * dil) + res)
            prev = rows_of(jnp.maximum(blk - 1, 0) * (QBLK * dil) + res)
            qb = qn_s[cur, :]
            kcat = _bf(jnp.concatenate([kn_s[prev, :], kn_s[cur, :]], axis=0))
            vcat = _bf(jnp.concatenate([v_ref[prev, :], v_ref[cur, :]], axis=0))
            m_old, l_old, acc_old = m_s[cur, :], l_s[cur, :], acc_s[cur, :]
            lower = jnp.where(in_prev, qi + jnp.where(blk > 0, 0, 4 * QBLK), 0)
            yield
            stats = []
            for sel in (head0, ~head0):
                qh = _bf(jnp.where(sel, qb, 0.0))
                s = jnp.where(kj >= lower, jnp.where(kj <= upper, _dot(qh, kcat, _NT), NEG), NEG)
                yield
                mb = jnp.max(s, axis=-1, keepdims=True)
                p = jnp.exp(s - mb)
                lb = jnp.sum(p, axis=-1, keepdims=True)
                yield
                stats.append((mb, lb, _dot(_bf(p), vcat)))
                yield
            mb = jnp.where(head0, stats[0][0], stats[1][0])
            lb = jnp.where(head0, stats[0][1], stats[1][1])
            ob = jnp.where(head0, stats[0][2], stats[1][2])
            m_new = jnp.maximum(m_old, mb)
            a_old = jnp.exp(m_old - m_new)
            a_blk = jnp.exp(mb - m_new)
            yield
            l_s[cur, :] = l_old * a_old + lb * a_blk
            acc_s[cur, :] = acc_old * a_old + ob * a_blk
            m_s[cur, :] = m_new

        def units(ui, carry, unit=unit):
            _lockstep([unit(ui * ATTN_INTERLEAVE + k) for k in range(ATTN_INTERLEAVE)])
            return carry

        lax.fori_loop(0, dil * nblk // ATTN_INTERLEAVE, units, 0)

    def epilogue(c, carry):
        rows = pl.ds(pl.multiple_of(c * rows_per, rows_per), rows_per)
        o_ref[rows, :] = acc_s[rows, :] / l_s[rows, :]
        return carry

    lax.fori_loop(0, t_len // rows_per, epilogue, 0)


def _attn_prompt(proj3, q_gain, k_gain, win):
    b, t, _ = proj3.shape
    pairs = ATTN_WIDTH // LANE
    qoff, koff, voff = P_PAD // LANE, (P_PAD + ATTN_WIDTH) // LANE, (P_PAD + 2 * ATTN_WIDTH) // LANE
    col = lambda off: pl.BlockSpec((None, t, LANE), lambda bi, hp: (bi, 0, off + hp))
    gain = lambda gvec: jnp.tile(gvec, 2).reshape(1, LANE)
    return pl.pallas_call(
        _attn_prompt_body,
        out_shape=(jax.ShapeDtypeStruct((b, t, ATTN_WIDTH), F32),
                   jax.ShapeDtypeStruct((b, win, ATTN_WIDTH), F32),
                   jax.ShapeDtypeStruct((b, win, ATTN_WIDTH), F32)),
        grid=(b, pairs),
        in_specs=[col(qoff), col(koff), col(voff),
                  pl.BlockSpec((1, LANE), lambda bi, hp: (0, 0)),
                  pl.BlockSpec((1, LANE), lambda bi, hp: (0, 0))],
        out_specs=(pl.BlockSpec((None, t, LANE), lambda bi, hp: (bi, 0, hp)),
                   pl.BlockSpec((None, win, LANE), lambda bi, hp: (bi, 0, hp)),
                   pl.BlockSpec((None, win, LANE), lambda bi, hp: (bi, 0, hp))),
        scratch_shapes=[pltpu.VMEM((t, LANE), F32) for _ in range(5)],
        compiler_params=_cparams(("arbitrary", "arbitrary")),
        name="attn_prompt",
    )(proj3, proj3, proj3, gain(q_gain), gain(k_gain))


def _attn_sample_body(qkv_ref, qg_ref, kg_ref, kt_ref, vt_ref, o_ref):
    def head_norm(x, gain):
        ms = jnp.mean(x * x, axis=1, keepdims=True)
        return x * lax.rsqrt(ms + NORM_EPS) * gain

    qkv = qkv_ref[...]
    qn = head_norm(qkv[:, :, 0:1], qg_ref[...]) * (HEAD_DIM ** -0.5)
    kn = head_norm(qkv[:, :, 1:2], kg_ref[...])
    o_ref[:, :, 1:2] = kn
    length = kt_ref.shape[-1]
    dist = length - lax.broadcasted_iota(jnp.int32, (1, 1, length), 2)
    cnt = jnp.zeros((1, 1, length), F32)
    for window, dil in DIL_PATTERNS:
        cnt = cnt + jnp.where(dist <= window, jnp.where((dist & (dil - 1)) == 0, 1.0, 0.0), 0.0)
    s = jnp.sum(kt_ref[...] * qn, axis=1, keepdims=True)
    s_self = jnp.sum(kn * qn, axis=1, keepdims=True)
    top = jnp.maximum(jnp.max(jnp.where(cnt > 0, s, NEG), axis=-1, keepdims=True), s_self)
    e = jnp.where(cnt > 0, jnp.exp(s - top), 0.0) * cnt
    e_self = float(len(DIL_PATTERNS)) * jnp.exp(s_self - top)
    den = jnp.sum(e, axis=-1, keepdims=True) + e_self
    num = jnp.sum(vt_ref[...] * e, axis=-1, keepdims=True) + e_self * qkv[:, :, 2:3]
    o_ref[:, :, 0:1] = num / den


def _attn_sample(q, k, v, q_gain, k_gain, cache_k, cache_v):
    bs = q.shape[0]
    length = cache_k.shape[1]
    assert length >= max(w for w, _ in DIL_PATTERNS) and all(d & (d - 1) == 0 for _, d in DIL_PATTERNS)
    heads = ATTN_WIDTH // HEAD_DIM
    hb = 8
    qkv = jnp.stack([q, k, v], axis=-1).reshape(bs, heads, HEAD_DIM, 3)
    time_minor = lambda c: jnp.transpose(c, (0, 2, 3, 1))
    in_spec = pl.BlockSpec((None, hb, HEAD_DIM, 3), lambda b, h: (b, h, 0, 0))
    out_spec = pl.BlockSpec((None, hb, HEAD_DIM, 2), lambda b, h: (b, h, 0, 0))
    gain_spec = pl.BlockSpec((HEAD_DIM, 1), lambda b, h: (0, 0))
    cache_spec = pl.BlockSpec((None, hb, HEAD_DIM, length), lambda b, h: (b, h, 0, 0))
    both = pl.pallas_call(
        _attn_sample_body,
        out_shape=jax.ShapeDtypeStruct((bs, heads, HEAD_DIM, 2), F32),
        grid=(bs, heads // hb),
        in_specs=[in_spec, gain_spec, gain_spec, cache_spec, cache_spec],
        out_specs=out_spec,
        compiler_params=_cparams(("arbitrary", "arbitrary")),
        name="attn_sample",
    )(qkv, q_gain.reshape(HEAD_DIM, 1), k_gain.reshape(HEAD_DIM, 1), time_minor(cache_k), time_minor(cache_v))
    return both[..., 0].reshape(bs, ATTN_WIDTH), both[..., 1].reshape(bs, ATTN_WIDTH)


def _rwkv_prep_body(seq_mode, p_ref, prev_ref, mu_ref, w0_ref, a0_ref, kk_ref, ka_ref, rk_ref,
                    w2_ref, a2_ref, g2_ref,
                    r_o, lw_o, kf_o, v_o, kn_o, al_o, g_o, bonus_o):
    p = p_ref[...]
    if seq_mode:
        row0 = lax.broadcasted_iota(jnp.int32, p.shape, 0) == 0
        p_prev = jnp.where(row0, prev_ref[...], pltpu.roll(p, 1, 0))
    else:
        p_prev = prev_ref[...]
    z = p + (p_prev - p) * mu_ref[...]
    r = z[:, 0:RWKV_WIDTH]
    k = z[:, RWKV_WIDTH:2 * RWKV_WIDTH]
    v = z[:, 2 * RWKV_WIDTH:3 * RWKV_WIDTH]
    tail = z[:, 3 * RWKV_WIDTH:3 * RWKV_WIDTH + LORA_PAD]
    dw = _mm3(jnp.tanh(tail), w2_ref[...])
    da = _mm3(tail, a2_ref[...])
    g = _mm3(1.0 / (1.0 + jnp.exp(-tail)), g2_ref[...])
    u = -(w0_ref[...] + dw)
    softplus = jnp.maximum(u, 0.0) + jnp.log1p(jnp.exp(-jnp.abs(u)))
    lw = -jnp.exp(-softplus - 0.5)
    alpha = 1.0 / (1.0 + jnp.exp(-(a0_ref[...] + da)))
    gmat = _group_ones(LANE)
    kk = k * kk_ref[...]
    nrm = jnp.sqrt(_group_sum(kk * kk, gmat))
    kn = kk / jnp.maximum(nrm, 1e-12)
    kf = k * (1.0 + (alpha - 1.0) * ka_ref[...])
    bonus = _group_sum(r * kf * rk_ref[...], gmat) * v
    lw_o[...] = lw
    for ref, val in ((r_o, r), (kf_o, kf), (v_o, v), (kn_o, kn), (al_o, alpha), (g_o, g), (bonus_o, bonus)):
        ref[...] = val.astype(ref.dtype)


def _rwkv_prep(seq_mode, proj, prev, params, tm, tiles_per_seq):
    n = proj.shape[0]
    mu, w0, a0, k_k, k_a, r_k, w2p, a2p, g2p = params
    p_spec = pl.BlockSpec((tm, P_PAD), lambda i: (i, 0))
    if seq_mode:
        prev_rows = _prev_rows(proj, prev, tm, tiles_per_seq)
        prev_spec = pl.BlockSpec((None, 1, P_PAD), lambda i: (i, 0, 0))
        prev_arg = prev_rows
    else:
        prev_spec = pl.BlockSpec((tm, P_PAD), lambda i: (i, 0))
        prev_arg = prev
    vec = lambda width: pl.BlockSpec((1, width), lambda i: (0, 0))
    mat = pl.BlockSpec((LORA_PAD, RWKV_WIDTH), lambda i: (0, 0))
    out_spec = pl.BlockSpec((tm, RWKV_WIDTH), lambda i: (i, 0))
    vec_dtype = BF16 if seq_mode else F32
    out = lambda dt: jax.ShapeDtypeStruct((n, RWKV_WIDTH), dt)
    return pl.pallas_call(
        functools.partial(_rwkv_prep_body, seq_mode),
        out_shape=(out(vec_dtype), out(F32)) + (out(vec_dtype),) * 6,
        grid=(n // tm,),
        in_specs=[p_spec, prev_spec, vec(P_PAD)] + [vec(RWKV_WIDTH)] * 5 + [mat] * 3,
        out_specs=(out_spec,) * 8,
        compiler_params=_cparams(("arbitrary",)),
        name="rwkv_prep_seq" if seq_mode else "rwkv_prep_row",
    )(proj, prev_arg, mu, w0, a0, k_k, k_a, r_k, w2p, a2p, g2p)


def _prev_rows(proj, shift0, tm, tiles_per_seq):
    n = proj.shape[0]
    tiles = n // tm
    last = proj[tm - 1::tm, :P_PAD][:tiles - 1]
    rows = jnp.concatenate([jnp.zeros((1, P_PAD), F32), last], axis=0).reshape(tiles // tiles_per_seq, tiles_per_seq, P_PAD)
    rows = rows.at[:, 0, :].set(shift0[:, 0, :])
    return rows.reshape(tiles, 1, P_PAD)


def _rwkv_scan_body(r_ref, lw_ref, kf_ref, v_ref, kn_ref, al_ref, g_ref, bonus_ref, lnw_ref, lnb_ref, h0_ref,
                    y_ref, hT_ref, h_s):
    c = CHUNK
    gc = RWKV_GROUP * c
    n = 2 * gc
    tt = r_ref.shape[0]
    n_inst = r_ref.shape[1] // LANE
    ti = pl.program_id(2)

    @pl.when(ti == 0)
    def _():
        h_s[...] = h0_ref[...]

    head0 = lax.broadcasted_iota(jnp.int32, (c, LANE), 1) < HEAD_DIM
    ri = lax.broadcasted_iota(jnp.int32, (n, n), 0)
    ci = lax.broadcasted_iota(jnp.int32, (n, n), 1)
    same_block = (ri // c) == (ci // c)
    strict = same_block & (ci < ri)
    incl = same_block & (ci <= ri)
    eye = ri == ci
    eye_l = lax.broadcasted_iota(jnp.int32, (LANE, LANE), 0) == lax.broadcasted_iota(jnp.int32, (LANE, LANE), 1)
    tr = lax.broadcasted_iota(jnp.int32, (gc, gc), 0)
    tc = lax.broadcasted_iota(jnp.int32, (gc, gc), 1)
    tri = jnp.where(((tr // c) == (tc // c)) & (tc <= tr), 1.0, 0.0).astype(BF16)
    gmat = _group_ones(LANE)

    def stack(x):
        parts = []
        for k in range(RWKV_GROUP):
            xk = x[k * c:(k + 1) * c, :]
            parts += [jnp.where(head0, xk, 0.0), jnp.where(head0, 0.0, xk)]
        return jnp.concatenate(parts, axis=0)

    def twice(x):
        parts = []
        for k in range(RWKV_GROUP):
            xk = x[k * c:(k + 1) * c, :]
            parts += [xk, xk]
        return jnp.concatenate(parts, axis=0)

    def solve(gi, j):
        rows = pl.ds(pl.multiple_of(gi * gc, gc), gc)
        ln = slice(j * LANE, (j + 1) * LANE)
        lw = lw_ref[rows, ln]
        r, kf, v, kn, al = (ref[rows, ln].astype(F32) for ref in (r_ref, kf_ref, v_ref, kn_ref, al_ref))
        gcum = _mm2r(tri, lw)
        g_end = jnp.concatenate([jnp.broadcast_to(gcum[(k + 1) * c - 1:(k + 1) * c, :], (c, LANE))
                                 for k in range(RWKV_GROUP)], axis=0)
        at = -kn * jnp.exp(gcum - lw)
        rt = r * jnp.exp(gcum)
        inv = jnp.exp(-gcum)
        bt = kn * al * inv
        kt = kf * inv
        dec_end = jnp.exp(g_end)
        at_st, rt_st, v_st = stack(at), stack(rt), stack(v)
        b2_st, k2_st = stack(bt * dec_end), stack(kt * dec_end)
        big = _mm(jnp.concatenate([at_st, rt_st], axis=0), jnp.concatenate([twice(bt), twice(kt)], axis=0), _NT)
        a_ab = jnp.where(strict, big[0:n, 0:n], 0.0)
        a_ak = jnp.where(strict, big[0:n, n:2 * n], 0.0)
        a_rb = jnp.where(incl, big[n:2 * n, 0:n], 0.0)
        a_rk = jnp.where(incl, big[n:2 * n, n:2 * n], 0.0)
        x = jnp.where(eye, 1.0, 0.0) + a_ab
        yield
        pw = _mm(a_ab, a_ab)
        yield
        steps = int(np.log2(c)) - 1
        for s in range(steps):
            if s < steps - 1:
                both = _mm(pw, jnp.concatenate([pw, x], axis=1))
                x = x + both[:, n:]
                pw = both[:, :n]
            else:
                x = x + _mm(pw, x)
            yield
        av_st = _mm(a_ak, v_st)
        yield
        wu_uv = _mm(x, jnp.concatenate([at_st, av_st], axis=1))
        wu_st, uv_st = wu_uv[:, :LANE], wu_uv[:, LANE:]
        yield
        rb = _mm(a_rb, wu_uv)
        r2_st = rt_st + rb[:, :LANE]
        yv_st = rb[:, LANE:] + _mm(a_rk, v_st)
        yield
        trans, add = [], []
        for k in range(RWKV_GROUP):
            blk = slice(2 * k * c, 2 * (k + 1) * c)
            trans.append(jnp.where(eye_l, dec_end[k * c:k * c + 1, :], 0.0) + _mm(b2_st[blk], wu_st[blk], _TN))
            add.append(_mm(jnp.concatenate([b2_st[blk], k2_st[blk]], axis=0),
                           jnp.concatenate([uv_st[blk], v_st[blk]], axis=0), _TN))
        yield
        hmat = h_s[j]
        ys = []
        for k in range(RWKV_GROUP):
            blk = slice(2 * k * c, 2 * (k + 1) * c)
            y_st = _mm(r2_st[blk], hmat) + yv_st[blk]
            hmat = _mm(trans[k], hmat) + add[k]
            ys.append(y_st[:c, :] + y_st[c:, :])
            yield
        h_s[j] = hmat
        y = ys[0] if len(ys) == 1 else jnp.concatenate(ys, axis=0)
        mean = _group_sum(y, gmat) * (1.0 / HEAD_DIM)
        dev = y - mean
        var = _group_sum(dev * dev, gmat) * (1.0 / HEAD_DIM)
        yn = dev * lax.rsqrt(var + GN_EPS) * lnw_ref[:, ln] + lnb_ref[:, ln]
        y_ref[rows, ln] = (yn + bonus_ref[rows, ln].astype(F32)) * g_ref[rows, ln].astype(F32)

    def group(gi, carry):
        _lockstep([solve(gi, j) for j in range(n_inst)])
        return carry

    lax.fori_loop(0, tt // gc, group, 0)

    @pl.when(ti == pl.num_programs(2) - 1)
    def _():
        hT_ref[...] = h_s[...]


def _rwkv_scan(vecs, lnw, lnb, h0, b, t):
    pairs = RWKV_WIDTH // LANE
    tt = min(t, 256)
    nt = t // tt
    ni = RWKV_INTERLEAVE
    seq = pl.BlockSpec((tt, ni * LANE), lambda bi, hp, ti: (bi * nt + ti, hp))
    vec = pl.BlockSpec((1, ni * LANE), lambda bi, hp, ti: (0, hp))
    st = pl.BlockSpec((None, ni, LANE, LANE), lambda bi, hp, ti: (bi, hp, 0, 0))
    return pl.pallas_call(
        _rwkv_scan_body,
        out_shape=(jax.ShapeDtypeStruct((b * t, RWKV_WIDTH), F32),
                   jax.ShapeDtypeStruct((b, pairs, LANE, LANE), F32)),
        grid=(b, pairs // ni, nt),
        in_specs=[seq] * 8 + [vec, vec, st],
        out_specs=(seq, st),
        scratch_shapes=[pltpu.VMEM((ni, LANE, LANE), F32)],
        compiler_params=_cparams(("arbitrary", "arbitrary", "arbitrary")),
        name="rwkv_scan",
    )(*vecs, lnw.reshape(1, RWKV_WIDTH), lnb.reshape(1, RWKV_WIDTH), h0)


def _rwkv_step_body(s_ref, lw_ref, kn_ref, al_ref, k_ref, r_ref, vgb_ref, lnw_ref, lnb_ref, y_ref, so_ref):
    s = s_ref[...]
    kn = kn_ref[...]
    sa = jnp.sum(s * (-kn), axis=-1, keepdims=True)
    vgb = vgb_ref[...]
    s_new = s * jnp.exp(lw_ref[...]) + sa * (kn * al_ref[...]) + vgb[:, :, 0:1] * k_ref[...]
    so_ref[...] = s_new
    y = jnp.sum(s_new * r_ref[...], axis=-1, keepdims=True)
    mean = jnp.mean(y, axis=1, keepdims=True)
    dev = y - mean
    var = jnp.mean(dev * dev, axis=1, keepdims=True)
    yn = dev * lax.rsqrt(var + GN_EPS) * lnw_ref[...] + lnb_ref[...]
    y_ref[...] = (yn + vgb[:, :, 2:3]) * vgb[:, :, 1:2]


def _rwkv_step(state, vecs, lnw, lnb):
    bs = state.shape[0]
    r, lw, kf, v, kn, al, g, bonus = vecs
    rowv = lambda a: a.reshape(bs, RWKV_HEADS, 1, HEAD_DIM)
    vgb = jnp.stack([v, g, bonus], axis=-1).reshape(bs, RWKV_HEADS, HEAD_DIM, 3)
    row_spec = pl.BlockSpec((None, RWKV_HEADS, 1, HEAD_DIM), lambda b: (b, 0, 0, 0))
    col_spec = pl.BlockSpec((None, RWKV_HEADS, HEAD_DIM, 1), lambda b: (b, 0, 0, 0))
    vgb_spec = pl.BlockSpec((None, RWKV_HEADS, HEAD_DIM, 3), lambda b: (b, 0, 0, 0))
    st_spec = pl.BlockSpec((None, RWKV_HEADS, HEAD_DIM, HEAD_DIM), lambda b: (b, 0, 0, 0))
    par_spec = pl.BlockSpec((RWKV_HEADS, HEAD_DIM, 1), lambda b: (0, 0, 0))
    y, s_new = pl.pallas_call(
        _rwkv_step_body,
        out_shape=(jax.ShapeDtypeStruct((bs, RWKV_HEADS, HEAD_DIM, 1), F32),
                   jax.ShapeDtypeStruct(state.shape, F32)),
        grid=(bs,),
        in_specs=[st_spec] + [row_spec] * 5 + [vgb_spec] + [par_spec] * 2,
        out_specs=(col_spec, st_spec),
        compiler_params=_cparams(("arbitrary",)),
        name="rwkv_step",
    )(state, rowv(lw), rowv(kn), rowv(al), rowv(kf), rowv(r), vgb,
      lnw.reshape(RWKV_HEADS, HEAD_DIM, 1), lnb.reshape(RWKV_HEADS, HEAD_DIM, 1))
    return y.reshape(bs, RWKV_WIDTH), s_new


def _outproj_body(a_ref, r_ref, x_ref, gt_ref, w_ref, g2_ref, sc_ref, sh_ref, x1_ref, h2_ref):
    y = _dot(_bf(a_ref[...]), w_ref[0:ATTN_WIDTH, :]) + _dot(_bf(r_ref[...]), w_ref[ATTN_WIDTH:, :])
    x1 = x_ref[...] + gt_ref[...] * y
    x1_ref[...] = x1
    h2_ref[...] = _modulated_norm(x1, g2_ref[...], sc_ref[...], sh_ref[...])


def _outproj(attn, rw, x2d, gt, w_bf, g2, sc, sh, tm, tiles_per_group):
    n = x2d.shape[0]
    r = gt.shape[1]
    half = pl.BlockSpec((tm, ATTN_WIDTH), lambda i: (i, 0))
    full = pl.BlockSpec((tm, D_MODEL), lambda i: (i, 0))
    mod = pl.BlockSpec((None, r, D_MODEL), lambda i: (i // tiles_per_group, 0, 0))
    return pl.pallas_call(
        _outproj_body,
        out_shape=(jax.ShapeDtypeStruct((n, D_MODEL), F32), jax.ShapeDtypeStruct((n, D_MODEL), F32)),
        grid=(n // tm,),
        in_specs=[half, half, full, mod, pl.BlockSpec((D_MODEL, D_MODEL), lambda i: (0, 0)),
                  pl.BlockSpec((1, D_MODEL), lambda i: (0, 0)), mod, mod],
        out_specs=(full, full),
        compiler_params=_cparams(("arbitrary",)),
        name="outproj",
    )(attn, rw, x2d, gt, w_bf, g2.reshape(1, D_MODEL), sc, sh)


def _staircase():
    pairs = [(a, b) for a in range(PEER_TOPK) for b in range(PEER_TOPK) if (a + 1) * (b + 1) <= PEER_TOPK]
    return pairs


_CAND = _staircase()
_CAND_ROWS = 56


_SENTINEL = 2.0 ** 100


def _extract_top(s, count, rows, tie_safe):
    vals = []
    limit = float(s.shape[0])
    for kth in range(count):
        mx = jnp.max(s, axis=0, keepdims=True)
        if tie_safe:
            hit = rows == jnp.min(jnp.where(s == mx, rows, limit), axis=0, keepdims=True)
        else:
            hit = s == mx
        s = jnp.where(hit, _rank_mark(kth), s)
        vals.append(mx)
        yield
    return vals, s


def _rank_mark(kth):
    return -_SENTINEL * (1.0 + kth / 16.0)


def _was_taken(s):
    return s < -0.5 * _SENTINEL


def _decode_rank(s):
    return jnp.where(_was_taken(s), (s * (-1.0 / _SENTINEL) - 1.0) * 16.0, 99.0)


def _router_body(h2_ref, wq_ref, keys_ref, oh_ref, h2t_ref, cnt_ref, e1_ref, rk_ref, e2_ref, q_s):
    h2 = h2_ref[...]
    tm = h2.shape[0]
    h2t_ref[...] = _bf(h2.T)
    q_s[...] = _dot(wq_ref[...], _bf(h2), _NT)
    rows = lax.broadcasted_iota(jnp.int32, (PEER_KEYS, LANE), 0).astype(F32)
    crow = lax.broadcasted_iota(jnp.int32, (_CAND_ROWS, LANE), 0).astype(F32)

    def route(h, tie_safe):
        scores = []
        for half in range(2):
            g = 2 * h + half
            q_t = q_s[pl.ds(pl.multiple_of(g * PEER_KEYS, PEER_KEYS), PEER_KEYS), :]
            scores.append(_mm3(keys_ref[g], q_t))
        wrong = jnp.zeros((1, LANE), F32)
        for sub in range(tm // LANE):
            sl = slice(sub * LANE, (sub + 1) * LANE)
            s0, s1 = scores[0][:, sl], scores[1][:, sl]
            (tops0, fin0), (tops1, fin1) = yield from _lockstep_gen(
                [_extract_top(s0, PEER_TOPK, rows, tie_safe), _extract_top(s1, PEER_TOPK, rows, tie_safe)])
            cand = [tops0[a] + tops1[b] for a, b in _CAND]
            cand += [jnp.full((1, LANE), NEG, F32)] * (_CAND_ROWS - len(cand))
            cand = jnp.concatenate(cand, axis=0)
            top = tops0[0] + tops1[0]
            taken = jnp.zeros((_CAND_ROWS, LANE), F32)
            zsum = jnp.zeros((1, LANE), F32)
            for _ in range(PEER_TOPK):
                mx = jnp.max(cand, axis=0, keepdims=True)
                if tie_safe:
                    hit = crow == jnp.min(jnp.where(cand == mx, crow, float(_CAND_ROWS)), axis=0, keepdims=True)
                else:
                    hit = cand == mx
                taken = jnp.where(hit, 1.0, taken)
                cand = jnp.where(hit, NEG, cand)
                zsum = zsum + jnp.exp(mx - top)
                yield
            per_rank = _dot(oh_ref[...], _bf(taken))
            cnt = jnp.full((PEER_KEYS, LANE), -0.5, F32)
            for kth in range(PEER_TOPK):
                cnt = jnp.where(fin0 == _rank_mark(kth), per_rank[kth:kth + 1, :] - 0.5, cnt)
            in0, in1 = _was_taken(fin0), _was_taken(fin1)
            cnt_ref[h, :, sl] = cnt
            rk_ref[h, :, sl] = _bf(_decode_rank(fin1))
            e1_ref[h, :, sl] = jnp.where(in0, jnp.exp(s0 - tops0[0]), 0.0)
            e2_ref[h, :, sl] = _bf(jnp.where(in1, jnp.exp(s1 - tops1[0]) / zsum, 0.0))
            for mask in (jnp.where(in0, 1.0, 0.0), jnp.where(in1, 1.0, 0.0), taken):
                wrong = wrong + jnp.abs(jnp.sum(mask, axis=0, keepdims=True) - float(PEER_TOPK))
        return wrong

    def heads(tie_safe):
        def body(hi, wrong):
            res = _lockstep([route(hi * ROUTER_INTERLEAVE + k, tie_safe) for k in range(ROUTER_INTERLEAVE)])
            return wrong + sum(res)
        return lax.fori_loop(0, PEER_HEADS // ROUTER_INTERLEAVE, body, jnp.zeros((1, LANE), F32))

    wrong = heads(False)

    @pl.when(jnp.max(wrong) > 0.0)
    def _():
        heads(True)


def _router(h2, wq_t_bf, keys, tm):
    n = h2.shape[0]
    onehot = np.zeros((PEER_TOPK, _CAND_ROWS), np.float32)
    for idx, (a, _) in enumerate(_CAND):
        onehot[a, idx] = 1.0
    tok = pl.BlockSpec((PEER_HEADS, PEER_KEYS, tm), lambda i: (0, 0, i))
    tok_shape = lambda dt: jax.ShapeDtypeStruct((PEER_HEADS, PEER_KEYS, n), dt)
    return pl.pallas_call(
        _router_body,
        out_shape=(jax.ShapeDtypeStruct((D_MODEL, n), BF16),
                   tok_shape(F32), tok_shape(F32), tok_shape(BF16), tok_shape(BF16)),
        grid=(n // tm,),
        in_specs=[pl.BlockSpec((tm, D_MODEL), lambda i: (i, 0)),
                  pl.BlockSpec((D_MODEL, D_MODEL), lambda i: (0, 0)),
                  pl.BlockSpec((2 * PEER_HEADS, PEER_KEYS, PEER_KEYS), lambda i: (0, 0, 0)),
                  pl.BlockSpec((PEER_TOPK, _CAND_ROWS), lambda i: (0, 0))],
        out_specs=(pl.BlockSpec((D_MODEL, tm), lambda i: (0, i)), tok, tok, tok, tok),
        scratch_shapes=[pltpu.VMEM((D_MODEL, tm), F32)],
        compiler_params=_cparams(("arbitrary",)),
        name="peer_router",
    )(h2, wq_t_bf, keys, jnp.asarray(onehot, BF16))


def _experts_body(h2t_ref, u_ref, v_ref, cnt_ref, e1_ref, rk_ref, e2_ref, x1_ref, gt_ref, y_ref, p_s):
    j = pl.program_id(1)
    last = pl.num_programs(1) - 1
    eb = u_ref.shape[0]
    per = eb // PEER_KEYS
    cur = j % 2

    @pl.when(j == 0)
    def _():
        y_ref[...] = jnp.zeros(y_ref.shape, F32)
        p_s[1] = jnp.zeros(p_s.shape[1:], BF16)

    @pl.when(j < last)
    def _():
        y_ref[...] += _dot(p_s[1 - cur], v_ref[...], _TN)
        h2t = h2t_ref[...]
        for ii in range(per):
            sub = slice(ii * PEER_KEYS, (ii + 1) * PEER_KEYS)
            i = j * per + ii
            wt = None
            for h in range(PEER_HEADS):
                partners = _bf(cnt_ref[h, pl.ds(i, 1), :])
                e1 = _bf(e1_ref[h, pl.ds(i, 1), :])
                term = jnp.where(rk_ref[h] < partners, e2_ref[h], 0.0) * e1
                wt = term if wt is None else wt + term
            act_in = _dot(u_ref[sub, :], h2t)
            act = 0.5 * act_in * (1.0 + lax.erf(act_in * (2.0 ** -0.5)))
            p_s[cur, sub, :] = wt * _bf(act)

    @pl.when(j == last)
    def _():
        acc = y_ref[...] + _dot(p_s[1 - cur], v_ref[...], _TN)
        y_ref[...] = x1_ref[...] + gt_ref[...] * acc


def _experts(h2t, u_bf, v_bf, route, x1, gt, tm, tiles_per_group):
    n = x1.shape[0]
    eb = 1024
    nblk = PEER_EXPERTS // eb
    r = gt.shape[1]
    tok = pl.BlockSpec((PEER_HEADS, PEER_KEYS, tm), lambda i, j: (0, 0, i))
    return pl.pallas_call(
        _experts_body,
        out_shape=jax.ShapeDtypeStruct((n, D_MODEL), F32),
        grid=(n // tm, nblk + 1),
        in_specs=[pl.BlockSpec((D_MODEL, tm), lambda i, j: (0, i)),
                  pl.BlockSpec((eb, D_MODEL), lambda i, j: (jnp.minimum(j, nblk - 1), 0)),
                  pl.BlockSpec((eb, D_MODEL), lambda i, j: (jnp.maximum(j - 1, 0), 0)),
                  tok, tok, tok, tok,
                  pl.BlockSpec((tm, D_MODEL), lambda i, j: (i, 0)),
                  pl.BlockSpec((None, r, D_MODEL), lambda i, j: (i // tiles_per_group, 0, 0))],
        out_specs=pl.BlockSpec((tm, D_MODEL), lambda i, j: (i, 0)),
        scratch_shapes=[pltpu.VMEM((2, eb, tm), BF16)],
        compiler_params=_cparams(("arbitrary", "arbitrary")),
        name="peer_experts",
    )(h2t, u_bf, v_bf, *route, x1, gt)


def _pad_rows(a, rows):
    return jnp.pad(a, ((0, rows - a.shape[0]),) + ((0, 0),) * (a.ndim - 1))


def _state_to_pairs(state):
    b = state.shape[0]
    h = jnp.swapaxes(state, -1, -2).reshape(b, RWKV_HEADS // 2, 2, HEAD_DIM, HEAD_DIM)
    z = jnp.zeros_like(h[:, :, 0])
    top = jnp.concatenate([h[:, :, 0], z], axis=-1)
    bot = jnp.concatenate([z, h[:, :, 1]], axis=-1)
    return jnp.concatenate([top, bot], axis=-2)


def _pairs_to_state(hp):
    b = hp.shape[0]
    h0 = hp[:, :, :HEAD_DIM, :HEAD_DIM]
    h1 = hp[:, :, HEAD_DIM:, HEAD_DIM:]
    h = jnp.stack([h0, h1], axis=2).reshape(b, RWKV_HEADS, HEAD_DIM, HEAD_DIM)
    return jnp.swapaxes(h, -1, -2)


def kernel(x_prompt, x_sample, cache_attn_k, cache_attn_v, state_rwkv, state_rwkv_shift, c_prompt, c_sample, w_ada, b_ada, norm1_g, norm2_g, w_in, q_gain, k_gain, rwkv_mu, rwkv_w0, rwkv_w2, rwkv_a0, rwkv_a2, rwkv_g2, rwkv_k_k, rwkv_k_a, rwkv_r_k, rwkv_lnx_w, rwkv_lnx_b, w_out, peer_w_q, peer_sub_keys, peer_u, peer_v):
    b, t, _ = x_prompt.shape
    bs = x_sample.shape[0]
    n = b * t
    heads = ATTN_WIDTH // HEAD_DIM
    win = min(CACHE_LEN, t)

    w_out_bf = w_out.astype(BF16)
    wq_t_bf = peer_w_q.T.astype(BF16)
    keys = peer_sub_keys.reshape(2 * PEER_HEADS, PEER_KEYS, PEER_KEYS)
    u_bf = peer_u.astype(BF16)
    v_bf = peer_v.astype(BF16)
    lora = lambda w, off: jnp.zeros((LORA_PAD, RWKV_WIDTH), F32).at[off:off + w.shape[0]].set(w)
    row = lambda a: a.reshape(1, -1)
    rwkv_params = (row(jnp.pad(rwkv_mu, (0, P_PAD - RWKV_PROJ))), row(rwkv_w0), row(rwkv_a0), row(rwkv_k_k),
                   row(rwkv_k_a), row(rwkv_r_k), lora(rwkv_w2, 0), lora(rwkv_a2, DECAY_RANK),
                   lora(rwkv_g2, DECAY_RANK + A_RANK))

    c_rows = b + bs
    c_pad = -(-c_rows // 8) * 8
    mod = _adaln(_pad_rows(jnp.concatenate([c_prompt, c_sample], axis=0), c_pad), w_ada, b_ada)
    mod_p = [m.reshape(b, 1, D_MODEL) for m in jnp.split(mod[:b], 6, axis=-1)]
    mod_s = [m.reshape(1, bs, D_MODEL) for m in jnp.split(mod[b:c_rows], 6, axis=-1)]

    tm_p = min(1024, t)
    xp = x_prompt.reshape(n, D_MODEL)
    proj_p = _inproj(xp, norm1_g, mod_p[1], mod_p[0], w_in, tm_p, t // tm_p)
    attn_p, k_cache, v_cache = _attn_prompt(proj_p.reshape(b, t, PROJ_PAD), q_gain, k_gain, win)
    tm_r = min(256, t)
    shift0 = jnp.zeros((b, 1, P_PAD), F32)
    vecs_p = _rwkv_prep(True, proj_p, shift0, rwkv_params, tm_r, t // tm_r)
    rw_p, h_fin = _rwkv_scan(vecs_p, rwkv_lnx_w, rwkv_lnx_b, jnp.zeros((b, RWKV_HEADS // 2, LANE, LANE), F32), b, t)
    tm_o = min(256, t)
    x1_p, h2_p = _outproj(attn_p.reshape(n, ATTN_WIDTH), rw_p, xp, mod_p[2], w_out_bf, norm2_g,
                          mod_p[4], mod_p[3], tm_o, t // tm_o)
    tm_q = min(256, t)
    tm_e = min(512, t)
    h2t_p, *route_p = _router(h2_p, wq_t_bf, keys, tm_q)
    y_p = _experts(h2t_p, u_bf, v_bf, route_p, x1_p, mod_p[5], tm_e, t // tm_e)

    xs = x_sample.reshape(bs, D_MODEL)
    proj_s = _inproj(xs, norm1_g, mod_s[1], mod_s[0], w_in, bs, 1)
    q_s = proj_s[:, P_PAD:P_PAD + ATTN_WIDTH]
    k_s = proj_s[:, P_PAD + ATTN_WIDTH:P_PAD + 2 * ATTN_WIDTH]
    v_s = proj_s[:, P_PAD + 2 * ATTN_WIDTH:]
    attn_s, kn_s = _attn_sample(q_s, k_s, v_s, q_gain, k_gain, cache_attn_k, cache_attn_v)
    prev_s = jnp.pad(state_rwkv_shift.reshape(bs, RWKV_PROJ), ((0, 0), (0, P_PAD - RWKV_PROJ)))
    vecs_s = _rwkv_prep(False, proj_s, prev_s, rwkv_params, bs, 1)
    rw_s, state_s = _rwkv_step(state_rwkv, vecs_s, rwkv_lnx_w, rwkv_lnx_b)
    x1_s, h2_s = _outproj(attn_s, rw_s, xs, mod_s[2], w_out_bf, norm2_g, mod_s[4], mod_s[3], bs, 1)
    ns = -(-bs // LANE) * LANE
    h2t_s, *route_s = _router(_pad_rows(h2_s, ns), wq_t_bf, keys, LANE)
    gt2_s = _pad_rows(mod_s[5][0], ns).reshape(ns // LANE, LANE, D_MODEL)
    y_s = _experts(h2t_s, u_bf, v_bf, route_s, _pad_rows(x1_s, ns), gt2_s, LANE, 1)[:bs]

    return (y_p.reshape(b, t, D_MODEL), y_s.reshape(bs, 1, D_MODEL),
            k_cache.reshape(b, win, heads, HEAD_DIM), v_cache.reshape(b, win, heads, HEAD_DIM),
            _pairs_to_state(h_fin), proj_p.reshape(b, t, PROJ_PAD)[:, t - 1:, :RWKV_PROJ],
            kn_s.reshape(bs, 1, heads, HEAD_DIM), v_s.reshape(bs, 1, heads, HEAD_DIM),
            state_s, proj_s[:, :RWKV_PROJ].reshape(bs, 1, RWKV_PROJ))
```

```python
import functools

import numpy as np
import jax
import jax.numpy as jnp
from jax import lax
from jax.experimental import pallas as pl
from jax.experimental.pallas import tpu as pltpu

F32 = jnp.float32
BF16 = jnp.bfloat16

D_MODEL = 2048
HEAD_DIM = 64
ATTN_WIDTH = 1024
RWKV_WIDTH = 1024
RWKV_HEADS = RWKV_WIDTH // HEAD_DIM
DECAY_RANK = 64
A_RANK = 64
GATE_RANK = 160
RWKV_PROJ = 3 * RWKV_WIDTH + DECAY_RANK + A_RANK + GATE_RANK
P_PAD = 3584
PROJ_PAD = P_PAD + 3 * ATTN_WIDTH
LORA_PAD = 384
DIL_PATTERNS = ((128, 1), (512, 4), (2048, 16))
CACHE_LEN = 2048
NORM_EPS = 1e-6
GN_EPS = HEAD_DIM * 1e-5
PEER_HEADS = 8
PEER_KEYS = 128
PEER_TOPK = 16
PEER_EXPERTS = PEER_KEYS * PEER_KEYS
NEG = -1e30

LANE = 128
QBLK = 128
CHUNK = 64
RWKV_GROUP = 1
RWKV_INTERLEAVE = 8
ROUTER_INTERLEAVE = 2
ATTN_INTERLEAVE = 8
VMEM_LIMIT = 56 * 1024 * 1024

TM_INPROJ = 1024
TM_PREP = 256
TM_OUTPROJ = 256
TM_ROUTE = 256
TM_EXPERT = 512
EXPERT_BLOCK = 1024
SCAN_ROWS = 256
ADALN_COLS = 1024
NORM_ROWS = 512
SAMPLE_HEADS = 8


def _cparams(sem):
    return pltpu.CompilerParams(dimension_semantics=sem, vmem_limit_bytes=VMEM_LIMIT)


_NN = (((1,), (0,)), ((), ()))
_NT = (((1,), (1,)), ((), ()))
_TN = (((0,), (0,)), ((), ()))


def _bf(x):
    return x.astype(BF16)


def _dot(a, b, dims=_NN):
    return lax.dot_general(a, b, dims, preferred_element_type=F32)


def _mm(a, b, dims=_NN):
    return _dot(_bf(a), _bf(b), dims)


def _split(x):
    hi = _bf(x)
    lo = _bf(x - hi.astype(F32))
    return hi, lo


def _mm3(a, b, dims=_NN):
    ah, al = _split(a)
    bh, bl = _split(b)
    return _dot(ah, bh, dims) + (_dot(ah, bl, dims) + _dot(al, bh, dims))


def _mm2l(a, b_exact, dims=_NN):
    ah, al = _split(a)
    return _dot(ah, b_exact, dims) + _dot(al, b_exact, dims)


def _mm2r(a_exact, b, dims=_NN):
    bh, bl = _split(b)
    return _dot(a_exact, bh, dims) + _dot(a_exact, bl, dims)


def _group_ones(n):
    r = lax.broadcasted_iota(jnp.int32, (n, n), 0) // HEAD_DIM
    c = lax.broadcasted_iota(jnp.int32, (n, n), 1) // HEAD_DIM
    return jnp.where(r == c, 1.0, 0.0).astype(BF16)


def _group_sum(x, g):
    cols = x.shape[1] // LANE
    parts = [_mm2l(x[:, c * LANE:(c + 1) * LANE], g) for c in range(cols)]
    return parts[0] if cols == 1 else jnp.concatenate(parts, axis=1)


def _lockstep_gen(gens):
    results = [None] * len(gens)
    live = list(enumerate(gens))
    while live:
        nxt = []
        for idx, gen in live:
            try:
                next(gen)
                nxt.append((idx, gen))
            except StopIteration as stop:
                results[idx] = stop.value
        live = nxt
        if live:
            yield
    return results


def _lockstep(gens):
    runner = _lockstep_gen(gens)
    while True:
        try:
            next(runner)
        except StopIteration as stop:
            return stop.value


def _adaln_body(c_ref, w_ref, b_ref, o_ref):
    c = c_ref[...]
    s = c * (1.0 / (1.0 + jnp.exp(-c)))
    o_ref[...] = _mm3(s, w_ref[...]) + b_ref[...]


def _adaln(c, w_ada, b_ada):
    rows = c.shape[0]
    n = w_ada.shape[1]
    tn = ADALN_COLS
    return pl.pallas_call(
        _adaln_body,
        out_shape=jax.ShapeDtypeStruct((rows, n), F32),
        grid=(n // tn,),
        in_specs=[pl.BlockSpec((rows, D_MODEL), lambda j: (0, 0)),
                  pl.BlockSpec((D_MODEL, tn), lambda j: (0, j)),
                  pl.BlockSpec((1, tn), lambda j: (0, j))],
        out_specs=pl.BlockSpec((rows, tn), lambda j: (0, j)),
        compiler_params=_cparams(("arbitrary",)),
        name="adaln",
    )(c, w_ada, b_ada.reshape(1, n))


def _modulated_norm(x, g, sc, sh):
    ms = jnp.mean(x * x, axis=-1, keepdims=True)
    return (x * lax.rsqrt(ms + NORM_EPS) * g) * (1.0 + sc) + sh


def _inproj_body(x_ref, g_ref, sc_ref, sh_ref, w_ref, o_ref, h_s):
    j = pl.program_id(1)

    @pl.when(j == 0)
    def _():
        h_s[...] = _bf(_modulated_norm(x_ref[...], g_ref[...], sc_ref[...], sh_ref[...]))

    tn = w_ref.shape[1]
    valid = jnp.where(j == _P_TILES - 1, RWKV_PROJ - (_P_TILES - 1) * tn, tn)
    cols = lax.broadcasted_iota(jnp.int32, w_ref.shape, 1)
    o_ref[...] = _dot(h_s[...], _bf(jnp.where(cols < valid, w_ref[...], 0.0)))


_IN_TN = 512
_P_TILES = P_PAD // _IN_TN


def _inproj(x2d, g, sc, sh, w_in, tm, tiles_per_group):
    n = x2d.shape[0]
    r = sc.shape[1]
    tn = _IN_TN
    qkv_tiles = 3 * ATTN_WIDTH // tn
    src = lambda j: jnp.where(j < _P_TILES, j + qkv_tiles, j - _P_TILES)
    mod_spec = pl.BlockSpec((None, r, D_MODEL), lambda i, j: (i // tiles_per_group, 0, 0))
    return pl.pallas_call(
        _inproj_body,
        out_shape=jax.ShapeDtypeStruct((n, PROJ_PAD), F32),
        grid=(n // tm, PROJ_PAD // tn),
        in_specs=[pl.BlockSpec((tm, D_MODEL), lambda i, j: (i, 0)),
                  pl.BlockSpec((1, D_MODEL), lambda i, j: (0, 0)),
                  mod_spec, mod_spec,
                  pl.BlockSpec((D_MODEL, tn), lambda i, j: (0, src(j)))],
        out_specs=pl.BlockSpec((tm, tn), lambda i, j: (i, j)),
        scratch_shapes=[pltpu.VMEM((tm, D_MODEL), BF16)],
        compiler_params=_cparams(("arbitrary", "arbitrary")),
        name="inproj",
    )(x2d, g.reshape(1, D_MODEL), sc, sh, w_in)


def _head_norm(x, gain, gmat):
    ss = _group_sum(x * x, gmat)
    return x * lax.rsqrt(ss * (1.0 / HEAD_DIM) + NORM_EPS) * gain


def _attn_prompt_body(q_ref, k_ref, v_ref, qg_ref, kg_ref, o_ref, kc_ref, vc_ref,
                      qn_s, kn_s, m_s, l_s, acc_s):
    t_len = q_ref.shape[0]
    win = kc_ref.shape[0]
    gmat = _group_ones(LANE)
    rows_per = NORM_ROWS

    def prologue(c, carry):
        rows = pl.ds(pl.multiple_of(c * rows_per, rows_per), rows_per)
        qn_s[rows, :] = _head_norm(q_ref[rows, :], qg_ref[...], gmat) * (HEAD_DIM ** -0.5)
        kn_s[rows, :] = _head_norm(k_ref[rows, :], kg_ref[...], gmat)
        m_s[rows, :] = jnp.full((rows_per, LANE), NEG, F32)
        l_s[rows, :] = jnp.zeros((rows_per, LANE), F32)
        acc_s[rows, :] = jnp.zeros((rows_per, LANE), F32)
        return carry

    lax.fori_loop(0, t_len // rows_per, prologue, 0)
    kc_ref[...] = kn_s[t_len - win:, :]
    vc_ref[...] = v_ref[t_len - win:, :]

    qi = lax.broadcasted_iota(jnp.int32, (QBLK, 2 * QBLK), 0)
    kj = lax.broadcasted_iota(jnp.int32, (QBLK, 2 * QBLK), 1)
    causal = jnp.where(kj - QBLK <= qi, 0.0, NEG)
    bias_both = jnp.where(kj < QBLK, jnp.where(kj >= qi, 0.0, NEG), causal)
    bias_first = jnp.where(kj < QBLK, NEG, causal)
    head0 = lax.broadcasted_iota(jnp.int32, (QBLK, LANE), 1) < HEAD_DIM

    for _, dil in DIL_PATTERNS:
        nblk = t_len // dil // QBLK

        def rows_of(start, dil=dil):
            if dil == 1:
                return pl.ds(start, QBLK)
            return pl.ds(start, QBLK, stride=dil)

        def unit(u, dil=dil, nblk=nblk, rows_of=rows_of):
            res = u // nblk
            blk = u - res * nblk
            cur = rows_of(blk * (QBLK * dil) + res)
            prev = rows_of(jnp.maximum(blk - 1, 0) * (QBLK * dil) + res)
            qb = qn_s[cur, :]
            kcat = _bf(jnp.concatenate([kn_s[prev, :], kn_s[cur, :]], axis=0))
            vcat = _bf(jnp.concatenate([v_ref[prev, :], v_ref[cur, :]], axis=0))
            m_old, l_old, acc_old = m_s[cur, :], l_s[cur, :], acc_s[cur, :]
            bias = jnp.where(blk > 0, bias_both, bias_first)
            yield
            stats = []
            for sel in (head0, ~head0):
                qh = _bf(jnp.where(sel, qb, 0.0))
                s = _dot(qh, kcat, _NT) + bias
                yield
                mb = jnp.max(s, axis=-1, keepdims=True)
                p = jnp.exp(s - mb)
                lb = jnp.sum(p, axis=-1, keepdims=True)
                yield
                stats.append((mb, lb, _dot(_bf(p), vcat)))
                yield
            mb = jnp.where(head0, stats[0][0], stats[1][0])
            lb = jnp.where(head0, stats[0][1], stats[1][1])
            ob = jnp.where(head0, stats[0][2], stats[1][2])
            m_new = jnp.maximum(m_old, mb)
            a_old = jnp.exp(m_old - m_new)
            a_blk = jnp.exp(mb - m_new)
            yield
            l_s[cur, :] = l_old * a_old + lb * a_blk
            acc_s[cur, :] = acc_old * a_old + ob * a_blk
            m_s[cur, :] = m_new

        def units(ui, carry, unit=unit):
            _lockstep([unit(ui * ATTN_INTERLEAVE + k) for k in range(ATTN_INTERLEAVE)])
            return carry

        lax.fori_loop(0, dil * nblk // ATTN_INTERLEAVE, units, 0)

    def epilogue(c, carry):
        rows = pl.ds(pl.multiple_of(c * rows_per, rows_per), rows_per)
        o_ref[rows, :] = acc_s[rows, :] / l_s[rows, :]
        return carry

    lax.fori_loop(0, t_len // rows_per, epilogue, 0)


def _attn_prompt(proj3, q_gain, k_gain, win):
    b, t, _ = proj3.shape
    pairs = ATTN_WIDTH // LANE
    qoff, koff, voff = P_PAD // LANE, (P_PAD + ATTN_WIDTH) // LANE, (P_PAD + 2 * ATTN_WIDTH) // LANE
    col = lambda off: pl.BlockSpec((None, t, LANE), lambda bi, hp: (bi, 0, off + hp))
    gain = lambda gvec: jnp.tile(gvec, 2).reshape(1, LANE)
    return pl.pallas_call(
        _attn_prompt_body,
        out_shape=(jax.ShapeDtypeStruct((b, t, ATTN_WIDTH), F32),
                   jax.ShapeDtypeStruct((b, win, ATTN_WIDTH), F32),
                   jax.ShapeDtypeStruct((b, win, ATTN_WIDTH), F32)),
        grid=(b, pairs),
        in_specs=[col(qoff), col(koff), col(voff),
                  pl.BlockSpec((1, LANE), lambda bi, hp: (0, 0)),
                  pl.BlockSpec((1, LANE), lambda bi, hp: (0, 0))],
        out_specs=(pl.BlockSpec((None, t, LANE), lambda bi, hp: (bi, 0, hp)),
                   pl.BlockSpec((None, win, LANE), lambda bi, hp: (bi, 0, hp)),
                   pl.BlockSpec((None, win, LANE), lambda bi, hp: (bi, 0, hp))),
        scratch_shapes=[pltpu.VMEM((t, LANE), F32) for _ in range(5)],
        compiler_params=_cparams(("arbitrary", "arbitrary")),
        name="attn_prompt",
    )(proj3, proj3, proj3, gain(q_gain), gain(k_gain))


def _attn_sample_body(qkv_ref, qg_ref, kg_ref, kt_ref, vt_ref, o_ref):
    def head_norm(x, gain):
        ms = jnp.mean(x * x, axis=1, keepdims=True)
        return x * lax.rsqrt(ms + NORM_EPS) * gain

    qkv = qkv_ref[...]
    qn = head_norm(qkv[:, :, 0:1], qg_ref[...]) * (HEAD_DIM ** -0.5)
    kn = head_norm(qkv[:, :, 1:2], kg_ref[...])
    o_ref[:, :, 1:2] = kn
    length = kt_ref.shape[-1]
    dist = length - lax.broadcasted_iota(jnp.int32, (1, 1, length), 2)
    cnt = jnp.zeros((1, 1, length), F32)
    for window, dil in DIL_PATTERNS:
        cnt = cnt + jnp.where(dist <= window, jnp.where((dist & (dil - 1)) == 0, 1.0, 0.0), 0.0)
    s = jnp.sum(kt_ref[...] * qn, axis=1, keepdims=True)
    s_self = jnp.sum(kn * qn, axis=1, keepdims=True)
    top = jnp.maximum(jnp.max(jnp.where(cnt > 0, s, NEG), axis=-1, keepdims=True), s_self)
    e = jnp.where(cnt > 0, jnp.exp(s - top), 0.0) * cnt
    e_self = float(len(DIL_PATTERNS)) * jnp.exp(s_self - top)
    den = jnp.sum(e, axis=-1, keepdims=True) + e_self
    num = jnp.sum(vt_ref[...] * e, axis=-1, keepdims=True) + e_self * qkv[:, :, 2:3]
    o_ref[:, :, 0:1] = num / den


def _attn_sample(q, k, v, q_gain, k_gain, cache_k, cache_v):
    bs = q.shape[0]
    length = cache_k.shape[1]
    assert length >= max(w for w, _ in DIL_PATTERNS) and all(d & (d - 1) == 0 for _, d in DIL_PATTERNS)
    heads = ATTN_WIDTH // HEAD_DIM
    hb = SAMPLE_HEADS
    qkv = jnp.stack([q, k, v], axis=-1).reshape(bs, heads, HEAD_DIM, 3)
    time_minor = lambda c: jnp.transpose(c, (0, 2, 3, 1))
    in_spec = pl.BlockSpec((None, hb, HEAD_DIM, 3), lambda b, h: (b, h, 0, 0))
    out_spec = pl.BlockSpec((None, hb, HEAD_DIM, 2), lambda b, h: (b, h, 0, 0))
    gain_spec = pl.BlockSpec((HEAD_DIM, 1), lambda b, h: (0, 0))
    cache_spec = pl.BlockSpec((None, hb, HEAD_DIM, length), lambda b, h: (b, h, 0, 0))
    both = pl.pallas_call(
        _attn_sample_body,
        out_shape=jax.ShapeDtypeStruct((bs, heads, HEAD_DIM, 2), F32),
        grid=(bs, heads // hb),
        in_specs=[in_spec, gain_spec, gain_spec, cache_spec, cache_spec],
        out_specs=out_spec,
        compiler_params=_cparams(("arbitrary", "arbitrary")),
        name="attn_sample",
    )(qkv, q_gain.reshape(HEAD_DIM, 1), k_gain.reshape(HEAD_DIM, 1), time_minor(cache_k), time_minor(cache_v))
    return both[..., 0].reshape(bs, ATTN_WIDTH), both[..., 1].reshape(bs, ATTN_WIDTH)


def _rwkv_prep_body(seq_mode, p_ref, prev_ref, mu_ref, w0_ref, a0_ref, kk_ref, ka_ref, rk_ref,
                    w2_ref, a2_ref, g2_ref,
                    r_o, lw_o, kf_o, v_o, kn_o, al_o, g_o, bonus_o):
    p = p_ref[...]
    if seq_mode:
        row0 = lax.broadcasted_iota(jnp.int32, p.shape, 0) == 0
        p_prev = jnp.where(row0, prev_ref[...], pltpu.roll(p, 1, 0))
    else:
        p_prev = prev_ref[...]
    z = p + (p_prev - p) * mu_ref[...]
    r = z[:, 0:RWKV_WIDTH]
    k = z[:, RWKV_WIDTH:2 * RWKV_WIDTH]
    v = z[:, 2 * RWKV_WIDTH:3 * RWKV_WIDTH]
    tail = z[:, 3 * RWKV_WIDTH:3 * RWKV_WIDTH + LORA_PAD]
    dw = _mm3(jnp.tanh(tail), w2_ref[...])
    da = _mm3(tail, a2_ref[...])
    g = _mm3(1.0 / (1.0 + jnp.exp(-tail)), g2_ref[...])
    u = -(w0_ref[...] + dw)
    softplus = jnp.maximum(u, 0.0) + jnp.log1p(jnp.exp(-jnp.abs(u)))
    lw = -jnp.exp(-softplus - 0.5)
    alpha = 1.0 / (1.0 + jnp.exp(-(a0_ref[...] + da)))
    gmat = _group_ones(LANE)
    kk = k * kk_ref[...]
    nrm = jnp.sqrt(_group_sum(kk * kk, gmat))
    kn = kk / jnp.maximum(nrm, 1e-12)
    kf = k * (1.0 + (alpha - 1.0) * ka_ref[...])
    bonus = _group_sum(r * kf * rk_ref[...], gmat) * v
    lw_o[...] = lw
    for ref, val in ((r_o, r), (kf_o, kf), (v_o, v), (kn_o, kn), (al_o, alpha), (g_o, g), (bonus_o, bonus)):
        ref[...] = val.astype(ref.dtype)


def _rwkv_prep(seq_mode, proj, prev, params, tm, tiles_per_seq):
    n = proj.shape[0]
    mu, w0, a0, k_k, k_a, r_k, w2p, a2p, g2p = params
    p_spec = pl.BlockSpec((tm, P_PAD), lambda i: (i, 0))
    if seq_mode:
        prev_rows = _prev_rows(proj, prev, tm, tiles_per_seq)
        prev_spec = pl.BlockSpec((None, 1, P_PAD), lambda i: (i, 0, 0))
        prev_arg = prev_rows
    else:
        prev_spec = pl.BlockSpec((tm, P_PAD), lambda i: (i, 0))
        prev_arg = prev
    vec = lambda width: pl.BlockSpec((1, width), lambda i: (0, 0))
    mat = pl.BlockSpec((LORA_PAD, RWKV_WIDTH), lambda i: (0, 0))
    out_spec = pl.BlockSpec((tm, RWKV_WIDTH), lambda i: (i, 0))
    vec_dtype = BF16 if seq_mode else F32
    out = lambda dt: jax.ShapeDtypeStruct((n, RWKV_WIDTH), dt)
    return pl.pallas_call(
        functools.partial(_rwkv_prep_body, seq_mode),
        out_shape=(out(vec_dtype), out(F32)) + (out(vec_dtype),) * 6,
        grid=(n // tm,),
        in_specs=[p_spec, prev_spec, vec(P_PAD)] + [vec(RWKV_WIDTH)] * 5 + [mat] * 3,
        out_specs=(out_spec,) * 8,
        compiler_params=_cparams(("arbitrary",)),
        name="rwkv_prep_seq" if seq_mode else "rwkv_prep_row",
    )(proj, prev_arg, mu, w0, a0, k_k, k_a, r_k, w2p, a2p, g2p)


def _prev_rows(proj, shift0, tm, tiles_per_seq):
    n = proj.shape[0]
    tiles = n // tm
    last = proj[tm - 1::tm, :P_PAD][:tiles - 1]
    rows = jnp.concatenate([jnp.zeros((1, P_PAD), F32), last], axis=0).reshape(tiles // tiles_per_seq, tiles_per_seq, P_PAD)
    rows = rows.at[:, 0, :].set(shift0[:, 0, :])
    return rows.reshape(tiles, 1, P_PAD)


def _rwkv_scan_body(r_ref, lw_ref, kf_ref, v_ref, kn_ref, al_ref, g_ref, bonus_ref, lnw_ref, lnb_ref, h0_ref,
                    y_ref, hT_ref, h_s):
    c = CHUNK
    gc = RWKV_GROUP * c
    n = 2 * gc
    tt = r_ref.shape[0]
    n_inst = r_ref.shape[1] // LANE
    ti = pl.program_id(2)

    @pl.when(ti == 0)
    def _():
        h_s[...] = h0_ref[...]

    head0 = lax.broadcasted_iota(jnp.int32, (c, LANE), 1) < HEAD_DIM
    ri = lax.broadcasted_iota(jnp.int32, (n, n), 0)
    ci = lax.broadcasted_iota(jnp.int32, (n, n), 1)
    same_block = (ri // c) == (ci // c)
    strict = same_block & (ci < ri)
    incl = same_block & (ci <= ri)
    eye = ri == ci
    eye_l = lax.broadcasted_iota(jnp.int32, (LANE, LANE), 0) == lax.broadcasted_iota(jnp.int32, (LANE, LANE), 1)
    tr = lax.broadcasted_iota(jnp.int32, (gc, gc), 0)
    tc = lax.broadcasted_iota(jnp.int32, (gc, gc), 1)
    tri = jnp.where(((tr // c) == (tc // c)) & (tc <= tr), 1.0, 0.0).astype(BF16)
    gmat = _group_ones(LANE)

    def stack(x):
        parts = []
        for k in range(RWKV_GROUP):
            xk = x[k * c:(k + 1) * c, :]
            parts += [jnp.where(head0, xk, 0.0), jnp.where(head0, 0.0, xk)]
        return jnp.concatenate(parts, axis=0)

    def twice(x):
        parts = []
        for k in range(RWKV_GROUP):
            xk = x[k * c:(k + 1) * c, :]
            parts += [xk, xk]
        return jnp.concatenate(parts, axis=0)

    def solve(gi, j):
        rows = pl.ds(pl.multiple_of(gi * gc, gc), gc)
        ln = slice(j * LANE, (j + 1) * LANE)
        lw = lw_ref[rows, ln]
        r, kf, v, kn, al = (ref[rows, ln].astype(F32) for ref in (r_ref, kf_ref, v_ref, kn_ref, al_ref))
        gcum = _mm2r(tri, lw)
        g_end = jnp.concatenate([jnp.broadcast_to(gcum[(k + 1) * c - 1:(k + 1) * c, :], (c, LANE))
                                 for k in range(RWKV_GROUP)], axis=0)
        at = -kn * jnp.exp(gcum - lw)
        rt = r * jnp.exp(gcum)
        inv = jnp.exp(-gcum)
        bt = kn * al * inv
        kt = kf * inv
        dec_end = jnp.exp(g_end)
        at_st, rt_st, v_st = stack(at), stack(rt), stack(v)
        b2_st, k2_st = stack(bt * dec_end), stack(kt * dec_end)
        big = _mm(jnp.concatenate([at_st, rt_st], axis=0), jnp.concatenate([twice(bt), twice(kt)], axis=0), _NT)
        a_ab = jnp.where(strict, big[0:n, 0:n], 0.0)
        a_ak = jnp.where(strict, big[0:n, n:2 * n], 0.0)
        a_rb = jnp.where(incl, big[n:2 * n, 0:n], 0.0)
        a_rk = jnp.where(incl, big[n:2 * n, n:2 * n], 0.0)
        x = jnp.where(eye, 1.0, 0.0) + a_ab
        yield
        pw = _mm(a_ab, a_ab)
        yield
        steps = int(np.log2(c)) - 1
        for s in range(steps):
            if s < steps - 1:
                both = _mm(pw, jnp.concatenate([pw, x], axis=1))
                x = x + both[:, n:]
                pw = both[:, :n]
            else:
                x = x + _mm(pw, x)
            yield
        av_st = _mm(a_ak, v_st)
        yield
        wu_uv = _mm(x, jnp.concatenate([at_st, av_st], axis=1))
        wu_st, uv_st = wu_uv[:, :LANE], wu_uv[:, LANE:]
        yield
        rb = _mm(a_rb, wu_uv)
        r2_st = rt_st + rb[:, :LANE]
        yv_st = rb[:, LANE:] + _mm(a_rk, v_st)
        yield
        trans, add = [], []
        for k in range(RWKV_GROUP):
            blk = slice(2 * k * c, 2 * (k + 1) * c)
            trans.append(jnp.where(eye_l, dec_end[k * c:k * c + 1, :], 0.0) + _mm(b2_st[blk], wu_st[blk], _TN))
            add.append(_mm(jnp.concatenate([b2_st[blk], k2_st[blk]], axis=0),
                           jnp.concatenate([uv_st[blk], v_st[blk]], axis=0), _TN))
        yield
        hmat = h_s[j]
        ys = []
        for k in range(RWKV_GROUP):
            blk = slice(2 * k * c, 2 * (k + 1) * c)
            y_st = _mm(r2_st[blk], hmat) + yv_st[blk]
            hmat = _mm(trans[k], hmat) + add[k]
            ys.append(y_st[:c, :] + y_st[c:, :])
            yield
        h_s[j] = hmat
        y = ys[0] if len(ys) == 1 else jnp.concatenate(ys, axis=0)
        mean = _group_sum(y, gmat) * (1.0 / HEAD_DIM)
        dev = y - mean
        var = _group_sum(dev * dev, gmat) * (1.0 / HEAD_DIM)
        yn = dev * lax.rsqrt(var + GN_EPS) * lnw_ref[:, ln] + lnb_ref[:, ln]
        y_ref[rows, ln] = (yn + bonus_ref[rows, ln].astype(F32)) * g_ref[rows, ln].astype(F32)

    def group(gi, carry):
        _lockstep([solve(gi, j) for j in range(n_inst)])
        return carry

    lax.fori_loop(0, tt // gc, group, 0)

    @pl.when(ti == pl.num_programs(2) - 1)
    def _():
        hT_ref[...] = h_s[...]


def _rwkv_scan(vecs, lnw, lnb, h0, b, t):
    pairs = RWKV_WIDTH // LANE
    tt = min(t, SCAN_ROWS)
    nt = t // tt
    ni = RWKV_INTERLEAVE
    seq = pl.BlockSpec((tt, ni * LANE), lambda bi, hp, ti: (bi * nt + ti, hp))
    vec = pl.BlockSpec((1, ni * LANE), lambda bi, hp, ti: (0, hp))
    st = pl.BlockSpec((None, ni, LANE, LANE), lambda bi, hp, ti: (bi, hp, 0, 0))
    return pl.pallas_call(
        _rwkv_scan_body,
        out_shape=(jax.ShapeDtypeStruct((b * t, RWKV_WIDTH), F32),
                   jax.ShapeDtypeStruct((b, pairs, LANE, LANE), F32)),
        grid=(b, pairs // ni, nt),
        in_specs=[seq] * 8 + [vec, vec, st],
        out_specs=(seq, st),
        scratch_shapes=[pltpu.VMEM((ni, LANE, LANE), F32)],
        compiler_params=_cparams(("arbitrary", "arbitrary", "arbitrary")),
        name="rwkv_scan",
    )(*vecs, lnw.reshape(1, RWKV_WIDTH), lnb.reshape(1, RWKV_WIDTH), h0)


def _rwkv_step_body(s_ref, lw_ref, kn_ref, al_ref, k_ref, r_ref, vgb_ref, lnw_ref, lnb_ref, y_ref, so_ref):
    s = s_ref[...]
    kn = kn_ref[...]
    sa = jnp.sum(s * (-kn), axis=-1, keepdims=True)
    vgb = vgb_ref[...]
    s_new = s * jnp.exp(lw_ref[...]) + sa * (kn * al_ref[...]) + vgb[:, :, 0:1] * k_ref[...]
    so_ref[...] = s_new
    y = jnp.sum(s_new * r_ref[...], axis=-1, keepdims=True)
    mean = jnp.mean(y, axis=1, keepdims=True)
    dev = y - mean
    var = jnp.mean(dev * dev, axis=1, keepdims=True)
    yn = dev * lax.rsqrt(var + GN_EPS) * lnw_ref[...] + lnb_ref[...]
    y_ref[...] = (yn + vgb[:, :, 2:3]) * vgb[:, :, 1:2]


def _rwkv_step(state, vecs, lnw, lnb):
    bs = state.shape[0]
    r, lw, kf, v, kn, al, g, bonus = vecs
    rowv = lambda a: a.reshape(bs, RWKV_HEADS, 1, HEAD_DIM)
    vgb = jnp.stack([v, g, bonus], axis=-1).reshape(bs, RWKV_HEADS, HEAD_DIM, 3)
    row_spec = pl.BlockSpec((None, RWKV_HEADS, 1, HEAD_DIM), lambda b: (b, 0, 0, 0))
    col_spec = pl.BlockSpec((None, RWKV_HEADS, HEAD_DIM, 1), lambda b: (b, 0, 0, 0))
    vgb_spec = pl.BlockSpec((None, RWKV_HEADS, HEAD_DIM, 3), lambda b: (b, 0, 0, 0))
    st_spec = pl.BlockSpec((None, RWKV_HEADS, HEAD_DIM, HEAD_DIM), lambda b: (b, 0, 0, 0))
    par_spec = pl.BlockSpec((RWKV_HEADS, HEAD_DIM, 1), lambda b: (0, 0, 0))
    y, s_new = pl.pallas_call(
        _rwkv_step_body,
        out_shape=(jax.ShapeDtypeStruct((bs, RWKV_HEADS, HEAD_DIM, 1), F32),
                   jax.ShapeDtypeStruct(state.shape, F32)),
        grid=(bs,),
        in_specs=[st_spec] + [row_spec] * 5 + [vgb_spec] + [par_spec] * 2,
        out_specs=(col_spec, st_spec),
        compiler_params=_cparams(("arbitrary",)),
        name="rwkv_step",
    )(state, rowv(lw), rowv(kn), rowv(al), rowv(kf), rowv(r), vgb,
      lnw.reshape(RWKV_HEADS, HEAD_DIM, 1), lnb.reshape(RWKV_HEADS, HEAD_DIM, 1))
    return y.reshape(bs, RWKV_WIDTH), s_new


def _outproj_body(a_ref, r_ref, x_ref, gt_ref, w_ref, g2_ref, sc_ref, sh_ref, x1_ref, h2_ref):
    y = _dot(_bf(a_ref[...]), w_ref[0:ATTN_WIDTH, :]) + _dot(_bf(r_ref[...]), w_ref[ATTN_WIDTH:, :])
    x1 = x_ref[...] + gt_ref[...] * y
    x1_ref[...] = x1
    h2_ref[...] = _modulated_norm(x1, g2_ref[...], sc_ref[...], sh_ref[...])


def _outproj(attn, rw, x2d, gt, w_bf, g2, sc, sh, tm, tiles_per_group):
    n = x2d.shape[0]
    r = gt.shape[1]
    half = pl.BlockSpec((tm, ATTN_WIDTH), lambda i: (i, 0))
    full = pl.BlockSpec((tm, D_MODEL), lambda i: (i, 0))
    mod = pl.BlockSpec((None, r, D_MODEL), lambda i: (i // tiles_per_group, 0, 0))
    return pl.pallas_call(
        _outproj_body,
        out_shape=(jax.ShapeDtypeStruct((n, D_MODEL), F32), jax.ShapeDtypeStruct((n, D_MODEL), F32)),
        grid=(n // tm,),
        in_specs=[half, half, full, mod, pl.BlockSpec((D_MODEL, D_MODEL), lambda i: (0, 0)),
                  pl.BlockSpec((1, D_MODEL), lambda i: (0, 0)), mod, mod],
        out_specs=(full, full),
        compiler_params=_cparams(("arbitrary",)),
        name="outproj",
    )(attn, rw, x2d, gt, w_bf, g2.reshape(1, D_MODEL), sc, sh)


def _staircase():
    pairs = [(a, b) for a in range(PEER_TOPK) for b in range(PEER_TOPK) if (a + 1) * (b + 1) <= PEER_TOPK]
    return pairs


_CAND = _staircase()
_CAND_ROWS = 56


_SENTINEL = 2.0 ** 100


def _extract_top(s, count, rows, tie_safe):
    vals = []
    limit = float(s.shape[0])
    for kth in range(count):
        mx = jnp.max(s, axis=0, keepdims=True)
        if tie_safe:
            hit = rows == jnp.min(jnp.where(s == mx, rows, limit), axis=0, keepdims=True)
        else:
            hit = s == mx
        s = jnp.where(hit, _rank_mark(kth), s)
        vals.append(mx)
        yield
    return vals, s


def _rank_mark(kth):
    return -_SENTINEL * (1.0 + kth / 16.0)


def _was_taken(s):
    return s < -0.5 * _SENTINEL


def _decode_rank(s):
    return jnp.where(_was_taken(s), (s * (-1.0 / _SENTINEL) - 1.0) * 16.0, 99.0)


def _router_body(h2_ref, wq_ref, keys_ref, oh_ref, h2t_ref, cnt_ref, e1_ref, rk_ref, e2_ref, q_s):
    h2 = h2_ref[...]
    tm = h2.shape[0]
    h2t_ref[...] = _bf(h2.T)
    q_s[...] = _dot(wq_ref[...], _bf(h2), _NT)
    rows = lax.broadcasted_iota(jnp.int32, (PEER_KEYS, LANE), 0).astype(F32)
    crow = lax.broadcasted_iota(jnp.int32, (_CAND_ROWS, LANE), 0).astype(F32)

    def route(h, tie_safe):
        scores = []
        for half in range(2):
            g = 2 * h + half
            q_t = q_s[pl.ds(pl.multiple_of(g * PEER_KEYS, PEER_KEYS), PEER_KEYS), :]
            scores.append(_mm3(keys_ref[g], q_t))
        wrong = jnp.zeros((1, LANE), F32)
        for sub in range(tm // LANE):
            sl = slice(sub * LANE, (sub + 1) * LANE)
            s0, s1 = scores[0][:, sl], scores[1][:, sl]
            (tops0, fin0), (tops1, fin1) = yield from _lockstep_gen(
                [_extract_top(s0, PEER_TOPK, rows, tie_safe), _extract_top(s1, PEER_TOPK, rows, tie_safe)])
            cand = [tops0[a] + tops1[b] for a, b in _CAND]
            cand += [jnp.full((1, LANE), NEG, F32)] * (_CAND_ROWS - len(cand))
            cand = jnp.concatenate(cand, axis=0)
            top = tops0[0] + tops1[0]
            taken = jnp.zeros((_CAND_ROWS, LANE), F32)
            zsum = jnp.zeros((1, LANE), F32)
            for _ in range(PEER_TOPK):
                mx = jnp.max(cand, axis=0, keepdims=True)
                if tie_safe:
                    hit = crow == jnp.min(jnp.where(cand == mx, crow, float(_CAND_ROWS)), axis=0, keepdims=True)
                else:
                    hit = cand == mx
                taken = jnp.where(hit, 1.0, taken)
                cand = jnp.where(hit, NEG, cand)
                zsum = zsum + jnp.exp(mx - top)
                yield
            per_rank = _dot(oh_ref[...], _bf(taken))
            cnt = jnp.full((PEER_KEYS, LANE), -0.5, F32)
            for kth in range(PEER_TOPK):
                cnt = jnp.where(fin0 == _rank_mark(kth), per_rank[kth:kth + 1, :] - 0.5, cnt)
            in0, in1 = _was_taken(fin0), _was_taken(fin1)
            cnt_ref[h, :, sl] = cnt
            rk_ref[h, :, sl] = _bf(_decode_rank(fin1))
            e1_ref[h, :, sl] = jnp.where(in0, jnp.exp(s0 - tops0[0]), 0.0)
            e2_ref[h, :, sl] = _bf(jnp.where(in1, jnp.exp(s1 - tops1[0]) / zsum, 0.0))
            for mask in (jnp.where(in0, 1.0, 0.0), jnp.where(in1, 1.0, 0.0), taken):
                wrong = wrong + jnp.abs(jnp.sum(mask, axis=0, keepdims=True) - float(PEER_TOPK))
        return wrong

    def heads(tie_safe):
        def body(hi, wrong):
            res = _lockstep([route(hi * ROUTER_INTERLEAVE + k, tie_safe) for k in range(ROUTER_INTERLEAVE)])
            return wrong + sum(res)
        return lax.fori_loop(0, PEER_HEADS // ROUTER_INTERLEAVE, body, jnp.zeros((1, LANE), F32))

    wrong = heads(False)

    @pl.when(jnp.max(wrong) > 0.0)
    def _():
        heads(True)


def _router(h2, wq_t_bf, keys, tm):
    n = h2.shape[0]
    onehot = np.zeros((PEER_TOPK, _CAND_ROWS), np.float32)
    for idx, (a, _) in enumerate(_CAND):
        onehot[a, idx] = 1.0
    tok = pl.BlockSpec((None, PEER_HEADS, PEER_KEYS, tm), lambda i: (i, 0, 0, 0))
    tok_shape = lambda dt: jax.ShapeDtypeStruct((n // tm, PEER_HEADS, PEER_KEYS, tm), dt)
    return pl.pallas_call(
        _router_body,
        out_shape=(jax.ShapeDtypeStruct((D_MODEL, n), BF16),
                   tok_shape(F32), tok_shape(F32), tok_shape(BF16), tok_shape(BF16)),
        grid=(n // tm,),
        in_specs=[pl.BlockSpec((tm, D_MODEL), lambda i: (i, 0)),
                  pl.BlockSpec((D_MODEL, D_MODEL), lambda i: (0, 0)),
                  pl.BlockSpec((2 * PEER_HEADS, PEER_KEYS, PEER_KEYS), lambda i: (0, 0, 0)),
                  pl.BlockSpec((PEER_TOPK, _CAND_ROWS), lambda i: (0, 0))],
        out_specs=(pl.BlockSpec((D_MODEL, tm), lambda i: (0, i)), tok, tok, tok, tok),
        scratch_shapes=[pltpu.VMEM((D_MODEL, tm), F32)],
        compiler_params=_cparams(("arbitrary",)),
        name="peer_router",
    )(h2, wq_t_bf, keys, jnp.asarray(onehot, BF16))


def _experts_body(h2t_ref, u_ref, v_ref, cnt_ref, e1_ref, rk_ref, e2_ref, x1_ref, gt_ref, y_ref, p_s):
    j = pl.program_id(1)
    last = pl.num_programs(1) - 1
    eb = u_ref.shape[0]
    per = eb // PEER_KEYS
    cur = j % 2

    @pl.when(j == 0)
    def _():
        y_ref[...] = jnp.zeros(y_ref.shape, F32)
        p_s[1] = jnp.zeros(p_s.shape[1:], BF16)

    @pl.when(j < last)
    def _():
        y_ref[...] += _dot(p_s[1 - cur], v_ref[...], _TN)
        h2t = h2t_ref[...]
        for ii in range(per):
            sub = slice(ii * PEER_KEYS, (ii + 1) * PEER_KEYS)
            i = j * per + ii
            gates = []
            for part in range(cnt_ref.shape[0]):
                wt = None
                for h in range(PEER_HEADS):
                    partners = _bf(cnt_ref[part, h, pl.ds(i, 1), :])
                    e1 = _bf(e1_ref[part, h, pl.ds(i, 1), :])
                    term = jnp.where(rk_ref[part, h] < partners, e2_ref[part, h], 0.0) * e1
                    wt = term if wt is None else wt + term
                gates.append(wt)
            wt = gates[0] if len(gates) == 1 else jnp.concatenate(gates, axis=1)
            act_in = _dot(u_ref[sub, :], h2t)
            act = 0.5 * act_in * (1.0 + lax.erf(act_in * (2.0 ** -0.5)))
            p_s[cur, sub, :] = wt * _bf(act)

    @pl.when(j == last)
    def _():
        acc = y_ref[...] + _dot(p_s[1 - cur], v_ref[...], _TN)
        y_ref[...] = x1_ref[...] + gt_ref[...] * acc


def _experts(h2t, u_bf, v_bf, route, x1, gt, tm, tiles_per_group):
    n = x1.shape[0]
    eb = EXPERT_BLOCK
    nblk = PEER_EXPERTS // eb
    r = gt.shape[1]
    slab = route[0].shape[-1]
    tok = pl.BlockSpec((tm // slab, PEER_HEADS, PEER_KEYS, slab), lambda i, j: (i, 0, 0, 0))
    return pl.pallas_call(
        _experts_body,
        out_shape=jax.ShapeDtypeStruct((n, D_MODEL), F32),
        grid=(n // tm, nblk + 1),
        in_specs=[pl.BlockSpec((D_MODEL, tm), lambda i, j: (0, i)),
                  pl.BlockSpec((eb, D_MODEL), lambda i, j: (jnp.minimum(j, nblk - 1), 0)),
                  pl.BlockSpec((eb, D_MODEL), lambda i, j: (jnp.maximum(j - 1, 0), 0)),
                  tok, tok, tok, tok,
                  pl.BlockSpec((tm, D_MODEL), lambda i, j: (i, 0)),
                  pl.BlockSpec((None, r, D_MODEL), lambda i, j: (i // tiles_per_group, 0, 0))],
        out_specs=pl.BlockSpec((tm, D_MODEL), lambda i, j: (i, 0)),
        scratch_shapes=[pltpu.VMEM((2, eb, tm), BF16)],
        compiler_params=_cparams(("arbitrary", "arbitrary")),
        name="peer_experts",
    )(h2t, u_bf, v_bf, *route, x1, gt)


def _pad_rows(a, rows):
    return jnp.pad(a, ((0, rows - a.shape[0]),) + ((0, 0),) * (a.ndim - 1))


def _state_to_pairs(state):
    b = state.shape[0]
    h = jnp.swapaxes(state, -1, -2).reshape(b, RWKV_HEADS // 2, 2, HEAD_DIM, HEAD_DIM)
    z = jnp.zeros_like(h[:, :, 0])
    top = jnp.concatenate([h[:, :, 0], z], axis=-1)
    bot = jnp.concatenate([z, h[:, :, 1]], axis=-1)
    return jnp.concatenate([top, bot], axis=-2)


def _pairs_to_state(hp):
    b = hp.shape[0]
    h0 = hp[:, :, :HEAD_DIM, :HEAD_DIM]
    h1 = hp[:, :, HEAD_DIM:, HEAD_DIM:]
    h = jnp.stack([h0, h1], axis=2).reshape(b, RWKV_HEADS, HEAD_DIM, HEAD_DIM)
    return jnp.swapaxes(h, -1, -2)


def kernel(x_prompt, x_sample, cache_attn_k, cache_attn_v, state_rwkv, state_rwkv_shift, c_prompt, c_sample, w_ada, b_ada, norm1_g, norm2_g, w_in, q_gain, k_gain, rwkv_mu, rwkv_w0, rwkv_w2, rwkv_a0, rwkv_a2, rwkv_g2, rwkv_k_k, rwkv_k_a, rwkv_r_k, rwkv_lnx_w, rwkv_lnx_b, w_out, peer_w_q, peer_sub_keys, peer_u, peer_v):
    b, t, _ = x_prompt.shape
    bs = x_sample.shape[0]
    n = b * t
    heads = ATTN_WIDTH // HEAD_DIM
    win = min(CACHE_LEN, t)

    w_out_bf = w_out.astype(BF16)
    wq_t_bf = peer_w_q.T.astype(BF16)
    keys = peer_sub_keys.reshape(2 * PEER_HEADS, PEER_KEYS, PEER_KEYS)
    u_bf = peer_u.astype(BF16)
    v_bf = peer_v.astype(BF16)
    lora = lambda w, off: jnp.zeros((LORA_PAD, RWKV_WIDTH), F32).at[off:off + w.shape[0]].set(w)
    row = lambda a: a.reshape(1, -1)
    rwkv_params = (row(jnp.pad(rwkv_mu, (0, P_PAD - RWKV_PROJ))), row(rwkv_w0), row(rwkv_a0), row(rwkv_k_k),
                   row(rwkv_k_a), row(rwkv_r_k), lora(rwkv_w2, 0), lora(rwkv_a2, DECAY_RANK),
                   lora(rwkv_g2, DECAY_RANK + A_RANK))

    c_rows = b + bs
    c_pad = -(-c_rows // 8) * 8
    mod = _adaln(_pad_rows(jnp.concatenate([c_prompt, c_sample], axis=0), c_pad), w_ada, b_ada)
    mod_p = [m.reshape(b, 1, D_MODEL) for m in jnp.split(mod[:b], 6, axis=-1)]
    mod_s = [m.reshape(1, bs, D_MODEL) for m in jnp.split(mod[b:c_rows], 6, axis=-1)]

    tm_p = min(TM_INPROJ, t)
    xp = x_prompt.reshape(n, D_MODEL)
    proj_p = _inproj(xp, norm1_g, mod_p[1], mod_p[0], w_in, tm_p, t // tm_p)
    attn_p, k_cache, v_cache = _attn_prompt(proj_p.reshape(b, t, PROJ_PAD), q_gain, k_gain, win)
    tm_r = min(TM_PREP, t)
    shift0 = jnp.zeros((b, 1, P_PAD), F32)
    vecs_p = _rwkv_prep(True, proj_p, shift0, rwkv_params, tm_r, t // tm_r)
    rw_p, h_fin = _rwkv_scan(vecs_p, rwkv_lnx_w, rwkv_lnx_b, jnp.zeros((b, RWKV_HEADS // 2, LANE, LANE), F32), b, t)
    tm_o = min(TM_OUTPROJ, t)
    x1_p, h2_p = _outproj(attn_p.reshape(n, ATTN_WIDTH), rw_p, xp, mod_p[2], w_out_bf, norm2_g,
                          mod_p[4], mod_p[3], tm_o, t // tm_o)
    tm_q = min(TM_ROUTE, t)
    tm_e = min(TM_EXPERT, t)
    h2t_p, *route_p = _router(h2_p, wq_t_bf, keys, tm_q)
    y_p = _experts(h2t_p, u_bf, v_bf, route_p, x1_p, mod_p[5], tm_e, t // tm_e)

    xs = x_sample.reshape(bs, D_MODEL)
    proj_s = _inproj(xs, norm1_g, mod_s[1], mod_s[0], w_in, bs, 1)
    q_s = proj_s[:, P_PAD:P_PAD + ATTN_WIDTH]
    k_s = proj_s[:, P_PAD + ATTN_WIDTH:P_PAD + 2 * ATTN_WIDTH]
    v_s = proj_s[:, P_PAD + 2 * ATTN_WIDTH:]
    attn_s, kn_s = _attn_sample(q_s, k_s, v_s, q_gain, k_gain, cache_attn_k, cache_attn_v)
    prev_s = jnp.pad(state_rwkv_shift.reshape(bs, RWKV_PROJ), ((0, 0), (0, P_PAD - RWKV_PROJ)))
    vecs_s = _rwkv_prep(False, proj_s, prev_s, rwkv_params, bs, 1)
    rw_s, state_s = _rwkv_step(state_rwkv, vecs_s, rwkv_lnx_w, rwkv_lnx_b)
    x1_s, h2_s = _outproj(attn_s, rw_s, xs, mod_s[2], w_out_bf, norm2_g, mod_s[4], mod_s[3], bs, 1)
    ns = -(-bs // LANE) * LANE
    h2t_s, *route_s = _router(_pad_rows(h2_s, ns), wq_t_bf, keys, LANE)
    gt2_s = _pad_rows(mod_s[5][0], ns).reshape(ns // LANE, LANE, D_MODEL)
    y_s = _experts(h2t_s, u_bf, v_bf, route_s, _pad_rows(x1_s, ns), gt2_s, LANE, 1)[:bs]

    return (y_p.reshape(b, t, D_MODEL), y_s.reshape(bs, 1, D_MODEL),
            k_cache.reshape(b, win, heads, HEAD_DIM), v_cache.reshape(b, win, heads, HEAD_DIM),
            _pairs_to_state(h_fin), proj_p.reshape(b, t, PROJ_PAD)[:, t - 1:, :RWKV_PROJ],
            kn_s.reshape(bs, 1, heads, HEAD_DIM), v_s.reshape(bs, 1, heads, HEAD_DIM),
            state_s, proj_s[:, :RWKV_PROJ].reshape(bs, 1, RWKV_PROJ))
```

```python
import functools

import numpy as np
import jax
import jax.numpy as jnp
from jax import lax
from jax.experimental import pallas as pl
from jax.experimental.pallas import tpu as pltpu

F32 = jnp.float32
BF16 = jnp.bfloat16

D_MODEL = 2048
HEAD_DIM = 64
ATTN_WIDTH = 1024
RWKV_WIDTH = 1024
RWKV_HEADS = RWKV_WIDTH // HEAD_DIM
DECAY_RANK = 64
A_RANK = 64
GATE_RANK = 160
RWKV_PROJ = 3 * RWKV_WIDTH + DECAY_RANK + A_RANK + GATE_RANK
P_PAD = 3584
PROJ_PAD = P_PAD + 3 * ATTN_WIDTH
LORA_PAD = 384
DIL_PATTERNS = ((128, 1), (512, 4), (2048, 16))
CACHE_LEN = 2048
NORM_EPS = 1e-6
GN_EPS = HEAD_DIM * 1e-5
PEER_HEADS = 8
PEER_KEYS = 128
PEER_TOPK = 16
PEER_EXPERTS = PEER_KEYS * PEER_KEYS
NEG = -1e30

LANE = 128
QBLK = 128
CHUNK = 64
RWKV_GROUP = 1
RWKV_INTERLEAVE = 8
ROUTER_INTERLEAVE = 4
ATTN_INTERLEAVE = 8
VMEM_LIMIT = 56 * 1024 * 1024

TM_INPROJ = 1024
TM_PREP = 256
TM_OUTPROJ = 256
TM_ROUTE = 256
TM_EXPERT = 512
EXPERT_BLOCK = 1024
SCAN_ROWS = 256
ADALN_COLS = 1024
NORM_ROWS = 512
SAMPLE_HEADS = 8


def _cparams(sem):
    return pltpu.CompilerParams(dimension_semantics=sem, vmem_limit_bytes=VMEM_LIMIT)


_NN = (((1,), (0,)), ((), ()))
_NT = (((1,), (1,)), ((), ()))
_TN = (((0,), (0,)), ((), ()))


def _bf(x):
    return x.astype(BF16)


def _dot(a, b, dims=_NN):
    return lax.dot_general(a, b, dims, preferred_element_type=F32)


def _mm(a, b, dims=_NN):
    return _dot(_bf(a), _bf(b), dims)


def _split(x):
    hi = _bf(x)
    lo = _bf(x - hi.astype(F32))
    return hi, lo


def _mm3(a, b, dims=_NN):
    ah, al = _split(a)
    bh, bl = _split(b)
    return _dot(ah, bh, dims) + (_dot(ah, bl, dims) + _dot(al, bh, dims))


def _mm3r(a, b_hi, b_lo, dims=_NN):
    ah, al = _split(a)
    return _dot(ah, b_hi, dims) + (_dot(ah, b_lo, dims) + _dot(al, b_hi, dims))


def _mm2l(a, b_exact, dims=_NN):
    ah, al = _split(a)
    return _dot(ah, b_exact, dims) + _dot(al, b_exact, dims)


def _mm2r(a_exact, b, dims=_NN):
    bh, bl = _split(b)
    return _dot(a_exact, bh, dims) + _dot(a_exact, bl, dims)


def _group_ones(n):
    r = lax.broadcasted_iota(jnp.int32, (n, n), 0) // HEAD_DIM
    c = lax.broadcasted_iota(jnp.int32, (n, n), 1) // HEAD_DIM
    return jnp.where(r == c, 1.0, 0.0).astype(BF16)


def _group_sum(x, g):
    cols = x.shape[1] // LANE
    parts = [_mm2l(x[:, c * LANE:(c + 1) * LANE], g) for c in range(cols)]
    return parts[0] if cols == 1 else jnp.concatenate(parts, axis=1)


def _lockstep_gen(gens):
    results = [None] * len(gens)
    live = list(enumerate(gens))
    while live:
        nxt = []
        for idx, gen in live:
            try:
                next(gen)
                nxt.append((idx, gen))
            except StopIteration as stop:
                results[idx] = stop.value
        live = nxt
        if live:
            yield
    return results


def _lockstep(gens):
    runner = _lockstep_gen(gens)
    while True:
        try:
            next(runner)
        except StopIteration as stop:
            return stop.value


def _adaln_body(c_ref, w_ref, b_ref, o_ref):
    c = c_ref[...]
    s = c * (1.0 / (1.0 + jnp.exp(-c)))
    o_ref[...] = _mm3(s, w_ref[...]) + b_ref[...]


def _adaln(c, w_ada, b_ada):
    rows = c.shape[0]
    n = w_ada.shape[1]
    tn = ADALN_COLS
    return pl.pallas_call(
        _adaln_body,
        out_shape=jax.ShapeDtypeStruct((rows, n), F32),
        grid=(n // tn,),
        in_specs=[pl.BlockSpec((rows, D_MODEL), lambda j: (0, 0)),
                  pl.BlockSpec((D_MODEL, tn), lambda j: (0, j)),
                  pl.BlockSpec((1, tn), lambda j: (0, j))],
        out_specs=pl.BlockSpec((rows, tn), lambda j: (0, j)),
        compiler_params=_cparams(("arbitrary",)),
        name="adaln",
    )(c, w_ada, b_ada.reshape(1, n))


def _modulated_norm(x, g, sc, sh):
    ms = jnp.mean(x * x, axis=-1, keepdims=True)
    return (x * lax.rsqrt(ms + NORM_EPS) * g) * (1.0 + sc) + sh


def _inproj_body(x_ref, g_ref, sc_ref, sh_ref, w_ref, o_ref, h_s):
    j = pl.program_id(1)

    @pl.when(j == 0)
    def _():
        h_s[...] = _bf(_modulated_norm(x_ref[...], g_ref[...], sc_ref[...], sh_ref[...]))

    tn = w_ref.shape[1]
    valid = jnp.where(j == _P_TILES - 1, RWKV_PROJ - (_P_TILES - 1) * tn, tn)
    cols = lax.broadcasted_iota(jnp.int32, w_ref.shape, 1)
    o_ref[...] = _dot(h_s[...], _bf(jnp.where(cols < valid, w_ref[...], 0.0)))


_IN_TN = 512
_P_TILES = P_PAD // _IN_TN


def _inproj(x2d, g, sc, sh, w_in, tm, tiles_per_group):
    n = x2d.shape[0]
    r = sc.shape[1]
    tn = _IN_TN
    qkv_tiles = 3 * ATTN_WIDTH // tn
    src = lambda j: jnp.where(j < _P_TILES, j + qkv_tiles, j - _P_TILES)
    mod_spec = pl.BlockSpec((None, r, D_MODEL), lambda i, j: (i // tiles_per_group, 0, 0))
    return pl.pallas_call(
        _inproj_body,
        out_shape=jax.ShapeDtypeStruct((n, PROJ_PAD), F32),
        grid=(n // tm, PROJ_PAD // tn),
        in_specs=[pl.BlockSpec((tm, D_MODEL), lambda i, j: (i, 0)),
                  pl.BlockSpec((1, D_MODEL), lambda i, j: (0, 0)),
                  mod_spec, mod_spec,
                  pl.BlockSpec((D_MODEL, tn), lambda i, j: (0, src(j)))],
        out_specs=pl.BlockSpec((tm, tn), lambda i, j: (i, j)),
        scratch_shapes=[pltpu.VMEM((tm, D_MODEL), BF16)],
        compiler_params=_cparams(("arbitrary", "arbitrary")),
        name="inproj",
    )(x2d, g.reshape(1, D_MODEL), sc, sh, w_in)


def _head_norm(x, gain, gmat):
    ss = _group_sum(x * x, gmat)
    return x * lax.rsqrt(ss * (1.0 / HEAD_DIM) + NORM_EPS) * gain


def _attn_prompt_body(q_ref, k_ref, v_ref, qg_ref, kg_ref, o_ref, kc_ref, vc_ref,
                      qn_s, kn_s, m_s, l_s, acc_s):
    t_len = q_ref.shape[0]
    win = kc_ref.shape[0]
    gmat = _group_ones(LANE)
    rows_per = NORM_ROWS

    def prologue(c, carry):
        rows = pl.ds(pl.multiple_of(c * rows_per, rows_per), rows_per)
        qn_s[rows, :] = _head_norm(q_ref[rows, :], qg_ref[...], gmat) * (HEAD_DIM ** -0.5)
        kn_s[rows, :] = _head_norm(k_ref[rows, :], kg_ref[...], gmat)
        m_s[rows, :] = jnp.full((rows_per, LANE), NEG, F32)
        l_s[rows, :] = jnp.zeros((rows_per, LANE), F32)
        acc_s[rows, :] = jnp.zeros((rows_per, LANE), F32)
        return carry

    lax.fori_loop(0, t_len // rows_per, prologue, 0)
    kc_ref[...] = kn_s[t_len - win:, :]
    vc_ref[...] = v_ref[t_len - win:, :]

    qi = lax.broadcasted_iota(jnp.int32, (QBLK, 2 * QBLK), 0)
    kj = lax.broadcasted_iota(jnp.int32, (QBLK, 2 * QBLK), 1)
    causal = jnp.where(kj - QBLK <= qi, 0.0, NEG)
    bias_both = jnp.where(kj < QBLK, jnp.where(kj >= qi, 0.0, NEG), causal)
    bias_first = jnp.where(kj < QBLK, NEG, causal)
    head0 = lax.broadcasted_iota(jnp.int32, (QBLK, LANE), 1) < HEAD_DIM

    for _, dil in DIL_PATTERNS:
        nblk = t_len // dil // QBLK

        def rows_of(start, dil=dil):
            if dil == 1:
                return pl.ds(start, QBLK)
            return pl.ds(start, QBLK, stride=dil)

        def unit(u, dil=dil, nblk=nblk, rows_of=rows_of):
            res = u // nblk
            blk = u - res * nblk
            cur = rows_of(blk * (QBLK * dil) + res)
            prev = rows_of(jnp.maximum(blk - 1, 0) * (QBLK * dil) + res)
            qb = qn_s[cur, :]
            kcat = _bf(jnp.concatenate([kn_s[prev, :], kn_s[cur, :]], axis=0))
            vcat = _bf(jnp.concatenate([v_ref[prev, :], v_ref[cur, :]], axis=0))
            m_old, l_old, acc_old = m_s[cur, :], l_s[cur, :], acc_s[cur, :]
            bias = jnp.where(blk > 0, bias_both, bias_first)
            yield
            stats = []
            for sel in (head0, ~head0):
                qh = _bf(jnp.where(sel, qb, 0.0))
                s = _dot(qh, kcat, _NT) + bias
                yield
                mb = jnp.max(s, axis=-1, keepdims=True)
                p = jnp.exp(s - mb)
                lb = jnp.sum(p, axis=-1, keepdims=True)
                yield
                stats.append((mb, lb, _dot(_bf(p), vcat)))
                yield
            mb = jnp.where(head0, stats[0][0], stats[1][0])
            lb = jnp.where(head0, stats[0][1], stats[1][1])
            ob = jnp.where(head0, stats[0][2], stats[1][2])
            m_new = jnp.maximum(m_old, mb)
            a_old = jnp.exp(m_old - m_new)
            a_blk = jnp.exp(mb - m_new)
            yield
            l_s[cur, :] = l_old * a_old + lb * a_blk
            acc_s[cur, :] = acc_old * a_old + ob * a_blk
            m_s[cur, :] = m_new

        def units(ui, carry, unit=unit):
            _lockstep([unit(ui * ATTN_INTERLEAVE + k) for k in range(ATTN_INTERLEAVE)])
            return carry

        lax.fori_loop(0, dil * nblk // ATTN_INTERLEAVE, units, 0)

    def epilogue(c, carry):
        rows = pl.ds(pl.multiple_of(c * rows_per, rows_per), rows_per)
        o_ref[rows, :] = acc_s[rows, :] / l_s[rows, :]
        return carry

    lax.fori_loop(0, t_len // rows_per, epilogue, 0)


def _attn_prompt(proj3, q_gain, k_gain, win):
    b, t, _ = proj3.shape
    pairs = ATTN_WIDTH // LANE
    qoff, koff, voff = P_PAD // LANE, (P_PAD + ATTN_WIDTH) // LANE, (P_PAD + 2 * ATTN_WIDTH) // LANE
    col = lambda off: pl.BlockSpec((None, t, LANE), lambda bi, hp: (bi, 0, off + hp))
    gain = lambda gvec: jnp.tile(gvec, 2).reshape(1, LANE)
    return pl.pallas_call(
        _attn_prompt_body,
        out_shape=(jax.ShapeDtypeStruct((b, t, ATTN_WIDTH), F32),
                   jax.ShapeDtypeStruct((b, win, ATTN_WIDTH), F32),
                   jax.ShapeDtypeStruct((b, win, ATTN_WIDTH), F32)),
        grid=(b, pairs),
        in_specs=[col(qoff), col(koff), col(voff),
                  pl.BlockSpec((1, LANE), lambda bi, hp: (0, 0)),
                  pl.BlockSpec((1, LANE), lambda bi, hp: (0, 0))],
        out_specs=(pl.BlockSpec((None, t, LANE), lambda bi, hp: (bi, 0, hp)),
                   pl.BlockSpec((None, win, LANE), lambda bi, hp: (bi, 0, hp)),
                   pl.BlockSpec((None, win, LANE), lambda bi, hp: (bi, 0, hp))),
        scratch_shapes=[pltpu.VMEM((t, LANE), F32) for _ in range(5)],
        compiler_params=_cparams(("arbitrary", "arbitrary")),
        name="attn_prompt",
    )(proj3, proj3, proj3, gain(q_gain), gain(k_gain))


def _attn_sample_body(qkv_ref, qg_ref, kg_ref, kt_ref, vt_ref, o_ref):
    def head_norm(x, gain):
        ms = jnp.mean(x * x, axis=1, keepdims=True)
        return x * lax.rsqrt(ms + NORM_EPS) * gain

    qkv = qkv_ref[...]
    qn = head_norm(qkv[:, :, 0:1], qg_ref[...]) * (HEAD_DIM ** -0.5)
    kn = head_norm(qkv[:, :, 1:2], kg_ref[...])
    o_ref[:, :, 1:2] = kn
    length = kt_ref.shape[-1]
    dist = length - lax.broadcasted_iota(jnp.int32, (1, 1, length), 2)
    cnt = jnp.zeros((1, 1, length), F32)
    for window, dil in DIL_PATTERNS:
        cnt = cnt + jnp.where(dist <= window, jnp.where((dist & (dil - 1)) == 0, 1.0, 0.0), 0.0)
    s = jnp.sum(kt_ref[...] * qn, axis=1, keepdims=True)
    s_self = jnp.sum(kn * qn, axis=1, keepdims=True)
    top = jnp.maximum(jnp.max(jnp.where(cnt > 0, s, NEG), axis=-1, keepdims=True), s_self)
    e = jnp.where(cnt > 0, jnp.exp(s - top), 0.0) * cnt
    e_self = float(len(DIL_PATTERNS)) * jnp.exp(s_self - top)
    den = jnp.sum(e, axis=-1, keepdims=True) + e_self
    num = jnp.sum(vt_ref[...] * e, axis=-1, keepdims=True) + e_self * qkv[:, :, 2:3]
    o_ref[:, :, 0:1] = num / den


def _attn_sample(q, k, v, q_gain, k_gain, cache_k, cache_v):
    bs = q.shape[0]
    length = cache_k.shape[1]
    assert length >= max(w for w, _ in DIL_PATTERNS) and all(d & (d - 1) == 0 for _, d in DIL_PATTERNS)
    heads = ATTN_WIDTH // HEAD_DIM
    hb = SAMPLE_HEADS
    qkv = jnp.stack([q, k, v], axis=-1).reshape(bs, heads, HEAD_DIM, 3)
    time_minor = lambda c: jnp.transpose(c, (0, 2, 3, 1))
    in_spec = pl.BlockSpec((None, hb, HEAD_DIM, 3), lambda b, h: (b, h, 0, 0))
    out_spec = pl.BlockSpec((None, hb, HEAD_DIM, 2), lambda b, h: (b, h, 0, 0))
    gain_spec = pl.BlockSpec((HEAD_DIM, 1), lambda b, h: (0, 0))
    cache_spec = pl.BlockSpec((None, hb, HEAD_DIM, length), lambda b, h: (b, h, 0, 0))
    both = pl.pallas_call(
        _attn_sample_body,
        out_shape=jax.ShapeDtypeStruct((bs, heads, HEAD_DIM, 2), F32),
        grid=(bs, heads // hb),
        in_specs=[in_spec, gain_spec, gain_spec, cache_spec, cache_spec],
        out_specs=out_spec,
        compiler_params=_cparams(("arbitrary", "arbitrary")),
        name="attn_sample",
    )(qkv, q_gain.reshape(HEAD_DIM, 1), k_gain.reshape(HEAD_DIM, 1), time_minor(cache_k), time_minor(cache_v))
    return both[..., 0].reshape(bs, ATTN_WIDTH), both[..., 1].reshape(bs, ATTN_WIDTH)


def _rwkv_prep_body(seq_mode, p_ref, prev_ref, mu_ref, w0_ref, a0_ref, kk_ref, ka_ref, rk_ref,
                    wa_hi_ref, wa_lo_ref, g2_hi_ref, g2_lo_ref,
                    r_o, lw_o, kf_o, v_o, kn_o, al_o, g_o, bonus_o):
    p = p_ref[...]
    if seq_mode:
        row0 = lax.broadcasted_iota(jnp.int32, p.shape, 0) == 0
        p_prev = jnp.where(row0, prev_ref[...], pltpu.roll(p, 1, 0))
    else:
        p_prev = prev_ref[...]
    z = p + (p_prev - p) * mu_ref[...]
    r = z[:, 0:RWKV_WIDTH]
    k = z[:, RWKV_WIDTH:2 * RWKV_WIDTH]
    v = z[:, 2 * RWKV_WIDTH:3 * RWKV_WIDTH]
    lo_wa = z[:, 3 * RWKV_WIDTH:3 * RWKV_WIDTH + LANE]
    lo_g = z[:, 3 * RWKV_WIDTH + LANE:3 * RWKV_WIDTH + LORA_PAD]
    decay_lanes = lax.broadcasted_iota(jnp.int32, lo_wa.shape, 1) < DECAY_RANK
    dwa = _mm3r(jnp.where(decay_lanes, jnp.tanh(lo_wa), lo_wa), wa_hi_ref[...], wa_lo_ref[...])
    dw, da = dwa[:, :RWKV_WIDTH], dwa[:, RWKV_WIDTH:]
    g = _mm3r(1.0 / (1.0 + jnp.exp(-lo_g)), g2_hi_ref[...], g2_lo_ref[...])
    u = -(w0_ref[...] + dw)
    softplus = jnp.maximum(u, 0.0) + jnp.log1p(jnp.exp(-jnp.abs(u)))
    lw = -jnp.exp(-softplus - 0.5)
    alpha = 1.0 / (1.0 + jnp.exp(-(a0_ref[...] + da)))
    gmat = _group_ones(LANE)
    kk = k * kk_ref[...]
    nrm = jnp.sqrt(_group_sum(kk * kk, gmat))
    kn = kk / jnp.maximum(nrm, 1e-12)
    kf = k * (1.0 + (alpha - 1.0) * ka_ref[...])
    bonus = _group_sum(r * kf * rk_ref[...], gmat) * v
    lw_o[...] = lw
    for ref, val in ((r_o, r), (kf_o, kf), (v_o, v), (kn_o, kn), (al_o, alpha), (g_o, g), (bonus_o, bonus)):
        ref[...] = val.astype(ref.dtype)


def _rwkv_prep(seq_mode, proj, prev, params, tm, tiles_per_seq):
    n = proj.shape[0]
    mu, w0, a0, k_k, k_a, r_k, wa_hi, wa_lo, g2_hi, g2_lo = params
    p_spec = pl.BlockSpec((tm, P_PAD), lambda i: (i, 0))
    if seq_mode:
        prev_rows = _prev_rows(proj, prev, tm, tiles_per_seq)
        prev_spec = pl.BlockSpec((None, 1, P_PAD), lambda i: (i, 0, 0))
        prev_arg = prev_rows
    else:
        prev_spec = pl.BlockSpec((tm, P_PAD), lambda i: (i, 0))
        prev_arg = prev
    vec = lambda width: pl.BlockSpec((1, width), lambda i: (0, 0))
    wa_spec = pl.BlockSpec((LANE, 2 * RWKV_WIDTH), lambda i: (0, 0))
    g2_spec = pl.BlockSpec((LORA_PAD - LANE, RWKV_WIDTH), lambda i: (0, 0))
    out_spec = pl.BlockSpec((tm, RWKV_WIDTH), lambda i: (i, 0))
    vec_dtype = BF16 if seq_mode else F32
    out = lambda dt: jax.ShapeDtypeStruct((n, RWKV_WIDTH), dt)
    return pl.pallas_call(
        functools.partial(_rwkv_prep_body, seq_mode),
        out_shape=(out(vec_dtype), out(F32)) + (out(vec_dtype),) * 6,
        grid=(n // tm,),
        in_specs=[p_spec, prev_spec, vec(P_PAD)] + [vec(RWKV_WIDTH)] * 5 + [wa_spec, wa_spec, g2_spec, g2_spec],
        out_specs=(out_spec,) * 8,
        compiler_params=_cparams(("arbitrary",)),
        name="rwkv_prep_seq" if seq_mode else "rwkv_prep_row",
    )(proj, prev_arg, mu, w0, a0, k_k, k_a, r_k, wa_hi, wa_lo, g2_hi, g2_lo)


def _prev_rows(proj, shift0, tm, tiles_per_seq):
    n = proj.shape[0]
    tiles = n // tm
    last = proj[tm - 1::tm, :P_PAD][:tiles - 1]
    rows = jnp.concatenate([jnp.zeros((1, P_PAD), F32), last], axis=0).reshape(tiles // tiles_per_seq, tiles_per_seq, P_PAD)
    rows = rows.at[:, 0, :].set(shift0[:, 0, :])
    return rows.reshape(tiles, 1, P_PAD)


def _rwkv_scan_body(r_ref, lw_ref, kf_ref, v_ref, kn_ref, al_ref, g_ref, bonus_ref, lnw_ref, lnb_ref, h0_ref,
                    y_ref, hT_ref, h_s):
    c = CHUNK
    gc = RWKV_GROUP * c
    n = 2 * gc
    tt = r_ref.shape[0]
    n_inst = r_ref.shape[1] // LANE
    ti = pl.program_id(2)

    @pl.when(ti == 0)
    def _():
        h_s[...] = h0_ref[...]

    head0 = lax.broadcasted_iota(jnp.int32, (c, LANE), 1) < HEAD_DIM
    ri = lax.broadcasted_iota(jnp.int32, (n, n), 0)
    ci = lax.broadcasted_iota(jnp.int32, (n, n), 1)
    same_block = (ri // c) == (ci // c)
    strict = same_block & (ci < ri)
    incl = same_block & (ci <= ri)
    eye = ri == ci
    eye_l = lax.broadcasted_iota(jnp.int32, (LANE, LANE), 0) == lax.broadcasted_iota(jnp.int32, (LANE, LANE), 1)
    tr = lax.broadcasted_iota(jnp.int32, (gc, gc), 0)
    tc = lax.broadcasted_iota(jnp.int32, (gc, gc), 1)
    tri = jnp.where(((tr // c) == (tc // c)) & (tc <= tr), 1.0, 0.0).astype(BF16)
    gmat = _group_ones(LANE)

    def stack(x):
        parts = []
        for k in range(RWKV_GROUP):
            xk = x[k * c:(k + 1) * c, :]
            parts += [jnp.where(head0, xk, 0.0), jnp.where(head0, 0.0, xk)]
        return jnp.concatenate(parts, axis=0)

    def twice(x):
        parts = []
        for k in range(RWKV_GROUP):
            xk = x[k * c:(k + 1) * c, :]
            parts += [xk, xk]
        return jnp.concatenate(parts, axis=0)

    def solve(gi, j):
        rows = pl.ds(pl.multiple_of(gi * gc, gc), gc)
        ln = slice(j * LANE, (j + 1) * LANE)
        lw = lw_ref[rows, ln]
        r, kf, v, kn, al = (ref[rows, ln].astype(F32) for ref in (r_ref, kf_ref, v_ref, kn_ref, al_ref))
        gcum = _mm2r(tri, lw)
        g_end = jnp.concatenate([jnp.broadcast_to(gcum[(k + 1) * c - 1:(k + 1) * c, :], (c, LANE))
                                 for k in range(RWKV_GROUP)], axis=0)
        at = -kn * jnp.exp(gcum - lw)
        rt = r * jnp.exp(gcum)
        inv = jnp.exp(-gcum)
        bt = kn * al * inv
        kt = kf * inv
        dec_end = jnp.exp(g_end)
        at_st, rt_st, v_st = stack(at), stack(rt), stack(v)
        b2_st, k2_st = stack(bt * dec_end), stack(kt * dec_end)
        big = _mm(jnp.concatenate([at_st, rt_st], axis=0), jnp.concatenate([twice(bt), twice(kt)], axis=0), _NT)
        a_ab = jnp.where(strict, big[0:n, 0:n], 0.0)
        a_ak = jnp.where(strict, big[0:n, n:2 * n], 0.0)
        a_rb = jnp.where(incl, big[n:2 * n, 0:n], 0.0)
        a_rk = jnp.where(incl, big[n:2 * n, n:2 * n], 0.0)
        x = jnp.where(eye, 1.0, 0.0) + a_ab
        yield
        pw = _mm(a_ab, a_ab)
        yield
        steps = int(np.log2(c)) - 1
        for s in range(steps):
            if s < steps - 1:
                both = _mm(pw, jnp.concatenate([pw, x], axis=1))
                x = x + both[:, n:]
                pw = both[:, :n]
            else:
                x = x + _mm(pw, x)
            yield
        av_st = _mm(a_ak, v_st)
        yield
        wu_uv = _mm(x, jnp.concatenate([at_st, av_st], axis=1))
        wu_st, uv_st = wu_uv[:, :LANE], wu_uv[:, LANE:]
        yield
        rb = _mm(a_rb, wu_uv)
        r2_st = rt_st + rb[:, :LANE]
        yv_st = rb[:, LANE:] + _mm(a_rk, v_st)
        yield
        trans, add = [], []
        for k in range(RWKV_GROUP):
            blk = slice(2 * k * c, 2 * (k + 1) * c)
            trans.append(jnp.where(eye_l, dec_end[k * c:k * c + 1, :], 0.0) + _mm(b2_st[blk], wu_st[blk], _TN))
            add.append(_mm(jnp.concatenate([b2_st[blk], k2_st[blk]], axis=0),
                           jnp.concatenate([uv_st[blk], v_st[blk]], axis=0), _TN))
        yield
        hmat = h_s[j]
        ys = []
        for k in range(RWKV_GROUP):
            blk = slice(2 * k * c, 2 * (k + 1) * c)
            y_st = _mm(r2_st[blk], hmat) + yv_st[blk]
            hmat = _mm(trans[k], hmat) + add[k]
            ys.append(y_st[:c, :] + y_st[c:, :])
            yield
        h_s[j] = hmat
        y = ys[0] if len(ys) == 1 else jnp.concatenate(ys, axis=0)
        mean = _group_sum(y, gmat) * (1.0 / HEAD_DIM)
        dev = y - mean
        var = _group_sum(dev * dev, gmat) * (1.0 / HEAD_DIM)
        yn = dev * lax.rsqrt(var + GN_EPS) * lnw_ref[:, ln] + lnb_ref[:, ln]
        y_ref[rows, ln] = (yn + bonus_ref[rows, ln].astype(F32)) * g_ref[rows, ln].astype(F32)

    def group(gi, carry):
        _lockstep([solve(gi, j) for j in range(n_inst)])
        return carry

    lax.fori_loop(0, tt // gc, group, 0)

    @pl.when(ti == pl.num_programs(2) - 1)
    def _():
        hT_ref[...] = h_s[...]


def _rwkv_scan(vecs, lnw, lnb, h0, b, t):
    pairs = RWKV_WIDTH // LANE
    tt = min(t, SCAN_ROWS)
    nt = t // tt
    ni = RWKV_INTERLEAVE
    seq = pl.BlockSpec((tt, ni * LANE), lambda bi, hp, ti: (bi * nt + ti, hp))
    vec = pl.BlockSpec((1, ni * LANE), lambda bi, hp, ti: (0, hp))
    st = pl.BlockSpec((None, ni, LANE, LANE), lambda bi, hp, ti: (bi, hp, 0, 0))
    return pl.pallas_call(
        _rwkv_scan_body,
        out_shape=(jax.ShapeDtypeStruct((b * t, RWKV_WIDTH), F32),
                   jax.ShapeDtypeStruct((b, pairs, LANE, LANE), F32)),
        grid=(b, pairs // ni, nt),
        in_specs=[seq] * 8 + [vec, vec, st],
        out_specs=(seq, st),
        scratch_shapes=[pltpu.VMEM((ni, LANE, LANE), F32)],
        compiler_params=_cparams(("arbitrary", "arbitrary", "arbitrary")),
        name="rwkv_scan",
    )(*vecs, lnw.reshape(1, RWKV_WIDTH), lnb.reshape(1, RWKV_WIDTH), h0)


def _rwkv_step_body(s_ref, lw_ref, kn_ref, al_ref, k_ref, r_ref, vgb_ref, lnw_ref, lnb_ref, y_ref, so_ref):
    s = s_ref[...]
    kn = kn_ref[...]
    sa = jnp.sum(s * (-kn), axis=-1, keepdims=True)
    vgb = vgb_ref[...]
    s_new = s * jnp.exp(lw_ref[...]) + sa * (kn * al_ref[...]) + vgb[:, :, 0:1] * k_ref[...]
    so_ref[...] = s_new
    y = jnp.sum(s_new * r_ref[...], axis=-1, keepdims=True)
    mean = jnp.mean(y, axis=1, keepdims=True)
    dev = y - mean
    var = jnp.mean(dev * dev, axis=1, keepdims=True)
    yn = dev * lax.rsqrt(var + GN_EPS) * lnw_ref[...] + lnb_ref[...]
    y_ref[...] = (yn + vgb[:, :, 2:3]) * vgb[:, :, 1:2]


def _rwkv_step(state, vecs, lnw, lnb):
    bs = state.shape[0]
    r, lw, kf, v, kn, al, g, bonus = vecs
    rowv = lambda a: a.reshape(bs, RWKV_HEADS, 1, HEAD_DIM)
    vgb = jnp.stack([v, g, bonus], axis=-1).reshape(bs, RWKV_HEADS, HEAD_DIM, 3)
    row_spec = pl.BlockSpec((None, RWKV_HEADS, 1, HEAD_DIM), lambda b: (b, 0, 0, 0))
    col_spec = pl.BlockSpec((None, RWKV_HEADS, HEAD_DIM, 1), lambda b: (b, 0, 0, 0))
    vgb_spec = pl.BlockSpec((None, RWKV_HEADS, HEAD_DIM, 3), lambda b: (b, 0, 0, 0))
    st_spec = pl.BlockSpec((None, RWKV_HEADS, HEAD_DIM, HEAD_DIM), lambda b: (b, 0, 0, 0))
    par_spec = pl.BlockSpec((RWKV_HEADS, HEAD_DIM, 1), lambda b: (0, 0, 0))
    y, s_new = pl.pallas_call(
        _rwkv_step_body,
        out_shape=(jax.ShapeDtypeStruct((bs, RWKV_HEADS, HEAD_DIM, 1), F32),
                   jax.ShapeDtypeStruct(state.shape, F32)),
        grid=(bs,),
        in_specs=[st_spec] + [row_spec] * 5 + [vgb_spec] + [par_spec] * 2,
        out_specs=(col_spec, st_spec),
        compiler_params=_cparams(("arbitrary",)),
        name="rwkv_step",
    )(state, rowv(lw), rowv(kn), rowv(al), rowv(kf), rowv(r), vgb,
      lnw.reshape(RWKV_HEADS, HEAD_DIM, 1), lnb.reshape(RWKV_HEADS, HEAD_DIM, 1))
    return y.reshape(bs, RWKV_WIDTH), s_new


def _outproj_body(a_ref, r_ref, x_ref, gt_ref, w_ref, g2_ref, sc_ref, sh_ref, x1_ref, h2_ref):
    y = _dot(_bf(a_ref[...]), w_ref[0:ATTN_WIDTH, :]) + _dot(_bf(r_ref[...]), w_ref[ATTN_WIDTH:, :])
    x1 = x_ref[...] + gt_ref[...] * y
    x1_ref[...] = x1
    h2_ref[...] = _modulated_norm(x1, g2_ref[...], sc_ref[...], sh_ref[...])


def _outproj(attn, rw, x2d, gt, w_bf, g2, sc, sh, tm, tiles_per_group):
    n = x2d.shape[0]
    r = gt.shape[1]
    half = pl.BlockSpec((tm, ATTN_WIDTH), lambda i: (i, 0))
    full = pl.BlockSpec((tm, D_MODEL), lambda i: (i, 0))
    mod = pl.BlockSpec((None, r, D_MODEL), lambda i: (i // tiles_per_group, 0, 0))
    return pl.pallas_call(
        _outproj_body,
        out_shape=(jax.ShapeDtypeStruct((n, D_MODEL), F32), jax.ShapeDtypeStruct((n, D_MODEL), F32)),
        grid=(n // tm,),
        in_specs=[half, half, full, mod, pl.BlockSpec((D_MODEL, D_MODEL), lambda i: (0, 0)),
                  pl.BlockSpec((1, D_MODEL), lambda i: (0, 0)), mod, mod],
        out_specs=(full, full),
        compiler_params=_cparams(("arbitrary",)),
        name="outproj",
    )(attn, rw, x2d, gt, w_bf, g2.reshape(1, D_MODEL), sc, sh)


def _staircase():
    pairs = [(a, b) for a in range(PEER_TOPK) for b in range(PEER_TOPK) if (a + 1) * (b + 1) <= PEER_TOPK]
    return pairs


_CAND = _staircase()
_CAND_ROWS = 56


_SENTINEL = 2.0 ** 100


def _extract_top(s, count, rows, tie_safe):
    vals = []
    limit = float(s.shape[0])
    for kth in range(count):
        mx = jnp.max(s, axis=0, keepdims=True)
        if tie_safe:
            hit = rows == jnp.min(jnp.where(s == mx, rows, limit), axis=0, keepdims=True)
        else:
            hit = s == mx
        s = jnp.where(hit, _rank_mark(kth), s)
        vals.append(mx)
        yield
    return vals, s


def _rank_mark(kth):
    return -_SENTINEL * (1.0 + kth / 16.0)


def _was_taken(s):
    return s < -0.5 * _SENTINEL


def _decode_rank(s):
    return jnp.where(_was_taken(s), (s * (-1.0 / _SENTINEL) - 1.0) * 16.0, 99.0)


def _router_body(h2_ref, wq_ref, keys_ref, oh_ref, h2t_ref, cnt_ref, e1_ref, rk_ref, e2_ref, q_s):
    h2 = h2_ref[...]
    tm = h2.shape[0]
    h2t_ref[...] = _bf(h2.T)
    q_s[...] = _dot(wq_ref[...], _bf(h2), _NT)
    rows = lax.broadcasted_iota(jnp.int32, (PEER_KEYS, LANE), 0).astype(F32)
    crow = lax.broadcasted_iota(jnp.int32, (_CAND_ROWS, LANE), 0).astype(F32)

    def route(h, tie_safe):
        scores = []
        for half in range(2):
            g = 2 * h + half
            q_t = q_s[pl.ds(pl.multiple_of(g * PEER_KEYS, PEER_KEYS), PEER_KEYS), :]
            scores.append(_mm3(keys_ref[g], q_t))
        wrong = jnp.zeros((1, LANE), F32)
        for sub in range(tm // LANE):
            sl = slice(sub * LANE, (sub + 1) * LANE)
            s0, s1 = scores[0][:, sl], scores[1][:, sl]
            (tops0, fin0), (tops1, fin1) = yield from _lockstep_gen(
                [_extract_top(s0, PEER_TOPK, rows, tie_safe), _extract_top(s1, PEER_TOPK, rows, tie_safe)])
            cand = [tops0[a] + tops1[b] for a, b in _CAND]
            cand += [jnp.full((1, LANE), NEG, F32)] * (_CAND_ROWS - len(cand))
            cand = jnp.concatenate(cand, axis=0)
            top = tops0[0] + tops1[0]
            taken = jnp.zeros((_CAND_ROWS, LANE), F32)
            zsum = jnp.zeros((1, LANE), F32)
            for _ in range(PEER_TOPK):
                mx = jnp.max(cand, axis=0, keepdims=True)
                if tie_safe:
                    hit = crow == jnp.min(jnp.where(cand == mx, crow, float(_CAND_ROWS)), axis=0, keepdims=True)
                else:
                    hit = cand == mx
                taken = jnp.where(hit, 1.0, taken)
                cand = jnp.where(hit, NEG, cand)
                zsum = zsum + jnp.exp(mx - top)
                yield
            per_rank = _dot(oh_ref[...], _bf(taken))
            cnt = jnp.full((PEER_KEYS, LANE), -0.5, F32)
            for kth in range(PEER_TOPK):
                cnt = jnp.where(fin0 == _rank_mark(kth), per_rank[kth:kth + 1, :] - 0.5, cnt)
            in0, in1 = _was_taken(fin0), _was_taken(fin1)
            cnt_ref[h, :, sl] = cnt
            rk_ref[h, :, sl] = _bf(_decode_rank(fin1))
            e1_ref[h, :, sl] = jnp.where(in0, jnp.exp(s0 - tops0[0]), 0.0)
            e2_ref[h, :, sl] = _bf(jnp.where(in1, jnp.exp(s1 - tops1[0]) / zsum, 0.0))
            for mask in (jnp.where(in0, 1.0, 0.0), jnp.where(in1, 1.0, 0.0), taken):
                wrong = wrong + jnp.abs(jnp.sum(mask, axis=0, keepdims=True) - float(PEER_TOPK))
        return wrong

    def heads(tie_safe):
        def body(hi, wrong):
            res = _lockstep([route(hi * ROUTER_INTERLEAVE + k, tie_safe) for k in range(ROUTER_INTERLEAVE)])
            return wrong + sum(res)
        return lax.fori_loop(0, PEER_HEADS // ROUTER_INTERLEAVE, body, jnp.zeros((1, LANE), F32))

    wrong = heads(False)

    @pl.when(jnp.max(wrong) > 0.0)
    def _():
        heads(True)


def _router(h2, wq_t_bf, keys, tm):
    n = h2.shape[0]
    onehot = np.zeros((PEER_TOPK, _CAND_ROWS), np.float32)
    for idx, (a, _) in enumerate(_CAND):
        onehot[a, idx] = 1.0
    tok = pl.BlockSpec((None, PEER_HEADS, PEER_KEYS, tm), lambda i: (i, 0, 0, 0))
    tok_shape = lambda dt: jax.ShapeDtypeStruct((n // tm, PEER_HEADS, PEER_KEYS, tm), dt)
    return pl.pallas_call(
        _router_body,
        out_shape=(jax.ShapeDtypeStruct((D_MODEL, n), BF16),
                   tok_shape(F32), tok_shape(F32), tok_shape(BF16), tok_shape(BF16)),
        grid=(n // tm,),
        in_specs=[pl.BlockSpec((tm, D_MODEL), lambda i: (i, 0)),
                  pl.BlockSpec((D_MODEL, D_MODEL), lambda i: (0, 0)),
                  pl.BlockSpec((2 * PEER_HEADS, PEER_KEYS, PEER_KEYS), lambda i: (0, 0, 0)),
                  pl.BlockSpec((PEER_TOPK, _CAND_ROWS), lambda i: (0, 0))],
        out_specs=(pl.BlockSpec((D_MODEL, tm), lambda i: (0, i)), tok, tok, tok, tok),
        scratch_shapes=[pltpu.VMEM((D_MODEL, tm), F32)],
        compiler_params=_cparams(("arbitrary",)),
        name="peer_router",
    )(h2, wq_t_bf, keys, jnp.asarray(onehot, BF16))


def _experts_body(h2t_ref, u_ref, v_ref, cnt_ref, e1_ref, rk_ref, e2_ref, x1_ref, gt_ref, y_ref, p_s):
    j = pl.program_id(1)
    last = pl.num_programs(1) - 1
    eb = u_ref.shape[0]
    per = eb // PEER_KEYS
    cur = j % 2

    @pl.when(j == 0)
    def _():
        y_ref[...] = jnp.zeros(y_ref.shape, F32)
        p_s[1] = jnp.zeros(p_s.shape[1:], BF16)

    @pl.when(j < last)
    def _():
        y_ref[...] += _dot(p_s[1 - cur], v_ref[...], _TN)
        h2t = h2t_ref[...]
        for ii in range(per):
            sub = slice(ii * PEER_KEYS, (ii + 1) * PEER_KEYS)
            i = j * per + ii
            gates = []
            for part in range(cnt_ref.shape[0]):
                wt = None
                for h in range(PEER_HEADS):
                    partners = _bf(cnt_ref[part, h, pl.ds(i, 1), :])
                    e1 = _bf(e1_ref[part, h, pl.ds(i, 1), :])
                    term = jnp.where(rk_ref[part, h] < partners, e2_ref[part, h], 0.0) * e1
                    wt = term if wt is None else wt + term
                gates.append(wt)
            wt = gates[0] if len(gates) == 1 else jnp.concatenate(gates, axis=1)
            act_in = _dot(u_ref[sub, :], h2t)
            act = 0.5 * act_in * (1.0 + lax.erf(act_in * (2.0 ** -0.5)))
            p_s[cur, sub, :] = wt * _bf(act)

    @pl.when(j == last)
    def _():
        acc = y_ref[...] + _dot(p_s[1 - cur], v_ref[...], _TN)
        y_ref[...] = x1_ref[...] + gt_ref[...] * acc


def _experts(h2t, u_bf, v_bf, route, x1, gt, tm, tiles_per_group):
    n = x1.shape[0]
    eb = EXPERT_BLOCK
    nblk = PEER_EXPERTS // eb
    r = gt.shape[1]
    slab = route[0].shape[-1]
    tok = pl.BlockSpec((tm // slab, PEER_HEADS, PEER_KEYS, slab), lambda i, j: (i, 0, 0, 0))
    return pl.pallas_call(
        _experts_body,
        out_shape=jax.ShapeDtypeStruct((n, D_MODEL), F32),
        grid=(n // tm, nblk + 1),
        in_specs=[pl.BlockSpec((D_MODEL, tm), lambda i, j: (0, i)),
                  pl.BlockSpec((eb, D_MODEL), lambda i, j: (jnp.minimum(j, nblk - 1), 0)),
                  pl.BlockSpec((eb, D_MODEL), lambda i, j: (jnp.maximum(j - 1, 0), 0)),
                  tok, tok, tok, tok,
                  pl.BlockSpec((tm, D_MODEL), lambda i, j: (i, 0)),
                  pl.BlockSpec((None, r, D_MODEL), lambda i, j: (i // tiles_per_group, 0, 0))],
        out_specs=pl.BlockSpec((tm, D_MODEL), lambda i, j: (i, 0)),
        scratch_shapes=[pltpu.VMEM((2, eb, tm), BF16)],
        compiler_params=_cparams(("arbitrary", "arbitrary")),
        name="peer_experts",
    )(h2t, u_bf, v_bf, *route, x1, gt)


def _pad_rows(a, rows):
    return jnp.pad(a, ((0, rows - a.shape[0]),) + ((0, 0),) * (a.ndim - 1))


def _state_to_pairs(state):
    b = state.shape[0]
    h = jnp.swapaxes(state, -1, -2).reshape(b, RWKV_HEADS // 2, 2, HEAD_DIM, HEAD_DIM)
    z = jnp.zeros_like(h[:, :, 0])
    top = jnp.concatenate([h[:, :, 0], z], axis=-1)
    bot = jnp.concatenate([z, h[:, :, 1]], axis=-1)
    return jnp.concatenate([top, bot], axis=-2)


def _pairs_to_state(hp):
    b = hp.shape[0]
    h0 = hp[:, :, :HEAD_DIM, :HEAD_DIM]
    h1 = hp[:, :, HEAD_DIM:, HEAD_DIM:]
    h = jnp.stack([h0, h1], axis=2).reshape(b, RWKV_HEADS, HEAD_DIM, HEAD_DIM)
    return jnp.swapaxes(h, -1, -2)


def kernel(x_prompt, x_sample, cache_attn_k, cache_attn_v, state_rwkv, state_rwkv_shift, c_prompt, c_sample, w_ada, b_ada, norm1_g, norm2_g, w_in, q_gain, k_gain, rwkv_mu, rwkv_w0, rwkv_w2, rwkv_a0, rwkv_a2, rwkv_g2, rwkv_k_k, rwkv_k_a, rwkv_r_k, rwkv_lnx_w, rwkv_lnx_b, w_out, peer_w_q, peer_sub_keys, peer_u, peer_v):
    b, t, _ = x_prompt.shape
    bs = x_sample.shape[0]
    n = b * t
    heads = ATTN_WIDTH // HEAD_DIM
    win = min(CACHE_LEN, t)

    w_out_bf = w_out.astype(BF16)
    wq_t_bf = peer_w_q.T.astype(BF16)
    keys = peer_sub_keys.reshape(2 * PEER_HEADS, PEER_KEYS, PEER_KEYS)
    u_bf = peer_u.astype(BF16)
    v_bf = peer_v.astype(BF16)
    row = lambda a: a.reshape(1, -1)
    wa = jnp.zeros((LANE, 2 * RWKV_WIDTH), F32)
    wa = wa.at[:DECAY_RANK, :RWKV_WIDTH].set(rwkv_w2).at[DECAY_RANK:DECAY_RANK + A_RANK, RWKV_WIDTH:].set(rwkv_a2)
    g2 = jnp.zeros((LORA_PAD - LANE, RWKV_WIDTH), F32).at[:GATE_RANK].set(rwkv_g2)
    hi_lo = lambda w: (w.astype(BF16), (w - w.astype(BF16).astype(F32)).astype(BF16))
    rwkv_params = (row(jnp.pad(rwkv_mu, (0, P_PAD - RWKV_PROJ))), row(rwkv_w0), row(rwkv_a0), row(rwkv_k_k),
                   row(rwkv_k_a), row(rwkv_r_k), *hi_lo(wa), *hi_lo(g2))

    c_rows = b + bs
    c_pad = -(-c_rows // 8) * 8
    mod = _adaln(_pad_rows(jnp.concatenate([c_prompt, c_sample], axis=0), c_pad), w_ada, b_ada)
    mod_p = [m.reshape(b, 1, D_MODEL) for m in jnp.split(mod[:b], 6, axis=-1)]
    mod_s = [m.reshape(1, bs, D_MODEL) for m in jnp.split(mod[b:c_rows], 6, axis=-1)]

    tm_p = min(TM_INPROJ, t)
    xp = x_prompt.reshape(n, D_MODEL)
    proj_p = _inproj(xp, norm1_g, mod_p[1], mod_p[0], w_in, tm_p, t // tm_p)
    attn_p, k_cache, v_cache = _attn_prompt(proj_p.reshape(b, t, PROJ_PAD), q_gain, k_gain, win)
    tm_r = min(TM_PREP, t)
    shift0 = jnp.zeros((b, 1, P_PAD), F32)
    vecs_p = _rwkv_prep(True, proj_p, shift0, rwkv_params, tm_r, t // tm_r)
    rw_p, h_fin = _rwkv_scan(vecs_p, rwkv_lnx_w, rwkv_lnx_b, jnp.zeros((b, RWKV_HEADS // 2, LANE, LANE), F32), b, t)
    tm_o = min(TM_OUTPROJ, t)
    x1_p, h2_p = _outproj(attn_p.reshape(n, ATTN_WIDTH), rw_p, xp, mod_p[2], w_out_bf, norm2_g,
                          mod_p[4], mod_p[3], tm_o, t // tm_o)
    tm_q = min(TM_ROUTE, t)
    tm_e = min(TM_EXPERT, t)
    h2t_p, *route_p = _router(h2_p, wq_t_bf, keys, tm_q)
    y_p = _experts(h2t_p, u_bf, v_bf, route_p, x1_p, mod_p[5], tm_e, t // tm_e)

    xs = x_sample.reshape(bs, D_MODEL)
    proj_s = _inproj(xs, norm1_g, mod_s[1], mod_s[0], w_in, bs, 1)
    q_s = proj_s[:, P_PAD:P_PAD + ATTN_WIDTH]
    k_s = proj_s[:, P_PAD + ATTN_WIDTH:P_PAD + 2 * ATTN_WIDTH]
    v_s = proj_s[:, P_PAD + 2 * ATTN_WIDTH:]
    attn_s, kn_s = _attn_sample(q_s, k_s, v_s, q_gain, k_gain, cache_attn_k, cache_attn_v)
    prev_s = jnp.pad(state_rwkv_shift.reshape(bs, RWKV_PROJ), ((0, 0), (0, P_PAD - RWKV_PROJ)))
    vecs_s = _rwkv_prep(False, proj_s, prev_s, rwkv_params, bs, 1)
    rw_s, state_s = _rwkv_step(state_rwkv, vecs_s, rwkv_lnx_w, rwkv_lnx_b)
    x1_s, h2_s = _outproj(attn_s, rw_s, xs, mod_s[2], w_out_bf, norm2_g, mod_s[4], mod_s[3], bs, 1)
    ns = -(-bs // LANE) * LANE
    h2t_s, *route_s = _router(_pad_rows(h2_s, ns), wq_t_bf, keys, LANE)
    gt2_s = _pad_rows(mod_s[5][0], ns).reshape(ns // LANE, LANE, D_MODEL)
    y_s = _experts(h2t_s, u_bf, v_bf, route_s, _pad_rows(x1_s, ns), gt2_s, LANE, 1)[:bs]

    return (y_p.reshape(b, t, D_MODEL), y_s.reshape(bs, 1, D_MODEL),
            k_cache.reshape(b, win, heads, HEAD_DIM), v_cache.reshape(b, win, heads, HEAD_DIM),
            _pairs_to_state(h_fin), proj_p.reshape(b, t, PROJ_PAD)[:, t - 1:, :RWKV_PROJ],
            kn_s.reshape(bs, 1, heads, HEAD_DIM), v_s.reshape(bs, 1, heads, HEAD_DIM),
            state_s, proj_s[:, :RWKV_PROJ].reshape(bs, 1, RWKV_PROJ))
```

```python
import functools

import numpy as np
import jax
import jax.numpy as jnp
from jax import lax
from jax.experimental import pallas as pl
from jax.experimental.pallas import tpu as pltpu

F32 = jnp.float32
BF16 = jnp.bfloat16

D_MODEL = 2048
HEAD_DIM = 64
ATTN_WIDTH = 1024
RWKV_WIDTH = 1024
RWKV_HEADS = RWKV_WIDTH // HEAD_DIM
DECAY_RANK = 64
A_RANK = 64
GATE_RANK = 160
RWKV_PROJ = 3 * RWKV_WIDTH + DECAY_RANK + A_RANK + GATE_RANK
P_PAD = 3584
PROJ_PAD = P_PAD + 3 * ATTN_WIDTH
LORA_PAD = 384
DIL_PATTERNS = ((128, 1), (512, 4), (2048, 16))
CACHE_LEN = 2048
NORM_EPS = 1e-6
GN_EPS = HEAD_DIM * 1e-5
PEER_HEADS = 8
PEER_KEYS = 128
PEER_TOPK = 16
PEER_EXPERTS = PEER_KEYS * PEER_KEYS
NEG = -1e30

LANE = 128
QBLK = 128
CHUNK = 64
RWKV_GROUP = 1
RWKV_INTERLEAVE = 8
SCAN_AHEAD = 2
ROUTER_INTERLEAVE = 4
ATTN_INTERLEAVE = 8
VMEM_LIMIT = 56 * 1024 * 1024

TM_INPROJ = 1024
TM_PREP = 256
TM_OUTPROJ = 256
TM_ROUTE = 256
TM_EXPERT = 512
EXPERT_BLOCK = 1024
SCAN_ROWS = 256
ADALN_COLS = 1024
NORM_ROWS = 512
SAMPLE_HEADS = 8


def _cparams(sem):
    return pltpu.CompilerParams(dimension_semantics=sem, vmem_limit_bytes=VMEM_LIMIT)


_NN = (((1,), (0,)), ((), ()))
_NT = (((1,), (1,)), ((), ()))
_TN = (((0,), (0,)), ((), ()))


def _bf(x):
    return x.astype(BF16)


def _dot(a, b, dims=_NN):
    return lax.dot_general(a, b, dims, preferred_element_type=F32)


def _mm(a, b, dims=_NN):
    return _dot(_bf(a), _bf(b), dims)


def _split(x):
    hi = _bf(x)
    lo = _bf(x - hi.astype(F32))
    return hi, lo


def _mm3(a, b, dims=_NN):
    ah, al = _split(a)
    bh, bl = _split(b)
    return _dot(ah, bh, dims) + (_dot(ah, bl, dims) + _dot(al, bh, dims))


def _mm3r(a, b_hi, b_lo, dims=_NN):
    ah, al = _split(a)
    return _dot(ah, b_hi, dims) + (_dot(ah, b_lo, dims) + _dot(al, b_hi, dims))


def _mm2l(a, b_exact, dims=_NN):
    ah, al = _split(a)
    return _dot(ah, b_exact, dims) + _dot(al, b_exact, dims)


def _mm2r(a_exact, b, dims=_NN):
    bh, bl = _split(b)
    return _dot(a_exact, bh, dims) + _dot(a_exact, bl, dims)


def _group_ones(n):
    r = lax.broadcasted_iota(jnp.int32, (n, n), 0) // HEAD_DIM
    c = lax.broadcasted_iota(jnp.int32, (n, n), 1) // HEAD_DIM
    return jnp.where(r == c, 1.0, 0.0).astype(BF16)


def _group_sum(x, g):
    cols = x.shape[1] // LANE
    parts = [_mm2l(x[:, c * LANE:(c + 1) * LANE], g) for c in range(cols)]
    return parts[0] if cols == 1 else jnp.concatenate(parts, axis=1)


def _lockstep_gen(gens):
    results = [None] * len(gens)
    live = list(enumerate(gens))
    while live:
        nxt = []
        for idx, gen in live:
            try:
                next(gen)
                nxt.append((idx, gen))
            except StopIteration as stop:
                results[idx] = stop.value
        live = nxt
        if live:
            yield
    return results


def _lockstep(gens):
    runner = _lockstep_gen(gens)
    while True:
        try:
            next(runner)
        except StopIteration as stop:
            return stop.value


def _adaln_body(c_ref, w_ref, b_ref, o_ref):
    c = c_ref[...]
    s = c * (1.0 / (1.0 + jnp.exp(-c)))
    o_ref[...] = _mm3(s, w_ref[...]) + b_ref[...]


def _adaln(c, w_ada, b_ada):
    rows = c.shape[0]
    n = w_ada.shape[1]
    tn = ADALN_COLS
    return pl.pallas_call(
        _adaln_body,
        out_shape=jax.ShapeDtypeStruct((rows, n), F32),
        grid=(n // tn,),
        in_specs=[pl.BlockSpec((rows, D_MODEL), lambda j: (0, 0)),
                  pl.BlockSpec((D_MODEL, tn), lambda j: (0, j)),
                  pl.BlockSpec((1, tn), lambda j: (0, j))],
        out_specs=pl.BlockSpec((rows, tn), lambda j: (0, j)),
        compiler_params=_cparams(("arbitrary",)),
        name="adaln",
    )(c, w_ada, b_ada.reshape(1, n))


def _modulated_norm(x, g, sc, sh):
    ms = jnp.mean(x * x, axis=-1, keepdims=True)
    return (x * lax.rsqrt(ms + NORM_EPS) * g) * (1.0 + sc) + sh


def _inproj_body(x_ref, g_ref, sc_ref, sh_ref, w_ref, o_ref, h_s):
    j = pl.program_id(1)

    @pl.when(j == 0)
    def _():
        h_s[...] = _bf(_modulated_norm(x_ref[...], g_ref[...], sc_ref[...], sh_ref[...]))

    tn = w_ref.shape[1]
    valid = jnp.where(j == _P_TILES - 1, RWKV_PROJ - (_P_TILES - 1) * tn, tn)
    cols = lax.broadcasted_iota(jnp.int32, w_ref.shape, 1)
    o_ref[...] = _dot(h_s[...], _bf(jnp.where(cols < valid, w_ref[...], 0.0)))


_IN_TN = 512
_P_TILES = P_PAD // _IN_TN


def _inproj(x2d, g, sc, sh, w_in, tm, tiles_per_group):
    n = x2d.shape[0]
    r = sc.shape[1]
    tn = _IN_TN
    qkv_tiles = 3 * ATTN_WIDTH // tn
    src = lambda j: jnp.where(j < _P_TILES, j + qkv_tiles, j - _P_TILES)
    mod_spec = pl.BlockSpec((None, r, D_MODEL), lambda i, j: (i // tiles_per_group, 0, 0))
    return pl.pallas_call(
        _inproj_body,
        out_shape=jax.ShapeDtypeStruct((n, PROJ_PAD), F32),
        grid=(n // tm, PROJ_PAD // tn),
        in_specs=[pl.BlockSpec((tm, D_MODEL), lambda i, j: (i, 0)),
                  pl.BlockSpec((1, D_MODEL), lambda i, j: (0, 0)),
                  mod_spec, mod_spec,
                  pl.BlockSpec((D_MODEL, tn), lambda i, j: (0, src(j)))],
        out_specs=pl.BlockSpec((tm, tn), lambda i, j: (i, j)),
        scratch_shapes=[pltpu.VMEM((tm, D_MODEL), BF16)],
        compiler_params=_cparams(("arbitrary", "arbitrary")),
        name="inproj",
    )(x2d, g.reshape(1, D_MODEL), sc, sh, w_in)


def _head_norm(x, gain, gmat):
    ss = _group_sum(x * x, gmat)
    return x * lax.rsqrt(ss * (1.0 / HEAD_DIM) + NORM_EPS) * gain


def _attn_prompt_body(q_ref, k_ref, v_ref, qg_ref, kg_ref, o_ref, kc_ref, vc_ref,
                      qn_s, kn_s, m_s, l_s, acc_s):
    t_len = q_ref.shape[0]
    win = kc_ref.shape[0]
    gmat = _group_ones(LANE)
    rows_per = NORM_ROWS

    def prologue(c, carry):
        rows = pl.ds(pl.multiple_of(c * rows_per, rows_per), rows_per)
        qn_s[rows, :] = _head_norm(q_ref[rows, :], qg_ref[...], gmat) * (HEAD_DIM ** -0.5)
        kn_s[rows, :] = _head_norm(k_ref[rows, :], kg_ref[...], gmat)
        m_s[rows, :] = jnp.full((rows_per, LANE), NEG, F32)
        l_s[rows, :] = jnp.zeros((rows_per, LANE), F32)
        acc_s[rows, :] = jnp.zeros((rows_per, LANE), F32)
        return carry

    lax.fori_loop(0, t_len // rows_per, prologue, 0)
    kc_ref[...] = kn_s[t_len - win:, :]
    vc_ref[...] = v_ref[t_len - win:, :]

    qi = lax.broadcasted_iota(jnp.int32, (QBLK, 2 * QBLK), 0)
    kj = lax.broadcasted_iota(jnp.int32, (QBLK, 2 * QBLK), 1)
    causal = jnp.where(kj - QBLK <= qi, 0.0, NEG)
    bias_both = jnp.where(kj < QBLK, jnp.where(kj >= qi, 0.0, NEG), causal)
    bias_first = jnp.where(kj < QBLK, NEG, causal)
    head0 = lax.broadcasted_iota(jnp.int32, (QBLK, LANE), 1) < HEAD_DIM

    for _, dil in DIL_PATTERNS:
        nblk = t_len // dil // QBLK

        def rows_of(start, dil=dil):
            if dil == 1:
                return pl.ds(start, QBLK)
            return pl.ds(start, QBLK, stride=dil)

        def unit(u, dil=dil, nblk=nblk, rows_of=rows_of):
            res = u // nblk
            blk = u - res * nblk
            cur = rows_of(blk * (QBLK * dil) + res)
            prev = rows_of(jnp.maximum(blk - 1, 0) * (QBLK * dil) + res)
            qb = qn_s[cur, :]
            kcat = _bf(jnp.concatenate([kn_s[prev, :], kn_s[cur, :]], axis=0))
            vcat = _bf(jnp.concatenate([v_ref[prev, :], v_ref[cur, :]], axis=0))
            m_old, l_old, acc_old = m_s[cur, :], l_s[cur, :], acc_s[cur, :]
            bias = jnp.where(blk > 0, bias_both, bias_first)
            yield
            stats = []
            for sel in (head0, ~head0):
                qh = _bf(jnp.where(sel, qb, 0.0))
                s = _dot(qh, kcat, _NT) + bias
                yield
                mb = jnp.max(s, axis=-1, keepdims=True)
                p = jnp.exp(s - mb)
                lb = jnp.sum(p, axis=-1, keepdims=True)
                yield
                stats.append((mb, lb, _dot(_bf(p), vcat)))
                yield
            mb = jnp.where(head0, stats[0][0], stats[1][0])
            lb = jnp.where(head0, stats[0][1], stats[1][1])
            ob = jnp.where(head0, stats[0][2], stats[1][2])
            m_new = jnp.maximum(m_old, mb)
            a_old = jnp.exp(m_old - m_new)
            a_blk = jnp.exp(mb - m_new)
            yield
            l_s[cur, :] = l_old * a_old + lb * a_blk
            acc_s[cur, :] = acc_old * a_old + ob * a_blk
            m_s[cur, :] = m_new

        def units(ui, carry, unit=unit):
            _lockstep([unit(ui * ATTN_INTERLEAVE + k) for k in range(ATTN_INTERLEAVE)])
            return carry

        lax.fori_loop(0, dil * nblk // ATTN_INTERLEAVE, units, 0)

    def epilogue(c, carry):
        rows = pl.ds(pl.multiple_of(c * rows_per, rows_per), rows_per)
        o_ref[rows, :] = acc_s[rows, :] / l_s[rows, :]
        return carry

    lax.fori_loop(0, t_len // rows_per, epilogue, 0)


def _attn_prompt(proj3, q_gain, k_gain, win):
    b, t, _ = proj3.shape
    pairs = ATTN_WIDTH // LANE
    qoff, koff, voff = P_PAD // LANE, (P_PAD + ATTN_WIDTH) // LANE, (P_PAD + 2 * ATTN_WIDTH) // LANE
    col = lambda off: pl.BlockSpec((None, t, LANE), lambda bi, hp: (bi, 0, off + hp))
    gain = lambda gvec: jnp.tile(gvec, 2).reshape(1, LANE)
    return pl.pallas_call(
        _attn_prompt_body,
        out_shape=(jax.ShapeDtypeStruct((b, t, ATTN_WIDTH), F32),
                   jax.ShapeDtypeStruct((b, win, ATTN_WIDTH), F32),
                   jax.ShapeDtypeStruct((b, win, ATTN_WIDTH), F32)),
        grid=(b, pairs),
        in_specs=[col(qoff), col(koff), col(voff),
                  pl.BlockSpec((1, LANE), lambda bi, hp: (0, 0)),
                  pl.BlockSpec((1, LANE), lambda bi, hp: (0, 0))],
        out_specs=(pl.BlockSpec((None, t, LANE), lambda bi, hp: (bi, 0, hp)),
                   pl.BlockSpec((None, win, LANE), lambda bi, hp: (bi, 0, hp)),
                   pl.BlockSpec((None, win, LANE), lambda bi, hp: (bi, 0, hp))),
        scratch_shapes=[pltpu.VMEM((t, LANE), F32) for _ in range(5)],
        compiler_params=_cparams(("arbitrary", "arbitrary")),
        name="attn_prompt",
    )(proj3, proj3, proj3, gain(q_gain), gain(k_gain))


def _attn_sample_body(qkv_ref, qg_ref, kg_ref, kt_ref, vt_ref, o_ref):
    def head_norm(x, gain):
        ms = jnp.mean(x * x, axis=1, keepdims=True)
        return x * lax.rsqrt(ms + NORM_EPS) * gain

    qkv = qkv_ref[...]
    qn = head_norm(qkv[:, :, 0:1], qg_ref[...]) * (HEAD_DIM ** -0.5)
    kn = head_norm(qkv[:, :, 1:2], kg_ref[...])
    o_ref[:, :, 1:2] = kn
    length = kt_ref.shape[-1]
    dist = length - lax.broadcasted_iota(jnp.int32, (1, 1, length), 2)
    cnt = jnp.zeros((1, 1, length), F32)
    for window, dil in DIL_PATTERNS:
        cnt = cnt + jnp.where(dist <= window, jnp.where((dist & (dil - 1)) == 0, 1.0, 0.0), 0.0)
    s = jnp.sum(kt_ref[...] * qn, axis=1, keepdims=True)
    s_self = jnp.sum(kn * qn, axis=1, keepdims=True)
    top = jnp.maximum(jnp.max(jnp.where(cnt > 0, s, NEG), axis=-1, keepdims=True), s_self)
    e = jnp.where(cnt > 0, jnp.exp(s - top), 0.0) * cnt
    e_self = float(len(DIL_PATTERNS)) * jnp.exp(s_self - top)
    den = jnp.sum(e, axis=-1, keepdims=True) + e_self
    num = jnp.sum(vt_ref[...] * e, axis=-1, keepdims=True) + e_self * qkv[:, :, 2:3]
    o_ref[:, :, 0:1] = num / den


def _attn_sample(q, k, v, q_gain, k_gain, cache_k, cache_v):
    bs = q.shape[0]
    length = cache_k.shape[1]
    assert length >= max(w for w, _ in DIL_PATTERNS) and all(d & (d - 1) == 0 for _, d in DIL_PATTERNS)
    heads = ATTN_WIDTH // HEAD_DIM
    hb = SAMPLE_HEADS
    qkv = jnp.stack([q, k, v], axis=-1).reshape(bs, heads, HEAD_DIM, 3)
    time_minor = lambda c: jnp.transpose(c, (0, 2, 3, 1))
    in_spec = pl.BlockSpec((None, hb, HEAD_DIM, 3), lambda b, h: (b, h, 0, 0))
    out_spec = pl.BlockSpec((None, hb, HEAD_DIM, 2), lambda b, h: (b, h, 0, 0))
    gain_spec = pl.BlockSpec((HEAD_DIM, 1), lambda b, h: (0, 0))
    cache_spec = pl.BlockSpec((None, hb, HEAD_DIM, length), lambda b, h: (b, h, 0, 0))
    both = pl.pallas_call(
        _attn_sample_body,
        out_shape=jax.ShapeDtypeStruct((bs, heads, HEAD_DIM, 2), F32),
        grid=(bs, heads // hb),
        in_specs=[in_spec, gain_spec, gain_spec, cache_spec, cache_spec],
        out_specs=out_spec,
        compiler_params=_cparams(("arbitrary", "arbitrary")),
        name="attn_sample",
    )(qkv, q_gain.reshape(HEAD_DIM, 1), k_gain.reshape(HEAD_DIM, 1), time_minor(cache_k), time_minor(cache_v))
    return both[..., 0].reshape(bs, ATTN_WIDTH), both[..., 1].reshape(bs, ATTN_WIDTH)


def _rwkv_prep_body(seq_mode, p_ref, prev_ref, mu_ref, w0_ref, a0_ref, kk_ref, ka_ref, rk_ref,
                    wa_hi_ref, wa_lo_ref, g2_hi_ref, g2_lo_ref,
                    r_o, lw_o, kf_o, v_o, kn_o, al_o, g_o, bonus_o):
    p = p_ref[...]
    if seq_mode:
        row0 = lax.broadcasted_iota(jnp.int32, p.shape, 0) == 0
        p_prev = jnp.where(row0, prev_ref[...], pltpu.roll(p, 1, 0))
    else:
        p_prev = prev_ref[...]
    z = p + (p_prev - p) * mu_ref[...]
    r = z[:, 0:RWKV_WIDTH]
    k = z[:, RWKV_WIDTH:2 * RWKV_WIDTH]
    v = z[:, 2 * RWKV_WIDTH:3 * RWKV_WIDTH]
    lo_wa = z[:, 3 * RWKV_WIDTH:3 * RWKV_WIDTH + LANE]
    lo_g = z[:, 3 * RWKV_WIDTH + LANE:3 * RWKV_WIDTH + LORA_PAD]
    decay_lanes = lax.broadcasted_iota(jnp.int32, lo_wa.shape, 1) < DECAY_RANK
    dwa = _mm3r(jnp.where(decay_lanes, jnp.tanh(lo_wa), lo_wa), wa_hi_ref[...], wa_lo_ref[...])
    dw, da = dwa[:, :RWKV_WIDTH], dwa[:, RWKV_WIDTH:]
    g = _mm3r(1.0 / (1.0 + jnp.exp(-lo_g)), g2_hi_ref[...], g2_lo_ref[...])
    u = -(w0_ref[...] + dw)
    softplus = jnp.maximum(u, 0.0) + jnp.log1p(jnp.exp(-jnp.abs(u)))
    lw = -jnp.exp(-softplus - 0.5)
    alpha = 1.0 / (1.0 + jnp.exp(-(a0_ref[...] + da)))
    gmat = _group_ones(LANE)
    kk = k * kk_ref[...]
    nrm = jnp.sqrt(_group_sum(kk * kk, gmat))
    kn = kk / jnp.maximum(nrm, 1e-12)
    kf = k * (1.0 + (alpha - 1.0) * ka_ref[...])
    bonus = _group_sum(r * kf * rk_ref[...], gmat) * v
    lw_o[...] = lw
    for ref, val in ((r_o, r), (kf_o, kf), (v_o, v), (kn_o, kn), (al_o, alpha), (g_o, g), (bonus_o, bonus)):
        ref[...] = val.astype(ref.dtype)


def _rwkv_prep(seq_mode, proj, prev, params, tm, tiles_per_seq):
    n = proj.shape[0]
    mu, w0, a0, k_k, k_a, r_k, wa_hi, wa_lo, g2_hi, g2_lo = params
    p_spec = pl.BlockSpec((tm, P_PAD), lambda i: (i, 0))
    if seq_mode:
        prev_rows = _prev_rows(proj, prev, tm, tiles_per_seq)
        prev_spec = pl.BlockSpec((None, 1, P_PAD), lambda i: (i, 0, 0))
        prev_arg = prev_rows
    else:
        prev_spec = pl.BlockSpec((tm, P_PAD), lambda i: (i, 0))
        prev_arg = prev
    vec = lambda width: pl.BlockSpec((1, width), lambda i: (0, 0))
    wa_spec = pl.BlockSpec((LANE, 2 * RWKV_WIDTH), lambda i: (0, 0))
    g2_spec = pl.BlockSpec((LORA_PAD - LANE, RWKV_WIDTH), lambda i: (0, 0))
    out_spec = pl.BlockSpec((tm, RWKV_WIDTH), lambda i: (i, 0))
    vec_dtype = BF16 if seq_mode else F32
    out = lambda dt: jax.ShapeDtypeStruct((n, RWKV_WIDTH), dt)
    return pl.pallas_call(
        functools.partial(_rwkv_prep_body, seq_mode),
        out_shape=(out(vec_dtype), out(F32)) + (out(vec_dtype),) * 6,
        grid=(n // tm,),
        in_specs=[p_spec, prev_spec, vec(P_PAD)] + [vec(RWKV_WIDTH)] * 5 + [wa_spec, wa_spec, g2_spec, g2_spec],
        out_specs=(out_spec,) * 8,
        compiler_params=_cparams(("arbitrary",)),
        name="rwkv_prep_seq" if seq_mode else "rwkv_prep_row",
    )(proj, prev_arg, mu, w0, a0, k_k, k_a, r_k, wa_hi, wa_lo, g2_hi, g2_lo)


def _prev_rows(proj, shift0, tm, tiles_per_seq):
    n = proj.shape[0]
    tiles = n // tm
    last = proj[tm - 1::tm, :P_PAD][:tiles - 1]
    rows = jnp.concatenate([jnp.zeros((1, P_PAD), F32), last], axis=0).reshape(tiles // tiles_per_seq, tiles_per_seq, P_PAD)
    rows = rows.at[:, 0, :].set(shift0[:, 0, :])
    return rows.reshape(tiles, 1, P_PAD)


def _rwkv_scan_body(r_ref, lw_ref, kf_ref, v_ref, kn_ref, al_ref, g_ref, bonus_ref, lnw_ref, lnb_ref, h0_ref,
                    y_ref, hT_ref, h_s):
    c = CHUNK
    gc = RWKV_GROUP * c
    n = 2 * gc
    tt = r_ref.shape[0]
    n_inst = r_ref.shape[1] // LANE
    ti = pl.program_id(2)

    @pl.when(ti == 0)
    def _():
        h_s[...] = h0_ref[...]

    head0 = lax.broadcasted_iota(jnp.int32, (c, LANE), 1) < HEAD_DIM
    ri = lax.broadcasted_iota(jnp.int32, (n, n), 0)
    ci = lax.broadcasted_iota(jnp.int32, (n, n), 1)
    same_block = (ri // c) == (ci // c)
    strict = same_block & (ci < ri)
    incl = same_block & (ci <= ri)
    eye = ri == ci
    eye_l = lax.broadcasted_iota(jnp.int32, (LANE, LANE), 0) == lax.broadcasted_iota(jnp.int32, (LANE, LANE), 1)
    tr = lax.broadcasted_iota(jnp.int32, (gc, gc), 0)
    tc = lax.broadcasted_iota(jnp.int32, (gc, gc), 1)
    tri = jnp.where(((tr // c) == (tc // c)) & (tc <= tr), 1.0, 0.0).astype(BF16)
    gmat = _group_ones(LANE)

    def stack(x):
        parts = []
        for k in range(RWKV_GROUP):
            xk = x[k * c:(k + 1) * c, :]
            parts += [jnp.where(head0, xk, 0.0), jnp.where(head0, 0.0, xk)]
        return jnp.concatenate(parts, axis=0)

    def twice(x):
        parts = []
        for k in range(RWKV_GROUP):
            xk = x[k * c:(k + 1) * c, :]
            parts += [xk, xk]
        return jnp.concatenate(parts, axis=0)

    def intra(gi, j):
        rows = pl.ds(pl.multiple_of(gi * gc, gc), gc)
        ln = slice(j * LANE, (j + 1) * LANE)
        lw = lw_ref[rows, ln]
        r, kf, v, kn, al = (ref[rows, ln].astype(F32) for ref in (r_ref, kf_ref, v_ref, kn_ref, al_ref))
        gcum = _mm2r(tri, lw)
        g_end = jnp.concatenate([jnp.broadcast_to(gcum[(k + 1) * c - 1:(k + 1) * c, :], (c, LANE))
                                 for k in range(RWKV_GROUP)], axis=0)
        at = -kn * jnp.exp(gcum - lw)
        rt = r * jnp.exp(gcum)
        inv = jnp.exp(-gcum)
        bt = kn * al * inv
        kt = kf * inv
        dec_end = jnp.exp(g_end)
        at_st, rt_st, v_st = stack(at), stack(rt), stack(v)
        b2_st, k2_st = stack(bt * dec_end), stack(kt * dec_end)
        big = _mm(jnp.concatenate([at_st, rt_st], axis=0), jnp.concatenate([twice(bt), twice(kt)], axis=0), _NT)
        a_ab = jnp.where(strict, big[0:n, 0:n], 0.0)
        a_ak = jnp.where(strict, big[0:n, n:2 * n], 0.0)
        a_rb = jnp.where(incl, big[n:2 * n, 0:n], 0.0)
        a_rk = jnp.where(incl, big[n:2 * n, n:2 * n], 0.0)
        x = jnp.where(eye, 1.0, 0.0) + a_ab
        yield
        pw = _mm(a_ab, a_ab)
        yield
        steps = int(np.log2(c)) - 1
        for s in range(steps):
            if s < steps - 1:
                both = _mm(pw, jnp.concatenate([pw, x], axis=1))
                x = x + both[:, n:]
                pw = both[:, :n]
            else:
                x = x + _mm(pw, x)
            yield
        av_st = _mm(a_ak, v_st)
        yield
        wu_uv = _mm(x, jnp.concatenate([at_st, av_st], axis=1))
        wu_st, uv_st = wu_uv[:, :LANE], wu_uv[:, LANE:]
        yield
        rb = _mm(a_rb, wu_uv)
        r2_st = rt_st + rb[:, :LANE]
        yv_st = rb[:, LANE:] + _mm(a_rk, v_st)
        yield
        trans, add = [], []
        for k in range(RWKV_GROUP):
            blk = slice(2 * k * c, 2 * (k + 1) * c)
            trans.append(jnp.where(eye_l, dec_end[k * c:k * c + 1, :], 0.0) + _mm(b2_st[blk], wu_st[blk], _TN))
            add.append(_mm(jnp.concatenate([b2_st[blk], k2_st[blk]], axis=0),
                           jnp.concatenate([uv_st[blk], v_st[blk]], axis=0), _TN))
        return rows, ln, r2_st, yv_st, trans, add, bonus_ref[rows, ln], g_ref[rows, ln]

    def finish(j, parts):
        hmat = h_s[j]
        outs = []
        for rows, ln, r2_st, yv_st, trans, add, bonus, gate in parts:
            ys = []
            for k in range(RWKV_GROUP):
                blk = slice(2 * k * c, 2 * (k + 1) * c)
                y_st = _mm(r2_st[blk], hmat) + yv_st[blk]
                hmat = _mm(trans[k], hmat) + add[k]
                ys.append(y_st[:c, :] + y_st[c:, :])
                yield
            outs.append((rows, ln, ys, bonus, gate))
        h_s[j] = hmat
        for rows, ln, ys, bonus, gate in outs:
            y = ys[0] if len(ys) == 1 else jnp.concatenate(ys, axis=0)
            mean = _group_sum(y, gmat) * (1.0 / HEAD_DIM)
            dev = y - mean
            var = _group_sum(dev * dev, gmat) * (1.0 / HEAD_DIM)
            yn = dev * lax.rsqrt(var + GN_EPS) * lnw_ref[:, ln] + lnb_ref[:, ln]
            y_ref[rows, ln] = (yn + bonus.astype(F32)) * gate.astype(F32)

    def group(gi, carry):
        parts = _lockstep([intra(gi * SCAN_AHEAD + a, j) for a in range(SCAN_AHEAD) for j in range(n_inst)])
        _lockstep([finish(j, [parts[a * n_inst + j] for a in range(SCAN_AHEAD)]) for j in range(n_inst)])
        return carry

    lax.fori_loop(0, tt // gc // SCAN_AHEAD, group, 0)

    @pl.when(ti == pl.num_programs(2) - 1)
    def _():
        hT_ref[...] = h_s[...]


def _rwkv_scan(vecs, lnw, lnb, h0, b, t):
    pairs = RWKV_WIDTH // LANE
    tt = min(t, SCAN_ROWS)
    nt = t // tt
    ni = RWKV_INTERLEAVE
    seq = pl.BlockSpec((tt, ni * LANE), lambda bi, hp, ti: (bi * nt + ti, hp))
    vec = pl.BlockSpec((1, ni * LANE), lambda bi, hp, ti: (0, hp))
    st = pl.BlockSpec((None, ni, LANE, LANE), lambda bi, hp, ti: (bi, hp, 0, 0))
    return pl.pallas_call(
        _rwkv_scan_body,
        out_shape=(jax.ShapeDtypeStruct((b * t, RWKV_WIDTH), F32),
                   jax.ShapeDtypeStruct((b, pairs, LANE, LANE), F32)),
        grid=(b, pairs // ni, nt),
        in_specs=[seq] * 8 + [vec, vec, st],
        out_specs=(seq, st),
        scratch_shapes=[pltpu.VMEM((ni, LANE, LANE), F32)],
        compiler_params=_cparams(("arbitrary", "arbitrary", "arbitrary")),
        name="rwkv_scan",
    )(*vecs, lnw.reshape(1, RWKV_WIDTH), lnb.reshape(1, RWKV_WIDTH), h0)


def _rwkv_step_body(s_ref, lw_ref, kn_ref, al_ref, k_ref, r_ref, vgb_ref, lnw_ref, lnb_ref, y_ref, so_ref):
    s = s_ref[...]
    kn = kn_ref[...]
    sa = jnp.sum(s * (-kn), axis=-1, keepdims=True)
    vgb = vgb_ref[...]
    s_new = s * jnp.exp(lw_ref[...]) + sa * (kn * al_ref[...]) + vgb[:, :, 0:1] * k_ref[...]
    so_ref[...] = s_new
    y = jnp.sum(s_new * r_ref[...], axis=-1, keepdims=True)
    mean = jnp.mean(y, axis=1, keepdims=True)
    dev = y - mean
    var = jnp.mean(dev * dev, axis=1, keepdims=True)
    yn = dev * lax.rsqrt(var + GN_EPS) * lnw_ref[...] + lnb_ref[...]
    y_ref[...] = (yn + vgb[:, :, 2:3]) * vgb[:, :, 1:2]


def _rwkv_step(state, vecs, lnw, lnb):
    bs = state.shape[0]
    r, lw, kf, v, kn, al, g, bonus = vecs
    rowv = lambda a: a.reshape(bs, RWKV_HEADS, 1, HEAD_DIM)
    vgb = jnp.stack([v, g, bonus], axis=-1).reshape(bs, RWKV_HEADS, HEAD_DIM, 3)
    row_spec = pl.BlockSpec((None, RWKV_HEADS, 1, HEAD_DIM), lambda b: (b, 0, 0, 0))
    col_spec = pl.BlockSpec((None, RWKV_HEADS, HEAD_DIM, 1), lambda b: (b, 0, 0, 0))
    vgb_spec = pl.BlockSpec((None, RWKV_HEADS, HEAD_DIM, 3), lambda b: (b, 0, 0, 0))
    st_spec = pl.BlockSpec((None, RWKV_HEADS, HEAD_DIM, HEAD_DIM), lambda b: (b, 0, 0, 0))
    par_spec = pl.BlockSpec((RWKV_HEADS, HEAD_DIM, 1), lambda b: (0, 0, 0))
    y, s_new = pl.pallas_call(
        _rwkv_step_body,
        out_shape=(jax.ShapeDtypeStruct((bs, RWKV_HEADS, HEAD_DIM, 1), F32),
                   jax.ShapeDtypeStruct(state.shape, F32)),
        grid=(bs,),
        in_specs=[st_spec] + [row_spec] * 5 + [vgb_spec] + [par_spec] * 2,
        out_specs=(col_spec, st_spec),
        compiler_params=_cparams(("arbitrary",)),
        name="rwkv_step",
    )(state, rowv(lw), rowv(kn), rowv(al), rowv(kf), rowv(r), vgb,
      lnw.reshape(RWKV_HEADS, HEAD_DIM, 1), lnb.reshape(RWKV_HEADS, HEAD_DIM, 1))
    return y.reshape(bs, RWKV_WIDTH), s_new


def _outproj_body(a_ref, r_ref, x_ref, gt_ref, w_ref, g2_ref, sc_ref, sh_ref, x1_ref, h2_ref):
    y = _dot(_bf(a_ref[...]), w_ref[0:ATTN_WIDTH, :]) + _dot(_bf(r_ref[...]), w_ref[ATTN_WIDTH:, :])
    x1 = x_ref[...] + gt_ref[...] * y
    x1_ref[...] = x1
    h2_ref[...] = _modulated_norm(x1, g2_ref[...], sc_ref[...], sh_ref[...])


def _outproj(attn, rw, x2d, gt, w_bf, g2, sc, sh, tm, tiles_per_group):
    n = x2d.shape[0]
    r = gt.shape[1]
    half = pl.BlockSpec((tm, ATTN_WIDTH), lambda i: (i, 0))
    full = pl.BlockSpec((tm, D_MODEL), lambda i: (i, 0))
    mod = pl.BlockSpec((None, r, D_MODEL), lambda i: (i // tiles_per_group, 0, 0))
    return pl.pallas_call(
        _outproj_body,
        out_shape=(jax.ShapeDtypeStruct((n, D_MODEL), F32), jax.ShapeDtypeStruct((n, D_MODEL), F32)),
        grid=(n // tm,),
        in_specs=[half, half, full, mod, pl.BlockSpec((D_MODEL, D_MODEL), lambda i: (0, 0)),
                  pl.BlockSpec((1, D_MODEL), lambda i: (0, 0)), mod, mod],
        out_specs=(full, full),
        compiler_params=_cparams(("arbitrary",)),
        name="outproj",
    )(attn, rw, x2d, gt, w_bf, g2.reshape(1, D_MODEL), sc, sh)


def _staircase():
    pairs = [(a, b) for a in range(PEER_TOPK) for b in range(PEER_TOPK) if (a + 1) * (b + 1) <= PEER_TOPK]
    return pairs


_CAND = _staircase()
_CAND_ROWS = 56


_SENTINEL = 2.0 ** 100


def _extract_top(s, count, rows, tie_safe):
    vals = []
    limit = float(s.shape[0])
    for kth in range(count):
        mx = jnp.max(s, axis=0, keepdims=True)
        if tie_safe:
            hit = rows == jnp.min(jnp.where(s == mx, rows, limit), axis=0, keepdims=True)
        else:
            hit = s == mx
        s = jnp.where(hit, _rank_mark(kth), s)
        vals.append(mx)
        yield
    return vals, s


def _rank_mark(kth):
    return -_SENTINEL * (1.0 + kth / 16.0)


def _was_taken(s):
    return s < -0.5 * _SENTINEL


def _decode_rank(s):
    return jnp.where(_was_taken(s), (s * (-1.0 / _SENTINEL) - 1.0) * 16.0, 99.0)


def _router_body(h2_ref, wq_ref, keys_ref, oh_ref, h2t_ref, cnt_ref, e1_ref, rk_ref, e2_ref, q_s):
    h2 = h2_ref[...]
    tm = h2.shape[0]
    h2t_ref[...] = _bf(h2.T)
    q_s[...] = _dot(wq_ref[...], _bf(h2), _NT)
    rows = lax.broadcasted_iota(jnp.int32, (PEER_KEYS, LANE), 0).astype(F32)
    crow = lax.broadcasted_iota(jnp.int32, (_CAND_ROWS, LANE), 0).astype(F32)

    def route(h, tie_safe):
        scores = []
        for half in range(2):
            g = 2 * h + half
            q_t = q_s[pl.ds(pl.multiple_of(g * PEER_KEYS, PEER_KEYS), PEER_KEYS), :]
            scores.append(_mm3(keys_ref[g], q_t))
        wrong = jnp.zeros((1, LANE), F32)
        for sub in range(tm // LANE):
            sl = slice(sub * LANE, (sub + 1) * LANE)
            s0, s1 = scores[0][:, sl], scores[1][:, sl]
            (tops0, fin0), (tops1, fin1) = yield from _lockstep_gen(
                [_extract_top(s0, PEER_TOPK, rows, tie_safe), _extract_top(s1, PEER_TOPK, rows, tie_safe)])
            cand = [tops0[a] + tops1[b] for a, b in _CAND]
            cand += [jnp.full((1, LANE), NEG, F32)] * (_CAND_ROWS - len(cand))
            cand = jnp.concatenate(cand, axis=0)
            top = tops0[0] + tops1[0]
            taken = jnp.zeros((_CAND_ROWS, LANE), F32)
            zsum = jnp.zeros((1, LANE), F32)
            for _ in range(PEER_TOPK):
                mx = jnp.max(cand, axis=0, keepdims=True)
                if tie_safe:
                    hit = crow == jnp.min(jnp.where(cand == mx, crow, float(_CAND_ROWS)), axis=0, keepdims=True)
                else:
                    hit = cand == mx
                taken = jnp.where(hit, 1.0, taken)
                cand = jnp.where(hit, NEG, cand)
                zsum = zsum + jnp.exp(mx - top)
                yield
            per_rank = _dot(oh_ref[...], _bf(taken))
            cnt = jnp.full((PEER_KEYS, LANE), -0.5, F32)
            for kth in range(PEER_TOPK):
                cnt = jnp.where(fin0 == _rank_mark(kth), per_rank[kth:kth + 1, :] - 0.5, cnt)
            in0, in1 = _was_taken(fin0), _was_taken(fin1)
            cnt_ref[h, :, sl] = cnt
            rk_ref[h, :, sl] = _bf(_decode_rank(fin1))
            e1_ref[h, :, sl] = jnp.where(in0, jnp.exp(s0 - tops0[0]), 0.0)
            e2_ref[h, :, sl] = _bf(jnp.where(in1, jnp.exp(s1 - tops1[0]) / zsum, 0.0))
            for mask in (jnp.where(in0, 1.0, 0.0), jnp.where(in1, 1.0, 0.0), taken):
                wrong = wrong + jnp.abs(jnp.sum(mask, axis=0, keepdims=True) - float(PEER_TOPK))
        return wrong

    def heads(tie_safe):
        def body(hi, wrong):
            res = _lockstep([route(hi * ROUTER_INTERLEAVE + k, tie_safe) for k in range(ROUTER_INTERLEAVE)])
            return wrong + sum(res)
        return lax.fori_loop(0, PEER_HEADS // ROUTER_INTERLEAVE, body, jnp.zeros((1, LANE), F32))

    wrong = heads(False)

    @pl.when(jnp.max(wrong) > 0.0)
    def _():
        heads(True)


def _router(h2, wq_t_bf, keys, tm):
    n = h2.shape[0]
    onehot = np.zeros((PEER_TOPK, _CAND_ROWS), np.float32)
    for idx, (a, _) in enumerate(_CAND):
        onehot[a, idx] = 1.0
    tok = pl.BlockSpec((None, PEER_HEADS, PEER_KEYS, tm), lambda i: (i, 0, 0, 0))
    tok_shape = lambda dt: jax.ShapeDtypeStruct((n // tm, PEER_HEADS, PEER_KEYS, tm), dt)
    return pl.pallas_call(
        _router_body,
        out_shape=(jax.ShapeDtypeStruct((D_MODEL, n), BF16),
                   tok_shape(F32), tok_shape(F32), tok_shape(BF16), tok_shape(BF16)),
        grid=(n // tm,),
        in_specs=[pl.BlockSpec((tm, D_MODEL), lambda i: (i, 0)),
                  pl.BlockSpec((D_MODEL, D_MODEL), lambda i: (0, 0)),
                  pl.BlockSpec((2 * PEER_HEADS, PEER_KEYS, PEER_KEYS), lambda i: (0, 0, 0)),
                  pl.BlockSpec((PEER_TOPK, _CAND_ROWS), lambda i: (0, 0))],
        out_specs=(pl.BlockSpec((D_MODEL, tm), lambda i: (0, i)), tok, tok, tok, tok),
        scratch_shapes=[pltpu.VMEM((D_MODEL, tm), F32)],
        compiler_params=_cparams(("arbitrary",)),
        name="peer_router",
    )(h2, wq_t_bf, keys, jnp.asarray(onehot, BF16))


def _experts_body(h2t_ref, u_ref, v_ref, cnt_ref, e1_ref, rk_ref, e2_ref, x1_ref, gt_ref, y_ref, p_s):
    j = pl.program_id(1)
    last = pl.num_programs(1) - 1
    eb = u_ref.shape[0]
    per = eb // PEER_KEYS
    cur = j % 2

    @pl.when(j == 0)
    def _():
        y_ref[...] = jnp.zeros(y_ref.shape, F32)
        p_s[1] = jnp.zeros(p_s.shape[1:], BF16)

    @pl.when(j < last)
    def _():
        y_ref[...] += _dot(p_s[1 - cur], v_ref[...], _TN)
        h2t = h2t_ref[...]
        for ii in range(per):
            sub = slice(ii * PEER_KEYS, (ii + 1) * PEER_KEYS)
            i = j * per + ii
            gates = []
            for part in range(cnt_ref.shape[0]):
                wt = None
                for h in range(PEER_HEADS):
                    partners = _bf(cnt_ref[part, h, pl.ds(i, 1), :])
                    e1 = _bf(e1_ref[part, h, pl.ds(i, 1), :])
                    term = jnp.where(rk_ref[part, h] < partners, e2_ref[part, h], 0.0) * e1
                    wt = term if wt is None else wt + term
                gates.append(wt)
            wt = gates[0] if len(gates) == 1 else jnp.concatenate(gates, axis=1)
            act_in = _dot(u_ref[sub, :], h2t)
            act = 0.5 * act_in * (1.0 + lax.erf(act_in * (2.0 ** -0.5)))
            p_s[cur, sub, :] = wt * _bf(act)

    @pl.when(j == last)
    def _():
        acc = y_ref[...] + _dot(p_s[1 - cur], v_ref[...], _TN)
        y_ref[...] = x1_ref[...] + gt_ref[...] * acc


def _experts(h2t, u_bf, v_bf, route, x1, gt, tm, tiles_per_group):
    n = x1.shape[0]
    eb = EXPERT_BLOCK
    nblk = PEER_EXPERTS // eb
    r = gt.shape[1]
    slab = route[0].shape[-1]
    tok = pl.BlockSpec((tm // slab, PEER_HEADS, PEER_KEYS, slab), lambda i, j: (i, 0, 0, 0))
    return pl.pallas_call(
        _experts_body,
        out_shape=jax.ShapeDtypeStruct((n, D_MODEL), F32),
        grid=(n // tm, nblk + 1),
        in_specs=[pl.BlockSpec((D_MODEL, tm), lambda i, j: (0, i)),
                  pl.BlockSpec((eb, D_MODEL), lambda i, j: (jnp.minimum(j, nblk - 1), 0)),
                  pl.BlockSpec((eb, D_MODEL), lambda i, j: (jnp.maximum(j - 1, 0), 0)),
                  tok, tok, tok, tok,
                  pl.BlockSpec((tm, D_MODEL), lambda i, j: (i, 0)),
                  pl.BlockSpec((None, r, D_MODEL), lambda i, j: (i // tiles_per_group, 0, 0))],
        out_specs=pl.BlockSpec((tm, D_MODEL), lambda i, j: (i, 0)),
        scratch_shapes=[pltpu.VMEM((2, eb, tm), BF16)],
        compiler_params=_cparams(("arbitrary", "arbitrary")),
        name="peer_experts",
    )(h2t, u_bf, v_bf, *route, x1, gt)


def _pad_rows(a, rows):
    return jnp.pad(a, ((0, rows - a.shape[0]),) + ((0, 0),) * (a.ndim - 1))


def _state_to_pairs(state):
    b = state.shape[0]
    h = jnp.swapaxes(state, -1, -2).reshape(b, RWKV_HEADS // 2, 2, HEAD_DIM, HEAD_DIM)
    z = jnp.zeros_like(h[:, :, 0])
    top = jnp.concatenate([h[:, :, 0], z], axis=-1)
    bot = jnp.concatenate([z, h[:, :, 1]], axis=-1)
    return jnp.concatenate([top, bot], axis=-2)


def _pairs_to_state(hp):
    b = hp.shape[0]
    h0 = hp[:, :, :HEAD_DIM, :HEAD_DIM]
    h1 = hp[:, :, HEAD_DIM:, HEAD_DIM:]
    h = jnp.stack([h0, h1], axis=2).reshape(b, RWKV_HEADS, HEAD_DIM, HEAD_DIM)
    return jnp.swapaxes(h, -1, -2)


def kernel(x_prompt, x_sample, cache_attn_k, cache_attn_v, state_rwkv, state_rwkv_shift, c_prompt, c_sample, w_ada, b_ada, norm1_g, norm2_g, w_in, q_gain, k_gain, rwkv_mu, rwkv_w0, rwkv_w2, rwkv_a0, rwkv_a2, rwkv_g2, rwkv_k_k, rwkv_k_a, rwkv_r_k, rwkv_lnx_w, rwkv_lnx_b, w_out, peer_w_q, peer_sub_keys, peer_u, peer_v):
    b, t, _ = x_prompt.shape
    bs = x_sample.shape[0]
    n = b * t
    heads = ATTN_WIDTH // HEAD_DIM
    win = min(CACHE_LEN, t)

    w_in_bf = w_in.astype(BF16)
    w_out_bf = w_out.astype(BF16)
    wq_t_bf = peer_w_q.T.astype(BF16)
    keys = peer_sub_keys.reshape(2 * PEER_HEADS, PEER_KEYS, PEER_KEYS)
    u_bf = peer_u.astype(BF16)
    v_bf = peer_v.astype(BF16)
    row = lambda a: a.reshape(1, -1)
    wa = jnp.zeros((LANE, 2 * RWKV_WIDTH), F32)
    wa = wa.at[:DECAY_RANK, :RWKV_WIDTH].set(rwkv_w2).at[DECAY_RANK:DECAY_RANK + A_RANK, RWKV_WIDTH:].set(rwkv_a2)
    g2 = jnp.zeros((LORA_PAD - LANE, RWKV_WIDTH), F32).at[:GATE_RANK].set(rwkv_g2)
    hi_lo = lambda w: (w.astype(BF16), (w - w.astype(BF16).astype(F32)).astype(BF16))
    rwkv_params = (row(jnp.pad(rwkv_mu, (0, P_PAD - RWKV_PROJ))), row(rwkv_w0), row(rwkv_a0), row(rwkv_k_k),
                   row(rwkv_k_a), row(rwkv_r_k), *hi_lo(wa), *hi_lo(g2))

    c_rows = b + bs
    c_pad = -(-c_rows // 8) * 8
    mod = _adaln(_pad_rows(jnp.concatenate([c_prompt, c_sample], axis=0), c_pad), w_ada, b_ada)
    mod_p = [m.reshape(b, 1, D_MODEL) for m in jnp.split(mod[:b], 6, axis=-1)]
    mod_s = [m.reshape(1, bs, D_MODEL) for m in jnp.split(mod[b:c_rows], 6, axis=-1)]

    tm_p = min(TM_INPROJ, t)
    xp = x_prompt.reshape(n, D_MODEL)
    proj_p = _inproj(xp, norm1_g, mod_p[1], mod_p[0], w_in_bf, tm_p, t // tm_p)
    attn_p, k_cache, v_cache = _attn_prompt(proj_p.reshape(b, t, PROJ_PAD), q_gain, k_gain, win)
    tm_r = min(TM_PREP, t)
    shift0 = jnp.zeros((b, 1, P_PAD), F32)
    vecs_p = _rwkv_prep(True, proj_p, shift0, rwkv_params, tm_r, t // tm_r)
    rw_p, h_fin = _rwkv_scan(vecs_p, rwkv_lnx_w, rwkv_lnx_b, jnp.zeros((b, RWKV_HEADS // 2, LANE, LANE), F32), b, t)
    tm_o = min(TM_OUTPROJ, t)
    x1_p, h2_p = _outproj(attn_p.reshape(n, ATTN_WIDTH), rw_p, xp, mod_p[2], w_out_bf, norm2_g,
                          mod_p[4], mod_p[3], tm_o, t // tm_o)
    tm_q = min(TM_ROUTE, t)
    tm_e = min(TM_EXPERT, t)
    h2t_p, *route_p = _router(h2_p, wq_t_bf, keys, tm_q)
    y_p = _experts(h2t_p, u_bf, v_bf, route_p, x1_p, mod_p[5], tm_e, t // tm_e)

    xs = x_sample.reshape(bs, D_MODEL)
    proj_s = _inproj(xs, norm1_g, mod_s[1], mod_s[0], w_in_bf, bs, 1)
    q_s = proj_s[:, P_PAD:P_PAD + ATTN_WIDTH]
    k_s = proj_s[:, P_PAD + ATTN_WIDTH:P_PAD + 2 * ATTN_WIDTH]
    v_s = proj_s[:, P_PAD + 2 * ATTN_WIDTH:]
    attn_s, kn_s = _attn_sample(q_s, k_s, v_s, q_gain, k_gain, cache_attn_k, cache_attn_v)
    prev_s = jnp.pad(state_rwkv_shift.reshape(bs, RWKV_PROJ), ((0, 0), (0, P_PAD - RWKV_PROJ)))
    vecs_s = _rwkv_prep(False, proj_s, prev_s, rwkv_params, bs, 1)
    rw_s, state_s = _rwkv_step(state_rwkv, vecs_s, rwkv_lnx_w, rwkv_lnx_b)
    x1_s, h2_s = _outproj(attn_s, rw_s, xs, mod_s[2], w_out_bf, norm2_g, mod_s[4], mod_s[3], bs, 1)
    ns = -(-bs // LANE) * LANE
    h2t_s, *route_s = _router(_pad_rows(h2_s, ns), wq_t_bf, keys, LANE)
    gt2_s = _pad_rows(mod_s[5][0], ns).reshape(ns // LANE, LANE, D_MODEL)
    y_s = _experts(h2t_s, u_bf, v_bf, route_s, _pad_rows(x1_s, ns), gt2_s, LANE, 1)[:bs]

    return (y_p.reshape(b, t, D_MODEL), y_s.reshape(bs, 1, D_MODEL),
            k_cache.reshape(b, win, heads, HEAD_DIM), v_cache.reshape(b, win, heads, HEAD_DIM),
            _pairs_to_state(h_fin), proj_p.reshape(b, t, PROJ_PAD)[:, t - 1:, :RWKV_PROJ],
            kn_s.reshape(bs, 1, heads, HEAD_DIM), v_s.reshape(bs, 1, heads, HEAD_DIM),
            state_s, proj_s[:, :RWKV_PROJ].reshape(bs, 1, RWKV_PROJ))
```

```python
import functools

import numpy as np
import jax
import jax.numpy as jnp
from jax import lax
from jax.experimental import pallas as pl
from jax.experimental.pallas import tpu as pltpu

F32 = jnp.float32
BF16 = jnp.bfloat16

D_MODEL = 2048
HEAD_DIM = 64
ATTN_WIDTH = 1024
RWKV_WIDTH = 1024
RWKV_HEADS = RWKV_WIDTH // HEAD_DIM
DECAY_RANK = 64
A_RANK = 64
GATE_RANK = 160
RWKV_PROJ = 3 * RWKV_WIDTH + DECAY_RANK + A_RANK + GATE_RANK
P_PAD = 3584
PROJ_PAD = P_PAD + 3 * ATTN_WIDTH
LORA_PAD = 384
DIL_PATTERNS = ((128, 1), (512, 4), (2048, 16))
CACHE_LEN = 2048
NORM_EPS = 1e-6
GN_EPS = HEAD_DIM * 1e-5
PEER_HEADS = 8
PEER_KEYS = 128
PEER_TOPK = 16
PEER_EXPERTS = PEER_KEYS * PEER_KEYS
NEG = -1e30

LANE = 128
QBLK = 128
CHUNK = 64
RWKV_GROUP = 1
RWKV_INTERLEAVE = 8
SCAN_AHEAD = 2
ROUTER_INTERLEAVE = 4
ATTN_INTERLEAVE = 8
VMEM_LIMIT = 56 * 1024 * 1024

TM_INPROJ = 1024
TM_PREP = 256
TM_OUTPROJ = 256
TM_ROUTE = 256
TM_EXPERT = 512
EXPERT_BLOCK = 1024
SCAN_ROWS = 256
ADALN_COLS = 1024
NORM_ROWS = 512
SAMPLE_HEADS = 8


def _cparams(sem):
    return pltpu.CompilerParams(dimension_semantics=sem, vmem_limit_bytes=VMEM_LIMIT)


_NN = (((1,), (0,)), ((), ()))
_NT = (((1,), (1,)), ((), ()))
_TN = (((0,), (0,)), ((), ()))


def _bf(x):
    return x.astype(BF16)


def _dot(a, b, dims=_NN):
    return lax.dot_general(a, b, dims, preferred_element_type=F32)


def _mm(a, b, dims=_NN):
    return _dot(_bf(a), _bf(b), dims)


def _split(x):
    hi = _bf(x)
    lo = _bf(x - hi.astype(F32))
    return hi, lo


def _mm3(a, b, dims=_NN):
    ah, al = _split(a)
    bh, bl = _split(b)
    return _dot(ah, bh, dims) + (_dot(ah, bl, dims) + _dot(al, bh, dims))


def _mm3r(a, b_hi, b_lo, dims=_NN):
    ah, al = _split(a)
    return _dot(ah, b_hi, dims) + (_dot(ah, b_lo, dims) + _dot(al, b_hi, dims))


def _mm2l(a, b_exact, dims=_NN):
    ah, al = _split(a)
    return _dot(ah, b_exact, dims) + _dot(al, b_exact, dims)


def _mm2r(a_exact, b, dims=_NN):
    bh, bl = _split(b)
    return _dot(a_exact, bh, dims) + _dot(a_exact, bl, dims)


def _group_ones(n):
    r = lax.broadcasted_iota(jnp.int32, (n, n), 0) // HEAD_DIM
    c = lax.broadcasted_iota(jnp.int32, (n, n), 1) // HEAD_DIM
    return jnp.where(r == c, 1.0, 0.0).astype(BF16)


def _group_sum(x, g):
    cols = x.shape[1] // LANE
    parts = [_mm2l(x[:, c * LANE:(c + 1) * LANE], g) for c in range(cols)]
    return parts[0] if cols == 1 else jnp.concatenate(parts, axis=1)


def _lockstep_gen(gens):
    results = [None] * len(gens)
    live = list(enumerate(gens))
    while live:
        nxt = []
        for idx, gen in live:
            try:
                next(gen)
                nxt.append((idx, gen))
            except StopIteration as stop:
                results[idx] = stop.value
        live = nxt
        if live:
            yield
    return results


def _lockstep(gens):
    runner = _lockstep_gen(gens)
    while True:
        try:
            next(runner)
        except StopIteration as stop:
            return stop.value


def _adaln_body(c_ref, w_ref, b_ref, o_ref):
    c = c_ref[...]
    s = c * (1.0 / (1.0 + jnp.exp(-c)))
    o_ref[...] = _mm3(s, w_ref[...]) + b_ref[...]


def _adaln(c, w_ada, b_ada):
    rows = c.shape[0]
    n = w_ada.shape[1]
    tn = ADALN_COLS
    return pl.pallas_call(
        _adaln_body,
        out_shape=jax.ShapeDtypeStruct((rows, n), F32),
        grid=(n // tn,),
        in_specs=[pl.BlockSpec((rows, D_MODEL), lambda j: (0, 0)),
                  pl.BlockSpec((D_MODEL, tn), lambda j: (0, j)),
                  pl.BlockSpec((1, tn), lambda j: (0, j))],
        out_specs=pl.BlockSpec((rows, tn), lambda j: (0, j)),
        compiler_params=_cparams(("arbitrary",)),
        name="adaln",
    )(c, w_ada, b_ada.reshape(1, n))


def _modulated_norm(x, g, sc, sh):
    ms = jnp.mean(x * x, axis=-1, keepdims=True)
    return (x * lax.rsqrt(ms + NORM_EPS) * g) * (1.0 + sc) + sh


def _inproj_body(x_ref, g_ref, sc_ref, sh_ref, w_ref, o_ref, h_s):
    j = pl.program_id(1)

    @pl.when(j == 0)
    def _():
        h_s[...] = _bf(_modulated_norm(x_ref[...], g_ref[...], sc_ref[...], sh_ref[...]))

    tn = w_ref.shape[1]
    valid = jnp.where(j == _P_TILES - 1, RWKV_PROJ - (_P_TILES - 1) * tn, tn)
    cols = lax.broadcasted_iota(jnp.int32, w_ref.shape, 1)
    o_ref[...] = _dot(h_s[...], _bf(jnp.where(cols < valid, w_ref[...], 0.0)))


_IN_TN = 512
_P_TILES = P_PAD // _IN_TN


def _inproj(x2d, g, sc, sh, w_in, tm, tiles_per_group):
    n = x2d.shape[0]
    r = sc.shape[1]
    tn = _IN_TN
    qkv_tiles = 3 * ATTN_WIDTH // tn
    src = lambda j: jnp.where(j < _P_TILES, j + qkv_tiles, j - _P_TILES)
    mod_spec = pl.BlockSpec((None, r, D_MODEL), lambda i, j: (i // tiles_per_group, 0, 0))
    return pl.pallas_call(
        _inproj_body,
        out_shape=jax.ShapeDtypeStruct((n, PROJ_PAD), F32),
        grid=(n // tm, PROJ_PAD // tn),
        in_specs=[pl.BlockSpec((tm, D_MODEL), lambda i, j: (i, 0)),
                  pl.BlockSpec((1, D_MODEL), lambda i, j: (0, 0)),
                  mod_spec, mod_spec,
                  pl.BlockSpec((D_MODEL, tn), lambda i, j: (0, src(j)))],
        out_specs=pl.BlockSpec((tm, tn), lambda i, j: (i, j)),
        scratch_shapes=[pltpu.VMEM((tm, D_MODEL), BF16)],
        compiler_params=_cparams(("arbitrary", "arbitrary")),
        name="inproj",
    )(x2d, g.reshape(1, D_MODEL), sc, sh, w_in)


def _head_norm(x, gain, gmat):
    ss = _group_sum(x * x, gmat)
    return x * lax.rsqrt(ss * (1.0 / HEAD_DIM) + NORM_EPS) * gain


def _attn_prompt_body(q_ref, k_ref, v_ref, qg_ref, kg_ref, o_ref, kc_ref, vc_ref,
                      qn_s, kn_s, m_s, l_s, acc_s):
    t_len = q_ref.shape[0]
    win = kc_ref.shape[0]
    gmat = _group_ones(LANE)
    rows_per = NORM_ROWS

    def prologue(c, carry):
        rows = pl.ds(pl.multiple_of(c * rows_per, rows_per), rows_per)
        qn_s[rows, :] = _head_norm(q_ref[rows, :], qg_ref[...], gmat) * (HEAD_DIM ** -0.5)
        kn_s[rows, :] = _head_norm(k_ref[rows, :], kg_ref[...], gmat)
        m_s[rows, :] = jnp.full((rows_per, LANE), NEG, F32)
        l_s[rows, :] = jnp.zeros((rows_per, LANE), F32)
        acc_s[rows, :] = jnp.zeros((rows_per, LANE), F32)
        return carry

    lax.fori_loop(0, t_len // rows_per, prologue, 0)
    kc_ref[...] = kn_s[t_len - win:, :]
    vc_ref[...] = v_ref[t_len - win:, :]

    qi = lax.broadcasted_iota(jnp.int32, (QBLK, 2 * QBLK), 0)
    kj = lax.broadcasted_iota(jnp.int32, (QBLK, 2 * QBLK), 1)
    causal = jnp.where(kj - QBLK <= qi, 0.0, NEG)
    bias_both = jnp.where(kj < QBLK, jnp.where(kj >= qi, 0.0, NEG), causal)
    bias_first = jnp.where(kj < QBLK, NEG, causal)
    head0 = lax.broadcasted_iota(jnp.int32, (QBLK, LANE), 1) < HEAD_DIM

    for _, dil in DIL_PATTERNS:
        nblk = t_len // dil // QBLK

        def rows_of(start, dil=dil):
            if dil == 1:
                return pl.ds(start, QBLK)
            return pl.ds(start, QBLK, stride=dil)

        def unit(u, dil=dil, nblk=nblk, rows_of=rows_of):
            res = u // nblk
            blk = u - res * nblk
            cur = rows_of(blk * (QBLK * dil) + res)
            prev = rows_of(jnp.maximum(blk - 1, 0) * (QBLK * dil) + res)
            qb = qn_s[cur, :]
            kcat = _bf(jnp.concatenate([kn_s[prev, :], kn_s[cur, :]], axis=0))
            vcat = _bf(jnp.concatenate([v_ref[prev, :], v_ref[cur, :]], axis=0))
            m_old, l_old, acc_old = m_s[cur, :], l_s[cur, :], acc_s[cur, :]
            bias = jnp.where(blk > 0, bias_both, bias_first)
            yield
            stats = []
            for sel in (head0, ~head0):
                qh = _bf(jnp.where(sel, qb, 0.0))
                s = _dot(qh, kcat, _NT) + bias
                yield
                mb = jnp.max(s, axis=-1, keepdims=True)
                p = jnp.exp(s - mb)
                lb = jnp.sum(p, axis=-1, keepdims=True)
                yield
                stats.append((mb, lb, _dot(_bf(p), vcat)))
                yield
            mb = jnp.where(head0, stats[0][0], stats[1][0])
            lb = jnp.where(head0, stats[0][1], stats[1][1])
            ob = jnp.where(head0, stats[0][2], stats[1][2])
            m_new = jnp.maximum(m_old, mb)
            a_old = jnp.exp(m_old - m_new)
            a_blk = jnp.exp(mb - m_new)
            yield
            l_s[cur, :] = l_old * a_old + lb * a_blk
            acc_s[cur, :] = acc_old * a_old + ob * a_blk
            m_s[cur, :] = m_new

        def units(ui, carry, unit=unit):
            _lockstep([unit(ui * ATTN_INTERLEAVE + k) for k in range(ATTN_INTERLEAVE)])
            return carry

        lax.fori_loop(0, dil * nblk // ATTN_INTERLEAVE, units, 0)

    def epilogue(c, carry):
        rows = pl.ds(pl.multiple_of(c * rows_per, rows_per), rows_per)
        o_ref[rows, :] = acc_s[rows, :] / l_s[rows, :]
        return carry

    lax.fori_loop(0, t_len // rows_per, epilogue, 0)


def _attn_prompt(proj3, q_gain, k_gain, win):
    b, t, _ = proj3.shape
    pairs = ATTN_WIDTH // LANE
    qoff, koff, voff = P_PAD // LANE, (P_PAD + ATTN_WIDTH) // LANE, (P_PAD + 2 * ATTN_WIDTH) // LANE
    col = lambda off: pl.BlockSpec((None, t, LANE), lambda bi, hp: (bi, 0, off + hp))
    gain = lambda gvec: jnp.tile(gvec, 2).reshape(1, LANE)
    return pl.pallas_call(
        _attn_prompt_body,
        out_shape=(jax.ShapeDtypeStruct((b, t, ATTN_WIDTH), F32),
                   jax.ShapeDtypeStruct((b, win, ATTN_WIDTH), F32),
                   jax.ShapeDtypeStruct((b, win, ATTN_WIDTH), F32)),
        grid=(b, pairs),
        in_specs=[col(qoff), col(koff), col(voff),
                  pl.BlockSpec((1, LANE), lambda bi, hp: (0, 0)),
                  pl.BlockSpec((1, LANE), lambda bi, hp: (0, 0))],
        out_specs=(pl.BlockSpec((None, t, LANE), lambda bi, hp: (bi, 0, hp)),
                   pl.BlockSpec((None, win, LANE), lambda bi, hp: (bi, 0, hp)),
                   pl.BlockSpec((None, win, LANE), lambda bi, hp: (bi, 0, hp))),
        scratch_shapes=[pltpu.VMEM((t, LANE), F32) for _ in range(5)],
        compiler_params=_cparams(("arbitrary", "arbitrary")),
        name="attn_prompt",
    )(proj3, proj3, proj3, gain(q_gain), gain(k_gain))


def _attn_sample_body(qkv_ref, qg_ref, kg_ref, kt_ref, vt_ref, o_ref):
    def head_norm(x, gain):
        ms = jnp.mean(x * x, axis=1, keepdims=True)
        return x * lax.rsqrt(ms + NORM_EPS) * gain

    qkv = qkv_ref[...]
    qn = head_norm(qkv[:, :, 0:1], qg_ref[...]) * (HEAD_DIM ** -0.5)
    kn = head_norm(qkv[:, :, 1:2], kg_ref[...])
    o_ref[:, :, 1:2] = kn
    length = kt_ref.shape[-1]
    dist = length - lax.broadcasted_iota(jnp.int32, (1, 1, length), 2)
    cnt = jnp.zeros((1, 1, length), F32)
    for window, dil in DIL_PATTERNS:
        cnt = cnt + jnp.where(dist <= window, jnp.where((dist & (dil - 1)) == 0, 1.0, 0.0), 0.0)
    s = jnp.sum(kt_ref[...] * qn, axis=1, keepdims=True)
    s_self = jnp.sum(kn * qn, axis=1, keepdims=True)
    top = jnp.maximum(jnp.max(jnp.where(cnt > 0, s, NEG), axis=-1, keepdims=True), s_self)
    e = jnp.where(cnt > 0, jnp.exp(s - top), 0.0) * cnt
    e_self = float(len(DIL_PATTERNS)) * jnp.exp(s_self - top)
    den = jnp.sum(e, axis=-1, keepdims=True) + e_self
    num = jnp.sum(vt_ref[...] * e, axis=-1, keepdims=True) + e_self * qkv[:, :, 2:3]
    o_ref[:, :, 0:1] = num / den


def _attn_sample(q, k, v, q_gain, k_gain, cache_k, cache_v):
    bs = q.shape[0]
    length = cache_k.shape[1]
    assert length >= max(w for w, _ in DIL_PATTERNS) and all(d & (d - 1) == 0 for _, d in DIL_PATTERNS)
    heads = ATTN_WIDTH // HEAD_DIM
    hb = SAMPLE_HEADS
    qkv = jnp.stack([q, k, v], axis=-1).reshape(bs, heads, HEAD_DIM, 3)
    time_minor = lambda c: jnp.transpose(c, (0, 2, 3, 1))
    in_spec = pl.BlockSpec((None, hb, HEAD_DIM, 3), lambda b, h: (b, h, 0, 0))
    out_spec = pl.BlockSpec((None, hb, HEAD_DIM, 2), lambda b, h: (b, h, 0, 0))
    gain_spec = pl.BlockSpec((HEAD_DIM, 1), lambda b, h: (0, 0))
    cache_spec = pl.BlockSpec((None, hb, HEAD_DIM, length), lambda b, h: (b, h, 0, 0))
    both = pl.pallas_call(
        _attn_sample_body,
        out_shape=jax.ShapeDtypeStruct((bs, heads, HEAD_DIM, 2), F32),
        grid=(bs, heads // hb),
        in_specs=[in_spec, gain_spec, gain_spec, cache_spec, cache_spec],
        out_specs=out_spec,
        compiler_params=_cparams(("arbitrary", "arbitrary")),
        name="attn_sample",
    )(qkv, q_gain.reshape(HEAD_DIM, 1), k_gain.reshape(HEAD_DIM, 1), time_minor(cache_k), time_minor(cache_v))
    return both[..., 0].reshape(bs, ATTN_WIDTH), both[..., 1].reshape(bs, ATTN_WIDTH)


def _rwkv_prep_body(seq_mode, p_ref, prev_ref, mu_ref, w0_ref, a0_ref, kk_ref, ka_ref, rk_ref,
                    wa_hi_ref, wa_lo_ref, g2_hi_ref, g2_lo_ref,
                    r_o, lw_o, kf_o, v_o, kn_o, al_o, g_o, bonus_o):
    p = p_ref[...]
    if seq_mode:
        row0 = lax.broadcasted_iota(jnp.int32, p.shape, 0) == 0
        p_prev = jnp.where(row0, prev_ref[...], pltpu.roll(p, 1, 0))
    else:
        p_prev = prev_ref[...]
    z = p + (p_prev - p) * mu_ref[...]
    r = z[:, 0:RWKV_WIDTH]
    k = z[:, RWKV_WIDTH:2 * RWKV_WIDTH]
    v = z[:, 2 * RWKV_WIDTH:3 * RWKV_WIDTH]
    lo_wa = z[:, 3 * RWKV_WIDTH:3 * RWKV_WIDTH + LANE]
    lo_g = z[:, 3 * RWKV_WIDTH + LANE:3 * RWKV_WIDTH + LORA_PAD]
    decay_lanes = lax.broadcasted_iota(jnp.int32, lo_wa.shape, 1) < DECAY_RANK
    dwa = _mm3r(jnp.where(decay_lanes, jnp.tanh(lo_wa), lo_wa), wa_hi_ref[...], wa_lo_ref[...])
    dw, da = dwa[:, :RWKV_WIDTH], dwa[:, RWKV_WIDTH:]
    g = _mm3r(1.0 / (1.0 + jnp.exp(-lo_g)), g2_hi_ref[...], g2_lo_ref[...])
    u = -(w0_ref[...] + dw)
    softplus = jnp.maximum(u, 0.0) + jnp.log1p(jnp.exp(-jnp.abs(u)))
    lw = -jnp.exp(-softplus - 0.5)
    alpha = 1.0 / (1.0 + jnp.exp(-(a0_ref[...] + da)))
    gmat = _group_ones(LANE)
    kk = k * kk_ref[...]
    nrm = jnp.sqrt(_group_sum(kk * kk, gmat))
    kn = kk / jnp.maximum(nrm, 1e-12)
    kf = k * (1.0 + (alpha - 1.0) * ka_ref[...])
    bonus = _group_sum(r * kf * rk_ref[...], gmat) * v
    lw_o[...] = lw
    for ref, val in ((r_o, r), (kf_o, kf), (v_o, v), (kn_o, kn), (al_o, alpha), (g_o, g), (bonus_o, bonus)):
        ref[...] = val.astype(ref.dtype)


def _rwkv_prep(seq_mode, proj, prev, params, tm, tiles_per_seq):
    n = proj.shape[0]
    mu, w0, a0, k_k, k_a, r_k, wa_hi, wa_lo, g2_hi, g2_lo = params
    p_spec = pl.BlockSpec((tm, P_PAD), lambda i: (i, 0))
    if seq_mode:
        prev_rows = _prev_rows(proj, prev, tm, tiles_per_seq)
        prev_spec = pl.BlockSpec((None, 1, P_PAD), lambda i: (i, 0, 0))
        prev_arg = prev_rows
    else:
        prev_spec = pl.BlockSpec((tm, P_PAD), lambda i: (i, 0))
        prev_arg = prev
    vec = lambda width: pl.BlockSpec((1, width), lambda i: (0, 0))
    wa_spec = pl.BlockSpec((LANE, 2 * RWKV_WIDTH), lambda i: (0, 0))
    g2_spec = pl.BlockSpec((LORA_PAD - LANE, RWKV_WIDTH), lambda i: (0, 0))
    out_spec = pl.BlockSpec((tm, RWKV_WIDTH), lambda i: (i, 0))
    vec_dtype = BF16 if seq_mode else F32
    out = lambda dt: jax.ShapeDtypeStruct((n, RWKV_WIDTH), dt)
    return pl.pallas_call(
        functools.partial(_rwkv_prep_body, seq_mode),
        out_shape=(out(vec_dtype), out(F32)) + (out(vec_dtype),) * 6,
        grid=(n // tm,),
        in_specs=[p_spec, prev_spec, vec(P_PAD)] + [vec(RWKV_WIDTH)] * 5 + [wa_spec, wa_spec, g2_spec, g2_spec],
        out_specs=(out_spec,) * 8,
        compiler_params=_cparams(("arbitrary",)),
        name="rwkv_prep_seq" if seq_mode else "rwkv_prep_row",
    )(proj, prev_arg, mu, w0, a0, k_k, k_a, r_k, wa_hi, wa_lo, g2_hi, g2_lo)


def _prev_rows(proj, shift0, tm, tiles_per_seq):
    n = proj.shape[0]
    tiles = n // tm
    last = proj[tm - 1::tm, :P_PAD][:tiles - 1]
    rows = jnp.concatenate([jnp.zeros((1, P_PAD), F32), last], axis=0).reshape(tiles // tiles_per_seq, tiles_per_seq, P_PAD)
    rows = rows.at[:, 0, :].set(shift0[:, 0, :])
    return rows.reshape(tiles, 1, P_PAD)


def _rwkv_scan_body(r_ref, lw_ref, kf_ref, v_ref, kn_ref, al_ref, g_ref, bonus_ref, lnw_ref, lnb_ref, h0_ref,
                    y_ref, hT_ref, h_s):
    c = CHUNK
    gc = RWKV_GROUP * c
    n = 2 * gc
    tt = r_ref.shape[0]
    n_inst = r_ref.shape[1] // LANE
    ti = pl.program_id(2)

    @pl.when(ti == 0)
    def _():
        h_s[...] = h0_ref[...]

    head0 = lax.broadcasted_iota(jnp.int32, (c, LANE), 1) < HEAD_DIM
    ri = lax.broadcasted_iota(jnp.int32, (n, n), 0)
    ci = lax.broadcasted_iota(jnp.int32, (n, n), 1)
    same_block = (ri // c) == (ci // c)
    strict = same_block & (ci < ri)
    incl = same_block & (ci <= ri)
    eye = ri == ci
    eye_l = lax.broadcasted_iota(jnp.int32, (LANE, LANE), 0) == lax.broadcasted_iota(jnp.int32, (LANE, LANE), 1)
    tr = lax.broadcasted_iota(jnp.int32, (gc, gc), 0)
    tc = lax.broadcasted_iota(jnp.int32, (gc, gc), 1)
    tri = jnp.where(((tr // c) == (tc // c)) & (tc <= tr), 1.0, 0.0).astype(BF16)
    gmat = _group_ones(LANE)

    def stack(x):
        parts = []
        for k in range(RWKV_GROUP):
            xk = x[k * c:(k + 1) * c, :]
            parts += [jnp.where(head0, xk, 0.0), jnp.where(head0, 0.0, xk)]
        return jnp.concatenate(parts, axis=0)

    def twice(x):
        parts = []
        for k in range(RWKV_GROUP):
            xk = x[k * c:(k + 1) * c, :]
            parts += [xk, xk]
        return jnp.concatenate(parts, axis=0)

    def intra(gi, j):
        rows = pl.ds(pl.multiple_of(gi * gc, gc), gc)
        ln = slice(j * LANE, (j + 1) * LANE)
        lw = lw_ref[rows, ln]
        r, kf, v, kn, al = (ref[rows, ln].astype(F32) for ref in (r_ref, kf_ref, v_ref, kn_ref, al_ref))
        gcum = _mm2r(tri, lw)
        g_end = jnp.concatenate([jnp.broadcast_to(gcum[(k + 1) * c - 1:(k + 1) * c, :], (c, LANE))
                                 for k in range(RWKV_GROUP)], axis=0)
        at = -kn * jnp.exp(gcum - lw)
        rt = r * jnp.exp(gcum)
        inv = jnp.exp(-gcum)
        bt = kn * al * inv
        kt = kf * inv
        dec_end = jnp.exp(g_end)
        at_st, rt_st, v_st = stack(at), stack(rt), stack(v)
        b2_st, k2_st = stack(bt * dec_end), stack(kt * dec_end)
        big = _mm(jnp.concatenate([at_st, rt_st], axis=0), jnp.concatenate([twice(bt), twice(kt)], axis=0), _NT)
        a_ab = jnp.where(strict, big[0:n, 0:n], 0.0)
        a_ak = jnp.where(strict, big[0:n, n:2 * n], 0.0)
        a_rb = jnp.where(incl, big[n:2 * n, 0:n], 0.0)
        a_rk = jnp.where(incl, big[n:2 * n, n:2 * n], 0.0)
        x = jnp.where(eye, 1.0, 0.0) + a_ab
        yield
        pw = _mm(a_ab, a_ab)
        yield
        steps = int(np.log2(c)) - 1
        for s in range(steps):
            if s < steps - 1:
                both = _mm(pw, jnp.concatenate([pw, x], axis=1))
                x = x + both[:, n:]
                pw = both[:, :n]
            else:
                x = x + _mm(pw, x)
            yield
        av_st = _mm(a_ak, v_st)
        yield
        wu_uv = _mm(x, jnp.concatenate([at_st, av_st], axis=1))
        wu_st, uv_st = wu_uv[:, :LANE], wu_uv[:, LANE:]
        yield
        rb = _mm(a_rb, wu_uv)
        r2_st = rt_st + rb[:, :LANE]
        yv_st = rb[:, LANE:] + _mm(a_rk, v_st)
        yield
        trans, add = [], []
        for k in range(RWKV_GROUP):
            blk = slice(2 * k * c, 2 * (k + 1) * c)
            trans.append(jnp.where(eye_l, dec_end[k * c:k * c + 1, :], 0.0) + _mm(b2_st[blk], wu_st[blk], _TN))
            add.append(_mm(jnp.concatenate([b2_st[blk], k2_st[blk]], axis=0),
                           jnp.concatenate([uv_st[blk], v_st[blk]], axis=0), _TN))
        return rows, ln, r2_st, yv_st, trans, add, bonus_ref[rows, ln], g_ref[rows, ln]

    def finish(j, parts):
        hmat = h_s[j]
        outs = []
        for rows, ln, r2_st, yv_st, trans, add, bonus, gate in parts:
            ys = []
            for k in range(RWKV_GROUP):
                blk = slice(2 * k * c, 2 * (k + 1) * c)
                y_st = _mm(r2_st[blk], hmat) + yv_st[blk]
                hmat = _mm(trans[k], hmat) + add[k]
                ys.append(y_st[:c, :] + y_st[c:, :])
                yield
            outs.append((rows, ln, ys, bonus, gate))
        h_s[j] = hmat
        for rows, ln, ys, bonus, gate in outs:
            y = ys[0] if len(ys) == 1 else jnp.concatenate(ys, axis=0)
            mean = _group_sum(y, gmat) * (1.0 / HEAD_DIM)
            dev = y - mean
            var = _group_sum(dev * dev, gmat) * (1.0 / HEAD_DIM)
            yn = dev * lax.rsqrt(var + GN_EPS) * lnw_ref[:, ln] + lnb_ref[:, ln]
            y_ref[rows, ln] = (yn + bonus.astype(F32)) * gate.astype(F32)

    def group(gi, carry):
        parts = _lockstep([intra(gi * SCAN_AHEAD + a, j) for a in range(SCAN_AHEAD) for j in range(n_inst)])
        _lockstep([finish(j, [parts[a * n_inst + j] for a in range(SCAN_AHEAD)]) for j in range(n_inst)])
        return carry

    lax.fori_loop(0, tt // gc // SCAN_AHEAD, group, 0)

    @pl.when(ti == pl.num_programs(2) - 1)
    def _():
        hT_ref[...] = h_s[...]


def _rwkv_scan(vecs, lnw, lnb, h0, b, t):
    pairs = RWKV_WIDTH // LANE
    tt = min(t, SCAN_ROWS)
    nt = t // tt
    ni = RWKV_INTERLEAVE
    seq = pl.BlockSpec((tt, ni * LANE), lambda bi, hp, ti: (bi * nt + ti, hp))
    vec = pl.BlockSpec((1, ni * LANE), lambda bi, hp, ti: (0, hp))
    st = pl.BlockSpec((None, ni, LANE, LANE), lambda bi, hp, ti: (bi, hp, 0, 0))
    return pl.pallas_call(
        _rwkv_scan_body,
        out_shape=(jax.ShapeDtypeStruct((b * t, RWKV_WIDTH), F32),
                   jax.ShapeDtypeStruct((b, pairs, LANE, LANE), F32)),
        grid=(b, pairs // ni, nt),
        in_specs=[seq] * 8 + [vec, vec, st],
        out_specs=(seq, st),
        scratch_shapes=[pltpu.VMEM((ni, LANE, LANE), F32)],
        compiler_params=_cparams(("arbitrary", "arbitrary", "arbitrary")),
        name="rwkv_scan",
    )(*vecs, lnw.reshape(1, RWKV_WIDTH), lnb.reshape(1, RWKV_WIDTH), h0)


def _rwkv_step_body(s_ref, lw_ref, kn_ref, al_ref, k_ref, r_ref, vgb_ref, lnw_ref, lnb_ref, y_ref, so_ref):
    s = s_ref[...]
    kn = kn_ref[...]
    sa = jnp.sum(s * (-kn), axis=-1, keepdims=True)
    vgb = vgb_ref[...]
    s_new = s * jnp.exp(lw_ref[...]) + sa * (kn * al_ref[...]) + vgb[:, :, 0:1] * k_ref[...]
    so_ref[...] = s_new
    y = jnp.sum(s_new * r_ref[...], axis=-1, keepdims=True)
    mean = jnp.mean(y, axis=1, keepdims=True)
    dev = y - mean
    var = jnp.mean(dev * dev, axis=1, keepdims=True)
    yn = dev * lax.rsqrt(var + GN_EPS) * lnw_ref[...] + lnb_ref[...]
    y_ref[...] = (yn + vgb[:, :, 2:3]) * vgb[:, :, 1:2]


def _rwkv_step(state, vecs, lnw, lnb):
    bs = state.shape[0]
    r, lw, kf, v, kn, al, g, bonus = vecs
    rowv = lambda a: a.reshape(bs, RWKV_HEADS, 1, HEAD_DIM)
    vgb = jnp.stack([v, g, bonus], axis=-1).reshape(bs, RWKV_HEADS, HEAD_DIM, 3)
    row_spec = pl.BlockSpec((None, RWKV_HEADS, 1, HEAD_DIM), lambda b: (b, 0, 0, 0))
    col_spec = pl.BlockSpec((None, RWKV_HEADS, HEAD_DIM, 1), lambda b: (b, 0, 0, 0))
    vgb_spec = pl.BlockSpec((None, RWKV_HEADS, HEAD_DIM, 3), lambda b: (b, 0, 0, 0))
    st_spec = pl.BlockSpec((None, RWKV_HEADS, HEAD_DIM, HEAD_DIM), lambda b: (b, 0, 0, 0))
    par_spec = pl.BlockSpec((RWKV_HEADS, HEAD_DIM, 1), lambda b: (0, 0, 0))
    y, s_new = pl.pallas_call(
        _rwkv_step_body,
        out_shape=(jax.ShapeDtypeStruct((bs, RWKV_HEADS, HEAD_DIM, 1), F32),
                   jax.ShapeDtypeStruct(state.shape, F32)),
        grid=(bs,),
        in_specs=[st_spec] + [row_spec] * 5 + [vgb_spec] + [par_spec] * 2,
        out_specs=(col_spec, st_spec),
        compiler_params=_cparams(("arbitrary",)),
        name="rwkv_step",
    )(state, rowv(lw), rowv(kn), rowv(al), rowv(kf), rowv(r), vgb,
      lnw.reshape(RWKV_HEADS, HEAD_DIM, 1), lnb.reshape(RWKV_HEADS, HEAD_DIM, 1))
    return y.reshape(bs, RWKV_WIDTH), s_new


def _outproj_body(a_ref, r_ref, x_ref, gt_ref, w_ref, g2_ref, sc_ref, sh_ref, x1_ref, h2_ref):
    y = _dot(_bf(a_ref[...]), w_ref[0:ATTN_WIDTH, :]) + _dot(_bf(r_ref[...]), w_ref[ATTN_WIDTH:, :])
    x1 = x_ref[...] + gt_ref[...] * y
    x1_ref[...] = x1
    h2_ref[...] = _modulated_norm(x1, g2_ref[...], sc_ref[...], sh_ref[...])


def _outproj(attn, rw, x2d, gt, w_bf, g2, sc, sh, tm, tiles_per_group):
    n = x2d.shape[0]
    r = gt.shape[1]
    half = pl.BlockSpec((tm, ATTN_WIDTH), lambda i: (i, 0))
    full = pl.BlockSpec((tm, D_MODEL), lambda i: (i, 0))
    mod = pl.BlockSpec((None, r, D_MODEL), lambda i: (i // tiles_per_group, 0, 0))
    return pl.pallas_call(
        _outproj_body,
        out_shape=(jax.ShapeDtypeStruct((n, D_MODEL), F32), jax.ShapeDtypeStruct((n, D_MODEL), F32)),
        grid=(n // tm,),
        in_specs=[half, half, full, mod, pl.BlockSpec((D_MODEL, D_MODEL), lambda i: (0, 0)),
                  pl.BlockSpec((1, D_MODEL), lambda i: (0, 0)), mod, mod],
        out_specs=(full, full),
        compiler_params=_cparams(("arbitrary",)),
        name="outproj",
    )(attn, rw, x2d, gt, w_bf, g2.reshape(1, D_MODEL), sc, sh)


def _staircase():
    pairs = [(a, b) for a in range(PEER_TOPK) for b in range(PEER_TOPK) if (a + 1) * (b + 1) <= PEER_TOPK]
    return pairs


_CAND = _staircase()
_CAND_ROWS = -(-len(_CAND) // 8) * 8


_SENTINEL = 2.0 ** 100


def _extract_top(s, count, rows, tie_safe):
    vals = []
    limit = float(s.shape[0])
    for kth in range(count):
        mx = jnp.max(s, axis=0, keepdims=True)
        if tie_safe:
            hit = rows == jnp.min(jnp.where(s == mx, rows, limit), axis=0, keepdims=True)
        else:
            hit = s == mx
        s = jnp.where(hit, _rank_mark(kth), s)
        vals.append(mx)
        yield
    return vals, s


def _rank_mark(kth):
    return -_SENTINEL * (1.0 + kth / 16.0)


def _was_taken(s):
    return s < -0.5 * _SENTINEL


def _decode_rank(s):
    return jnp.where(_was_taken(s), (s * (-1.0 / _SENTINEL) - 1.0) * 16.0, 99.0)


def _router_body(h2_ref, wq_ref, keys_ref, oh_ref, h2t_ref, cnt_ref, e1_ref, rk_ref, e2_ref, q_s):
    h2 = h2_ref[...]
    tm = h2.shape[0]
    h2t_ref[...] = _bf(h2.T)
    q_s[...] = _dot(wq_ref[...], _bf(h2), _NT)
    rows = lax.broadcasted_iota(jnp.int32, (PEER_KEYS, LANE), 0).astype(F32)
    crow = lax.broadcasted_iota(jnp.int32, (_CAND_ROWS, LANE), 0).astype(F32)

    def route(h, tie_safe):
        scores = []
        for half in range(2):
            g = 2 * h + half
            q_t = q_s[pl.ds(pl.multiple_of(g * PEER_KEYS, PEER_KEYS), PEER_KEYS), :]
            scores.append(_mm3(keys_ref[g], q_t))
        wrong = jnp.zeros((1, LANE), F32)
        for sub in range(tm // LANE):
            sl = slice(sub * LANE, (sub + 1) * LANE)
            s0, s1 = scores[0][:, sl], scores[1][:, sl]
            (tops0, fin0), (tops1, fin1) = yield from _lockstep_gen(
                [_extract_top(s0, PEER_TOPK, rows, tie_safe), _extract_top(s1, PEER_TOPK, rows, tie_safe)])
            cand = [tops0[a] + tops1[b] for a, b in _CAND]
            cand += [jnp.full((1, LANE), NEG, F32)] * (_CAND_ROWS - len(cand))
            cand = jnp.concatenate(cand, axis=0)
            top = tops0[0] + tops1[0]
            taken = jnp.zeros((_CAND_ROWS, LANE), F32)
            zsum = jnp.zeros((1, LANE), F32)
            for _ in range(PEER_TOPK):
                mx = jnp.max(cand, axis=0, keepdims=True)
                if tie_safe:
                    hit = crow == jnp.min(jnp.where(cand == mx, crow, float(_CAND_ROWS)), axis=0, keepdims=True)
                else:
                    hit = cand == mx
                taken = jnp.where(hit, 1.0, taken)
                cand = jnp.where(hit, NEG, cand)
                zsum = zsum + jnp.exp(mx - top)
                yield
            per_rank = _dot(oh_ref[...], _bf(taken))
            cnt = jnp.full((PEER_KEYS, LANE), -0.5, F32)
            for kth in range(PEER_TOPK):
                cnt = jnp.where(fin0 == _rank_mark(kth), per_rank[kth:kth + 1, :] - 0.5, cnt)
            in0, in1 = _was_taken(fin0), _was_taken(fin1)
            cnt_ref[h, :, sl] = cnt
            rk_ref[h, :, sl] = _bf(_decode_rank(fin1))
            e1_ref[h, :, sl] = jnp.where(in0, jnp.exp(s0 - tops0[0]), 0.0)
            e2_ref[h, :, sl] = _bf(jnp.where(in1, jnp.exp(s1 - tops1[0]) / zsum, 0.0))
            for mask in (jnp.where(in0, 1.0, 0.0), jnp.where(in1, 1.0, 0.0), taken):
                wrong = wrong + jnp.abs(jnp.sum(mask, axis=0, keepdims=True) - float(PEER_TOPK))
        return wrong

    def heads(tie_safe):
        def body(hi, wrong):
            res = _lockstep([route(hi * ROUTER_INTERLEAVE + k, tie_safe) for k in range(ROUTER_INTERLEAVE)])
            return wrong + sum(res)
        return lax.fori_loop(0, PEER_HEADS // ROUTER_INTERLEAVE, body, jnp.zeros((1, LANE), F32))

    wrong = heads(False)

    @pl.when(jnp.max(wrong) > 0.0)
    def _():
        heads(True)


def _router(h2, wq_t_bf, keys, tm):
    n = h2.shape[0]
    onehot = np.zeros((PEER_TOPK, _CAND_ROWS), np.float32)
    for idx, (a, _) in enumerate(_CAND):
        onehot[a, idx] = 1.0
    tok = pl.BlockSpec((None, PEER_HEADS, PEER_KEYS, tm), lambda i: (i, 0, 0, 0))
    tok_shape = lambda dt: jax.ShapeDtypeStruct((n // tm, PEER_HEADS, PEER_KEYS, tm), dt)
    return pl.pallas_call(
        _router_body,
        out_shape=(jax.ShapeDtypeStruct((D_MODEL, n), BF16),
                   tok_shape(F32), tok_shape(F32), tok_shape(BF16), tok_shape(BF16)),
        grid=(n // tm,),
        in_specs=[pl.BlockSpec((tm, D_MODEL), lambda i: (i, 0)),
                  pl.BlockSpec((D_MODEL, D_MODEL), lambda i: (0, 0)),
                  pl.BlockSpec((2 * PEER_HEADS, PEER_KEYS, PEER_KEYS), lambda i: (0, 0, 0)),
                  pl.BlockSpec((PEER_TOPK, _CAND_ROWS), lambda i: (0, 0))],
        out_specs=(pl.BlockSpec((D_MODEL, tm), lambda i: (0, i)), tok, tok, tok, tok),
        scratch_shapes=[pltpu.VMEM((D_MODEL, tm), F32)],
        compiler_params=_cparams(("arbitrary",)),
        name="peer_router",
    )(h2, wq_t_bf, keys, jnp.asarray(onehot, BF16))


def _experts_body(h2t_ref, u_ref, v_ref, cnt_ref, e1_ref, rk_ref, e2_ref, x1_ref, gt_ref, y_ref, p_s):
    j = pl.program_id(1)
    last = pl.num_programs(1) - 1
    eb = u_ref.shape[0]
    per = eb // PEER_KEYS
    cur = j % 2

    @pl.when(j == 0)
    def _():
        y_ref[...] = jnp.zeros(y_ref.shape, F32)
        p_s[1] = jnp.zeros(p_s.shape[1:], BF16)

    @pl.when(j < last)
    def _():
        y_ref[...] += _dot(p_s[1 - cur], v_ref[...], _TN)
        h2t = h2t_ref[...]
        for ii in range(per):
            sub = slice(ii * PEER_KEYS, (ii + 1) * PEER_KEYS)
            i = j * per + ii
            gates = []
            for part in range(cnt_ref.shape[0]):
                wt = None
                for h in range(PEER_HEADS):
                    partners = _bf(cnt_ref[part, h, pl.ds(i, 1), :])
                    e1 = _bf(e1_ref[part, h, pl.ds(i, 1), :])
                    term = jnp.where(rk_ref[part, h] < partners, e2_ref[part, h], 0.0) * e1
                    wt = term if wt is None else wt + term
                gates.append(wt)
            wt = gates[0] if len(gates) == 1 else jnp.concatenate(gates, axis=1)
            act_in = _dot(u_ref[sub, :], h2t)
            act = 0.5 * act_in * (1.0 + lax.erf(act_in * (2.0 ** -0.5)))
            p_s[cur, sub, :] = wt * _bf(act)

    @pl.when(j == last)
    def _():
        acc = y_ref[...] + _dot(p_s[1 - cur], v_ref[...], _TN)
        y_ref[...] = x1_ref[...] + gt_ref[...] * acc


def _experts(h2t, u_bf, v_bf, route, x1, gt, tm, tiles_per_group):
    n = x1.shape[0]
    eb = EXPERT_BLOCK
    nblk = PEER_EXPERTS // eb
    r = gt.shape[1]
    slab = route[0].shape[-1]
    tok = pl.BlockSpec((tm // slab, PEER_HEADS, PEER_KEYS, slab), lambda i, j: (i, 0, 0, 0))
    return pl.pallas_call(
        _experts_body,
        out_shape=jax.ShapeDtypeStruct((n, D_MODEL), F32),
        grid=(n // tm, nblk + 1),
        in_specs=[pl.BlockSpec((D_MODEL, tm), lambda i, j: (0, i)),
                  pl.BlockSpec((eb, D_MODEL), lambda i, j: (jnp.minimum(j, nblk - 1), 0)),
                  pl.BlockSpec((eb, D_MODEL), lambda i, j: (jnp.maximum(j - 1, 0), 0)),
                  tok, tok, tok, tok,
                  pl.BlockSpec((tm, D_MODEL), lambda i, j: (i, 0)),
                  pl.BlockSpec((None, r, D_MODEL), lambda i, j: (i // tiles_per_group, 0, 0))],
        out_specs=pl.BlockSpec((tm, D_MODEL), lambda i, j: (i, 0)),
        scratch_shapes=[pltpu.VMEM((2, eb, tm), BF16)],
        compiler_params=_cparams(("arbitrary", "arbitrary")),
        name="peer_experts",
    )(h2t, u_bf, v_bf, *route, x1, gt)


def _pad_rows(a, rows):
    return jnp.pad(a, ((0, rows - a.shape[0]),) + ((0, 0),) * (a.ndim - 1))


def _pairs_to_state(hp):
    b = hp.shape[0]
    h0 = hp[:, :, :HEAD_DIM, :HEAD_DIM]
    h1 = hp[:, :, HEAD_DIM:, HEAD_DIM:]
    h = jnp.stack([h0, h1], axis=2).reshape(b, RWKV_HEADS, HEAD_DIM, HEAD_DIM)
    return jnp.swapaxes(h, -1, -2)


def kernel(x_prompt, x_sample, cache_attn_k, cache_attn_v, state_rwkv, state_rwkv_shift, c_prompt, c_sample, w_ada, b_ada, norm1_g, norm2_g, w_in, q_gain, k_gain, rwkv_mu, rwkv_w0, rwkv_w2, rwkv_a0, rwkv_a2, rwkv_g2, rwkv_k_k, rwkv_k_a, rwkv_r_k, rwkv_lnx_w, rwkv_lnx_b, w_out, peer_w_q, peer_sub_keys, peer_u, peer_v):
    b, t, _ = x_prompt.shape
    bs = x_sample.shape[0]
    n = b * t
    heads = ATTN_WIDTH // HEAD_DIM
    win = min(CACHE_LEN, t)

    w_in_bf = w_in.astype(BF16)
    w_out_bf = w_out.astype(BF16)
    wq_t_bf = peer_w_q.T.astype(BF16)
    keys = peer_sub_keys.reshape(2 * PEER_HEADS, PEER_KEYS, PEER_KEYS)
    u_bf = peer_u.astype(BF16)
    v_bf = peer_v.astype(BF16)
    row = lambda a: a.reshape(1, -1)
    wa = jnp.zeros((LANE, 2 * RWKV_WIDTH), F32)
    wa = wa.at[:DECAY_RANK, :RWKV_WIDTH].set(rwkv_w2).at[DECAY_RANK:DECAY_RANK + A_RANK, RWKV_WIDTH:].set(rwkv_a2)
    g2 = jnp.zeros((LORA_PAD - LANE, RWKV_WIDTH), F32).at[:GATE_RANK].set(rwkv_g2)
    hi_lo = lambda w: (w.astype(BF16), (w - w.astype(BF16).astype(F32)).astype(BF16))
    rwkv_params = (row(jnp.pad(rwkv_mu, (0, P_PAD - RWKV_PROJ))), row(rwkv_w0), row(rwkv_a0), row(rwkv_k_k),
                   row(rwkv_k_a), row(rwkv_r_k), *hi_lo(wa), *hi_lo(g2))

    c_rows = b + bs
    c_pad = -(-c_rows // 8) * 8
    mod = _adaln(_pad_rows(jnp.concatenate([c_prompt, c_sample], axis=0), c_pad), w_ada, b_ada)
    mod_p = [m.reshape(b, 1, D_MODEL) for m in jnp.split(mod[:b], 6, axis=-1)]
    mod_s = [m.reshape(1, bs, D_MODEL) for m in jnp.split(mod[b:c_rows], 6, axis=-1)]

    tm_p = min(TM_INPROJ, t)
    xp = x_prompt.reshape(n, D_MODEL)
    proj_p = _inproj(xp, norm1_g, mod_p[1], mod_p[0], w_in_bf, tm_p, t // tm_p)
    attn_p, k_cache, v_cache = _attn_prompt(proj_p.reshape(b, t, PROJ_PAD), q_gain, k_gain, win)
    tm_r = min(TM_PREP, t)
    shift0 = jnp.zeros((b, 1, P_PAD), F32)
    vecs_p = _rwkv_prep(True, proj_p, shift0, rwkv_params, tm_r, t // tm_r)
    rw_p, h_fin = _rwkv_scan(vecs_p, rwkv_lnx_w, rwkv_lnx_b, jnp.zeros((b, RWKV_HEADS // 2, LANE, LANE), F32), b, t)
    tm_o = min(TM_OUTPROJ, t)
    x1_p, h2_p = _outproj(attn_p.reshape(n, ATTN_WIDTH), rw_p, xp, mod_p[2], w_out_bf, norm2_g,
                          mod_p[4], mod_p[3], tm_o, t // tm_o)
    tm_q = min(TM_ROUTE, t)
    tm_e = min(TM_EXPERT, t)
    h2t_p, *route_p = _router(h2_p, wq_t_bf, keys, tm_q)
    y_p = _experts(h2t_p, u_bf, v_bf, route_p, x1_p, mod_p[5], tm_e, t // tm_e)

    xs = x_sample.reshape(bs, D_MODEL)
    proj_s = _inproj(xs, norm1_g, mod_s[1], mod_s[0], w_in_bf, bs, 1)
    q_s = proj_s[:, P_PAD:P_PAD + ATTN_WIDTH]
    k_s = proj_s[:, P_PAD + ATTN_WIDTH:P_PAD + 2 * ATTN_WIDTH]
    v_s = proj_s[:, P_PAD + 2 * ATTN_WIDTH:]
    attn_s, kn_s = _attn_sample(q_s, k_s, v_s, q_gain, k_gain, cache_attn_k, cache_attn_v)
    prev_s = jnp.pad(state_rwkv_shift.reshape(bs, RWKV_PROJ), ((0, 0), (0, P_PAD - RWKV_PROJ)))
    vecs_s = _rwkv_prep(False, proj_s, prev_s, rwkv_params, bs, 1)
    rw_s, state_s = _rwkv_step(state_rwkv, vecs_s, rwkv_lnx_w, rwkv_lnx_b)
    x1_s, h2_s = _outproj(attn_s, rw_s, xs, mod_s[2], w_out_bf, norm2_g, mod_s[4], mod_s[3], bs, 1)
    ns = -(-bs // LANE) * LANE
    h2t_s, *route_s = _router(_pad_rows(h2_s, ns), wq_t_bf, keys, LANE)
    gt2_s = _pad_rows(mod_s[5][0], ns).reshape(ns // LANE, LANE, D_MODEL)
    y_s = _experts(h2t_s, u_bf, v_bf, route_s, _pad_rows(x1_s, ns), gt2_s, LANE, 1)[:bs]

    return (y_p.reshape(b, t, D_MODEL), y_s.reshape(bs, 1, D_MODEL),
            k_cache.reshape(b, win, heads, HEAD_DIM), v_cache.reshape(b, win, heads, HEAD_DIM),
            _pairs_to_state(h_fin), proj_p.reshape(b, t, PROJ_PAD)[:, t - 1:, :RWKV_PROJ],
            kn_s.reshape(bs, 1, heads, HEAD_DIM), v_s.reshape(bs, 1, heads, HEAD_DIM),
            state_s, proj_s[:, :RWKV_PROJ].reshape(bs, 1, RWKV_PROJ))
```
